```python
import math
import jax
import jax.numpy as jnp
from jax import lax
import numpy as np

D_MODEL = 4096
BATCH = 2
SEQ = 8192
DEPTH = 1

HEAD_DIM = 128
HEADS_PER_GROUP = 8
DILATED_GROUPS = ((128, 1), (512, 4), (2048, 16))
N_GROUPS = len(DILATED_GROUPS)
N_ATTN_HEADS = N_GROUPS * HEADS_PER_GROUP
ATTN_WIDTH = N_ATTN_HEADS * HEAD_DIM
ATTN_OUT_WIDTH = HEADS_PER_GROUP * HEAD_DIM
HYENA_WIDTH = D_MODEL // 4
HYENA_SHORT = 3
HYENA_FILTER_WIDTH = 64
HYENA_N_BANDS = 16
HYENA_EMB_DIM = 1 + 2 * HYENA_N_BANDS
HYENA_DECAY_TARGET = 1e-2
HYENA_FAST_DECAY_PCT = 0.3
HYENA_SLOW_DECAY_PCT = 1.5
N_EXPERTS = 32
TOP_K = 4
EXPERT_WIDTH = D_MODEL // 4
SWIGLU_LIMIT = 7.0
SWIGLU_ALPHA = 1.702
MOE_BLOCK = 512
RMS_EPS = 1e-6
NEG_INF = -1e30
N_IN_COLS = 3 * HYENA_WIDTH + 3 * ATTN_WIDTH + 2 * D_MODEL
ADA_COLS = 6 * D_MODEL

kernel_name = 'hybrid_hyena_dilated_attn_moe_block'


def _rms_norm(x, g):
    xf = x.astype(jnp.float32)
    y = xf * lax.rsqrt(jnp.mean(xf * xf, axis=-1, keepdims=True) + RMS_EPS)
    return (y * g.astype(jnp.float32)).astype(x.dtype)


def _alibi_slopes(n_heads):
    def pow2_slopes(m):
        start = 2.0 ** (-8.0 / m)
        return [start ** (i + 1) for i in range(m)]
    base = 2 ** int(math.floor(math.log2(n_heads)))
    slopes = pow2_slopes(base)
    if base < n_heads:
        slopes = slopes + pow2_slopes(2 * base)[0::2][: n_heads - base]
    return np.array(sorted(slopes, reverse=True), dtype=np.float32)


def _hyena_filters(length, w1, b1, w2, b2, w3, b3, freq, wout):
    f32 = jnp.float32
    t = jnp.linspace(0.0, 1.0, length, dtype=f32)[:, None]
    bands = jnp.linspace(1e-4, HYENA_N_BANDS - 1, HYENA_N_BANDS, dtype=f32)[None, :]
    ang = (2.0 * math.pi / length) * jnp.arange(length, dtype=f32)[:, None] * bands
    z = jnp.concatenate([t, jnp.cos(ang), -jnp.sin(ang)], axis=-1)
    fr = freq.astype(f32)
    hdn = jnp.sin(fr * (z @ w1.astype(f32) + b1.astype(f32)))
    hdn = jnp.sin(fr * (hdn @ w2.astype(f32) + b2.astype(f32)))
    hdn = jnp.sin(fr * (hdn @ w3.astype(f32) + b3.astype(f32)))
    filt = hdn @ wout.astype(f32)
    min_decay = math.log(HYENA_DECAY_TARGET) / HYENA_FAST_DECAY_PCT
    max_decay = math.log(HYENA_DECAY_TARGET) / HYENA_SLOW_DECAY_PCT
    deltas = jnp.abs(jnp.linspace(min_decay, max_decay, HYENA_WIDTH, dtype=f32))[None, :]
    decay = jnp.exp(-t * deltas)
    return filt[:, :HYENA_WIDTH] * decay, filt[:, HYENA_WIDTH:] * decay


def _hyena(u, conv_w, conv_b, skip, h_fwd, h_bwd):
    f32 = jnp.float32
    length = u.shape[1]
    up = jnp.pad(u, ((0, 0), (1, 1), (0, 0)))
    uc = conv_w[0] * up[:, :-2] + conv_w[1] * up[:, 1:-1] + conv_w[2] * up[:, 2:] + conv_b
    x1, x2, v = jnp.split(uc, 3, axis=-1)
    zin = (v * x2).astype(f32)
    k2 = jnp.concatenate([h_fwd, jnp.zeros((1, HYENA_WIDTH), f32), h_bwd[1:][::-1]], axis=0)
    kf = jnp.fft.rfft(k2, axis=0)
    zf = jnp.fft.rfft(zin, n=2 * length, axis=1)
    y = jnp.fft.irfft(zf * kf[None], n=2 * length, axis=1)[:, :length]
    y = y + zin * skip.astype(f32)
    return (y * x1.astype(f32)).astype(u.dtype)


def _dilated_window_attention(q, k, v, slopes, window, dilation):
    f32 = jnp.float32
    b, s, h, e = q.shape
    side = window // (2 * dilation)
    n = s // dilation
    nb = -(-n // side)
    n_pad = nb * side

    def to_residue(a):
        return a.reshape(b, n, dilation, h, e).transpose(0, 2, 3, 1, 4)

    qr = jnp.pad(to_residue(q), ((0, 0), (0, 0), (0, 0), (0, n_pad - n), (0, 0)))
    kpad = ((0, 0), (0, 0), (0, 0), (side, side + n_pad - n), (0, 0))
    kr = jnp.pad(to_residue(k), kpad)
    vr = jnp.pad(to_residue(v), kpad)
    qb = qr.reshape(b, dilation, h, nb, side, e)

    def band(a):
        return jnp.concatenate(
            [a[..., j * side: j * side + n_pad, :].reshape(b, dilation, h, nb, side, e) for j in range(3)],
            axis=-2)

    kb, vb = band(kr), band(vr)
    scores = jnp.einsum('brhnqe,brhnke->brhnqk', qb.astype(f32), kb.astype(f32)) * (e ** -0.5)
    qi = jnp.arange(nb)[:, None, None] * side + jnp.arange(side)[None, :, None]
    ki = jnp.arange(nb)[:, None, None] * side + jnp.arange(3 * side)[None, None, :] - side
    rel = jnp.abs(ki - qi)
    valid = (rel <= side) & (ki >= 0) & (ki < n)
    alibi = -slopes.astype(f32)[:, None, None, None] * (rel * dilation).astype(f32)
    scores = jnp.where(valid, scores + alibi, NEG_INF)
    m = jnp.max(scores, axis=-1, keepdims=True)
    p = jnp.exp(scores - m)
    denom = jnp.sum(p, axis=-1, keepdims=True)
    o = jnp.einsum('brhnqk,brhnke->brhnqe', p, vb.astype(f32)) / denom
    lse = (m + jnp.log(denom))[..., 0]
    o = o.reshape(b, dilation, h, n_pad, e)[:, :, :, :n]
    lse = lse.reshape(b, dilation, h, n_pad)[:, :, :, :n]
    o = o.transpose(0, 3, 1, 2, 4).reshape(b, s, h, e)
    lse = lse.transpose(0, 3, 1, 2).reshape(b, s, h)
    return o, lse


def _dilated_mixture(q, k, v, slopes):
    b, s, _ = q.shape
    q = q.reshape(b, s, N_GROUPS, HEADS_PER_GROUP, HEAD_DIM)
    k = k.reshape(b, s, N_GROUPS, HEADS_PER_GROUP, HEAD_DIM)
    v = v.reshape(b, s, N_GROUPS, HEADS_PER_GROUP, HEAD_DIM)
    outs, lses = [], []
    for gi, (window, dilation) in enumerate(DILATED_GROUPS):
        o, lse = _dilated_window_attention(q[:, :, gi], k[:, :, gi], v[:, :, gi], slopes[gi], window, dilation)
        outs.append(o)
        lses.append(lse)
    alpha = jax.nn.softmax(jnp.stack(lses, axis=0), axis=0)
    o = jnp.sum(alpha[..., None] * jnp.stack(outs, axis=0), axis=0)
    return o.reshape(b, s, ATTN_OUT_WIDTH).astype(q.dtype)


def _moe(h, w_router, b_router, w_gate, b_gate, w_up, b_up, w_down, b_down):
    f32 = jnp.float32
    b, s, d = h.shape
    xt = h.reshape(b * s, d)
    n_assign = b * s * TOP_K
    logits = (xt @ w_router + b_router).astype(f32)
    top_logits, top_idx = lax.top_k(logits, TOP_K)
    top_w = jax.nn.softmax(top_logits, axis=-1)
    flat_e = top_idx.reshape(-1).astype(jnp.int32)
    order = jnp.argsort(flat_e)
    sorted_e = flat_e[order]
    sorted_tok = (order // TOP_K).astype(jnp.int32)
    sorted_w = top_w.reshape(-1)[order]
    sizes = jnp.bincount(flat_e, length=N_EXPERTS).astype(jnp.int32)
    starts = jnp.cumsum(sizes) - sizes
    padded = (sizes + MOE_BLOCK - 1) // MOE_BLOCK * MOE_BLOCK
    pad_end = jnp.cumsum(padded)
    pad_start = pad_end - padded
    dest = pad_start[sorted_e] + jnp.arange(n_assign, dtype=jnp.int32) - starts[sorted_e]
    n_rows = -(-n_assign // MOE_BLOCK) * MOE_BLOCK + N_EXPERTS * MOE_BLOCK
    n_blocks = n_rows // MOE_BLOCK
    row_tok = jnp.zeros((n_rows,), jnp.int32).at[dest].set(sorted_tok)
    row_w = jnp.zeros((n_rows,), f32).at[dest].set(sorted_w)
    block_e = jnp.minimum(
        jnp.searchsorted(pad_end, jnp.arange(n_blocks, dtype=jnp.int32) * MOE_BLOCK, side='right'),
        N_EXPERTS - 1)

    def expert_block(acc, blk):
        tok, wts, e = blk
        xb = xt[tok]
        g = jnp.minimum(xb @ w_gate[e] + b_gate[e], SWIGLU_LIMIT)
        u = jnp.clip(xb @ w_up[e] + b_up[e], -SWIGLU_LIMIT, SWIGLU_LIMIT)
        y = (g * jax.nn.sigmoid(SWIGLU_ALPHA * g) * (u + 1.0)) @ w_down[e] + b_down[e]
        return acc.at[tok].add((y * wts[:, None].astype(y.dtype)).astype(acc.dtype)), None

    out, _ = lax.scan(expert_block, jnp.zeros_like(xt),
                      (row_tok.reshape(n_blocks, MOE_BLOCK), row_w.reshape(n_blocks, MOE_BLOCK), block_e))
    return out.reshape(b, s, d)


def setup_inputs(seed: int = 0) -> dict:
    key = jax.random.key(seed)
    ks = iter(jax.random.split(key, 40))
    f32 = jnp.float32

    def nrm(shape, std):
        return std * jax.random.normal(next(ks), shape, f32)

    L, d, hw, F, E = DEPTH, D_MODEL, HYENA_WIDTH, EXPERT_WIDTH, N_EXPERTS
    fw = HYENA_FILTER_WIDTH
    return {
        'x': nrm((BATCH, SEQ, d), 1.0),
        'c': nrm((BATCH, d), 1.0),
        'w_ada': nrm((L, d, ADA_COLS), 0.2 * d ** -0.5),
        'b_ada': nrm((L, ADA_COLS), 0.02),
        'g_pre_mix': 1.0 + nrm((L, d), 0.05),
        'g_post_mix': 1.0 + nrm((L, d), 0.05),
        'g_pre_ffn': 1.0 + nrm((L, d), 0.05),
        'g_post_ffn': 1.0 + nrm((L, d), 0.05),
        'w_in': nrm((L, d, N_IN_COLS), d ** -0.5),
        'hy_conv_w': nrm((L, HYENA_SHORT, 3 * hw), 0.5),
        'hy_conv_b': nrm((L, 3 * hw), 0.02),
        'hy_skip': nrm((L, hw), 1.0),
        'hy_f_w1': nrm((L, HYENA_EMB_DIM, fw), HYENA_EMB_DIM ** -0.5),
        'hy_f_b1': nrm((L, fw), 0.1),
        'hy_f_w2': nrm((L, fw, fw), fw ** -0.5),
        'hy_f_b2': nrm((L, fw), 0.1),
        'hy_f_w3': nrm((L, fw, fw), fw ** -0.5),
        'hy_f_b3': nrm((L, fw), 0.1),
        'hy_f_freq': 1.0 + nrm((L, fw), 0.1),
        'hy_f_wout': nrm((L, fw, 2 * hw), 0.02),
        'w_proj_hyena': nrm((L, hw, d), hw ** -0.5),
        'w_proj_attn': nrm((L, ATTN_OUT_WIDTH, d), ATTN_OUT_WIDTH ** -0.5),
        'w_out': nrm((L, d, d), d ** -0.5),
        'w_router': nrm((L, d, E), d ** -0.5),
        'b_router': nrm((L, E), 0.01),
        'w_gate': nrm((L, E, d, F), d ** -0.5),
        'b_gate': nrm((L, E, F), 0.02),
        'w_up': nrm((L, E, d, F), d ** -0.5),
        'b_up': nrm((L, E, F), 0.02),
        'w_down': nrm((L, E, F, d), F ** -0.5),
        'b_down': nrm((L, E, d), 0.02),
    }


def reference(x, c, w_ada, b_ada, g_pre_mix, g_post_mix, g_pre_ffn, g_post_ffn, w_in,
              hy_conv_w, hy_conv_b, hy_skip, hy_f_w1, hy_f_b1, hy_f_w2, hy_f_b2, hy_f_w3, hy_f_b3,
              hy_f_freq, hy_f_wout, w_proj_hyena, w_proj_attn, w_out, w_router, b_router,
              w_gate, b_gate, w_up, b_up, w_down, b_down):
    s = x.shape[1]
    slopes = jnp.asarray(_alibi_slopes(N_ATTN_HEADS)).reshape(N_GROUPS, HEADS_PER_GROUP)
    splits = np.cumsum([3 * HYENA_WIDTH, ATTN_WIDTH, ATTN_WIDTH, ATTN_WIDTH, D_MODEL]).tolist()
    for l in range(DEPTH):
        mod = jax.nn.silu(c) @ w_ada[l] + b_ada[l]
        shift_m, scale_m, gate_m, shift_f, scale_f, gate_f = jnp.split(mod[:, None, :], 6, axis=-1)

        h = _rms_norm(x, g_pre_mix[l]) * (1.0 + scale_m) + shift_m
        proj = h @ w_in[l]
        u_hy, q, k, v, gate_hy, gate_at = jnp.split(proj, splits, axis=-1)
        h_fwd, h_bwd = _hyena_filters(s, hy_f_w1[l], hy_f_b1[l], hy_f_w2[l], hy_f_b2[l],
                                      hy_f_w3[l], hy_f_b3[l], hy_f_freq[l], hy_f_wout[l])
        hy = _hyena(u_hy, hy_conv_w[l], hy_conv_b[l], hy_skip[l], h_fwd, h_bwd)
        at = _dilated_mixture(q, k, v, slopes)
        merged = (jax.nn.sigmoid(gate_hy) * (hy @ w_proj_hyena[l])
                  + jax.nn.sigmoid(gate_at) * (at @ w_proj_attn[l]))
        x = x + gate_m * _rms_norm(merged @ w_out[l], g_post_mix[l])

        h = _rms_norm(x, g_pre_ffn[l]) * (1.0 + scale_f) + shift_f
        ff = _moe(h, w_router[l], b_router[l], w_gate[l], b_gate[l], w_up[l], b_up[l], w_down[l], b_down[l])
        x = x + gate_f * _rms_norm(ff, g_post_ffn[l])
    return x
```

```python
import functools
import math

import jax
import jax.numpy as jnp
import numpy as np
from jax import lax
from jax.experimental import pallas as pl
from jax.experimental.pallas import tpu as pltpu

F32 = jnp.float32
BF16 = jnp.bfloat16
HIGHEST = lax.Precision.HIGHEST

LANES = 128
HEAD_DIM = 128
HEADS_PER_GROUP = 8
DILATED_GROUPS = ((128, 1), (512, 4), (2048, 16))
N_GROUPS = len(DILATED_GROUPS)
GROUP_WIDTH = HEADS_PER_GROUP * HEAD_DIM
ATTN_SIDE = 64
TOP_K = 4
SWIGLU_LIMIT = 7.0
SWIGLU_ALPHA = 1.702
RMS_EPS = 1e-6
NEG_INF = -1e30
HYENA_N_BANDS = 16
HYENA_DECAY_TARGET = 1e-2
HYENA_FAST_DECAY_PCT = 0.3
HYENA_SLOW_DECAY_PCT = 1.5
COL_BLOCK = 1024
MOE_ROWS = 256
VMEM_LIMIT = 56 * 1024 * 1024


def _params(sem, vmem=VMEM_LIMIT):
    return pltpu.CompilerParams(dimension_semantics=sem, vmem_limit_bytes=vmem)


def _sds(shape, dtype):
    return jax.ShapeDtypeStruct(shape, dtype)


def _ada_kernel(c_ref, w_ref, b_ref, o_ref):
    c = c_ref[...]
    sc = (c * jax.nn.sigmoid(c)).astype(BF16)
    o_ref[...] = jnp.dot(sc, w_ref[...].astype(BF16), preferred_element_type=F32) + b_ref[...]


def _adaln(c8, w_ada, b_ada):
    d, cols = w_ada.shape
    tn = 512
    return pl.pallas_call(
        _ada_kernel,
        out_shape=_sds((8, cols), F32),
        grid=(cols // tn,),
        in_specs=[pl.BlockSpec((8, d), lambda j: (0, 0)),
                  pl.BlockSpec((d, tn), lambda j: (0, j)),
                  pl.BlockSpec((1, tn), lambda j: (0, j))],
        out_specs=pl.BlockSpec((8, tn), lambda j: (0, j)),
        compiler_params=_params(("arbitrary",)),
        name="adaln",
    )(c8, w_ada, b_ada)


def _prenorm_kernel(x_ref, g_ref, sc_ref, sh_ref, o_ref, o4_ref, o16_ref, scr_ref, *, ts, d_model):
    x = x_ref[...]
    ms = jnp.mean(x * x, axis=-1, keepdims=True)
    h = x * lax.rsqrt(ms + RMS_EPS) * g_ref[...] * (1.0 + sc_ref[...]) + sh_ref[...]
    o_ref[...] = h.astype(BF16)
    nct = d_model // LANES
    for c in range(nct):
        scr_ref[c] = h[:, c * LANES:(c + 1) * LANES]
    for dil, oref in ((4, o4_ref), (16, o16_ref)):
        for r in range(dil):
            for c in range(nct):
                oref[r, :, c * LANES:(c + 1) * LANES] = scr_ref[c, pl.ds(r, ts // dil, stride=dil), :].astype(BF16)


def _prenorm_mix(x, g, scale, shift):
    b, s, d = x.shape
    ts = 256
    kern = functools.partial(_prenorm_kernel, ts=ts, d_model=d)
    return pl.pallas_call(
        kern,
        out_shape=(_sds((b, s, d), BF16), _sds((b, 4, s // 4, d), BF16), _sds((b, 16, s // 16, d), BF16)),
        grid=(b, s // ts),
        in_specs=[pl.BlockSpec((None, ts, d), lambda bi, i: (bi, i, 0)),
                  pl.BlockSpec((1, d), lambda bi, i: (0, 0)),
                  pl.BlockSpec((None, 1, d), lambda bi, i: (bi, 0, 0)),
                  pl.BlockSpec((None, 1, d), lambda bi, i: (bi, 0, 0))],
        out_specs=(pl.BlockSpec((None, ts, d), lambda bi, i: (bi, i, 0)),
                   pl.BlockSpec((None, 4, ts // 4, d), lambda bi, i: (bi, 0, i, 0)),
                   pl.BlockSpec((None, 16, ts // 16, d), lambda bi, i: (bi, 0, i, 0))),
        scratch_shapes=[pltpu.VMEM((d // LANES, ts, LANES), F32)],
        compiler_params=_params(("arbitrary", "arbitrary")),
        name="prenorm_mix",
    )(x, g, scale, shift)


def _mm_kernel(tbl_ref, a_ref, w_ref, o_ref):
    del tbl_ref
    o_ref[...] = jnp.dot(a_ref[...], w_ref[...], preferred_element_type=F32).astype(o_ref.dtype)


def _matmul(a, w, col_blocks, out_dtype, name, tm=1024, tn=COL_BLOCK):
    m, k = a.shape
    tm = min(tm, m)
    nb = len(col_blocks)
    tbl = jnp.asarray(col_blocks, jnp.int32)
    grid_spec = pltpu.PrefetchScalarGridSpec(
        num_scalar_prefetch=1,
        grid=(m // tm, nb),
        in_specs=[pl.BlockSpec((tm, k), lambda i, j, t: (i, 0)),
                  pl.BlockSpec((k, tn), lambda i, j, t: (0, t[j]))],
        out_specs=pl.BlockSpec((tm, tn), lambda i, j, t: (i, j)),
    )
    return pl.pallas_call(
        _mm_kernel,
        out_shape=_sds((m, nb * tn), out_dtype),
        grid_spec=grid_spec,
        compiler_params=_params(("arbitrary", "arbitrary")),
        name=name,
    )(tbl, a, w)


def _filter_kernel(z_ref, w1_ref, b1_ref, w2_ref, b2_ref, w3_ref, b3_ref, fr_ref, wo_ref, dl_ref,
                   hf_ref, hb_ref, *, hw):
    def dot(a, b):
        return jnp.dot(a, b, precision=HIGHEST, preferred_element_type=F32)

    z = z_ref[...]
    fr = fr_ref[...]
    h = jnp.sin(fr * (dot(z, w1_ref[...]) + b1_ref[...]))
    h = jnp.sin(fr * (dot(h, w2_ref[...]) + b2_ref[...]))
    h = jnp.sin(fr * (dot(h, w3_ref[...]) + b3_ref[...]))
    filt = dot(h, wo_ref[...])
    decay = jnp.exp(-z[:, 0:1] * dl_ref[...])
    hf_ref[...] = filt[:, :hw] * decay
    hb_ref[...] = filt[:, hw:] * decay


def _hyena_filters(length, w1, b1, w2, b2, w3, b3, freq, wout):
    emb, fw = w1.shape
    hw = wout.shape[1] // 2
    t = np.linspace(0.0, 1.0, length)[:, None]
    bands = np.linspace(1e-4, HYENA_N_BANDS - 1, HYENA_N_BANDS)[None, :]
    ang = (2.0 * math.pi / length) * np.arange(length)[:, None] * bands
    z = np.concatenate([t, np.cos(ang), -np.sin(ang)], axis=-1)
    zpad = np.zeros((length, LANES), np.float32)
    zpad[:, :emb] = z
    w1p = jnp.zeros((LANES, fw), F32).at[:emb].set(w1)
    min_decay = math.log(HYENA_DECAY_TARGET) / HYENA_FAST_DECAY_PCT
    max_decay = math.log(HYENA_DECAY_TARGET) / HYENA_SLOW_DECAY_PCT
    deltas = np.abs(np.linspace(min_decay, max_decay, hw))[None, :].astype(np.float32)
    tl = min(1024, length)
    full = lambda shape: pl.BlockSpec(shape, lambda i: (0,) * len(shape))
    return pl.pallas_call(
        functools.partial(_filter_kernel, hw=hw),
        out_shape=(_sds((length, hw), F32), _sds((length, hw), F32)),
        grid=(length // tl,),
        in_specs=[pl.BlockSpec((tl, LANES), lambda i: (i, 0)),
                  full((LANES, fw)), full((1, fw)), full((fw, fw)), full((1, fw)),
                  full((fw, fw)), full((1, fw)), full((1, fw)), full((fw, 2 * hw)), full((1, hw))],
        out_specs=(pl.BlockSpec((tl, hw), lambda i: (i, 0)), pl.BlockSpec((tl, hw), lambda i: (i, 0))),
        compiler_params=_params(("arbitrary",)),
        name="hyena_filters",
    )(jnp.asarray(zpad), w1p, b1[None], w2, b2[None], w3, b3[None], freq[None], wout, jnp.asarray(deltas))


def _shortconv_kernel(u_ref, up_ref, un_ref, w_ref, b_ref, z_ref, x1_ref, *, ts, hw):
    i = pl.program_id(1)
    last = pl.num_programs(1) - 1
    u = u_ref[...].astype(F32)
    prev_blk = up_ref[...].astype(F32)
    next_blk = un_ref[...].astype(F32)
    prev_row = jnp.where(i > 0, prev_blk[15:16, :], 0.0)
    next_row = jnp.where(i < last, next_blk[0:1, :], 0.0)
    row = lax.broadcasted_iota(jnp.int32, u.shape, 0)
    um = jnp.where(row == 0, prev_row, pltpu.roll(u, 1, 0))
    up = jnp.where(row == ts - 1, next_row, pltpu.roll(u, ts - 1, 0))
    w = w_ref[...]
    uc = w[0:1] * um + w[1:2] * u + w[2:3] * up + b_ref[...]
    x1_ref[...] = uc[:, :hw]
    z_ref[...] = uc[:, 2 * hw:] * uc[:, hw:2 * hw]


def _shortconv(pn3, conv_w, conv_b, hw):
    b, s, _ = pn3.shape
    ts = 512
    w3 = 3 * hw
    nh = s // 16
    return pl.pallas_call(
        functools.partial(_shortconv_kernel, ts=ts, hw=hw),
        out_shape=(_sds((b, s, hw), F32), _sds((b, s, hw), F32)),
        grid=(b, s // ts),
        in_specs=[pl.BlockSpec((None, ts, w3), lambda bi, i: (bi, i, 0)),
                  pl.BlockSpec((None, 16, w3), lambda bi, i: (bi, jnp.maximum(i * (ts // 16) - 1, 0), 0)),
                  pl.BlockSpec((None, 16, w3), lambda bi, i: (bi, jnp.minimum((i + 1) * (ts // 16), nh - 1), 0)),
                  pl.BlockSpec((3, w3), lambda bi, i: (0, 0)),
                  pl.BlockSpec((1, w3), lambda bi, i: (0, 0))],
        out_specs=(pl.BlockSpec((None, ts, hw), lambda bi, i: (bi, i, 0)),
                   pl.BlockSpec((None, ts, hw), lambda bi, i: (bi, i, 0))),
        compiler_params=_params(("arbitrary", "arbitrary")),
        name="hyena_shortconv",
    )(pn3, pn3, pn3, conv_w, conv_b[None])


def _dft_tables(r):
    idx = np.arange(r)
    ang = 2.0 * np.pi * np.outer(idx, idx) / r
    cos, sin = np.cos(ang), np.sin(ang)
    fa_half = np.concatenate([cos[:, :r // 2], -sin[:, :r // 2]], axis=0)
    fa_full = np.concatenate([cos, -sin], axis=0)
    fbig = np.block([[cos, sin], [-sin, cos]])
    fconj = np.block([[cos, -sin], [sin, cos]])
    gfin = np.concatenate([cos[:r // 2], -sin[:r // 2]], axis=1)
    tang = 2.0 * np.pi * np.outer(idx, idx) / (r * r)
    tw = np.stack([np.cos(tang), -np.sin(tang)], axis=-1)
    f32 = lambda a: jnp.asarray(a.astype(np.float32))
    return f32(fa_half), f32(fa_full), f32(fbig), f32(fconj), f32(gfin), f32(tw)


def _fft_a_kernel(x_ref, f_ref, o_ref, *, r, k1):
    f = f_ref[...]

    def body(n2, carry):
        xs = x_ref[pl.ds(n2, k1, stride=r), :]
        a = jnp.dot(f, xs, precision=HIGHEST, preferred_element_type=F32)
        o_ref[0, pl.ds(n2, r, stride=r), :] = a[:r]
        o_ref[1, pl.ds(n2, r, stride=r), :] = a[r:]
        return carry

    lax.fori_loop(0, r, body, 0)


def _fft_stage_a(x, fmat, r):
    bx, rows, c = x.shape
    k1 = rows // r
    return pl.pallas_call(
        functools.partial(_fft_a_kernel, r=r, k1=k1),
        out_shape=_sds((bx, 2, r * r, c), F32),
        grid=(bx, c // LANES),
        in_specs=[pl.BlockSpec((None, rows, LANES), lambda b, ci: (b, 0, ci)),
                  pl.BlockSpec((2 * r, k1), lambda b, ci: (0, 0))],
        out_specs=pl.BlockSpec((None, 2, r * r, LANES), lambda b, ci: (b, 0, 0, ci)),
        compiler_params=_params(("arbitrary", "arbitrary")),
        name="fft_stage_a",
    )(x, fmat)


def _fft_mk_kernel(a_ref, tw_ref, fb_ref, o_ref, *, r, scale):
    are, aim = a_ref[0], a_ref[1]
    tr, ti = tw_ref[:, 0:1], tw_ref[:, 1:2]
    bcat = jnp.concatenate([are * tr - aim * ti, are * ti + aim * tr], axis=0)
    x = jnp.dot(fb_ref[...], bcat, precision=HIGHEST, preferred_element_type=F32) * scale
    o_ref[0] = x[:r]
    o_ref[1] = x[r:]


def _fft_filter_spectrum(a, tw, fbig, r):
    c = a.shape[-1]
    return pl.pallas_call(
        functools.partial(_fft_mk_kernel, r=r, scale=1.0 / (r * r)),
        out_shape=_sds((2, r * r, c), F32),
        grid=(r,),
        in_specs=[pl.BlockSpec((None, 2, r, c), lambda k: (0, 0, k, 0)),
                  pl.BlockSpec((None, r, 2), lambda k: (k, 0, 0)),
                  pl.BlockSpec((2 * r, 2 * r), lambda k: (0, 0))],
        out_specs=pl.BlockSpec((2, r, c), lambda k: (0, k, 0)),
        compiler_params=_params(("arbitrary",)),
        name="fft_filter_spectrum",
    )(a, tw, fbig)


def _fft_m_kernel(a_ref, ks_ref, tw_ref, fb_ref, fc_ref, o_ref, *, r):
    def dot(a, b):
        return jnp.dot(a, b, precision=HIGHEST, preferred_element_type=F32)

    are, aim = a_ref[0], a_ref[1]
    tr, ti = tw_ref[:, 0:1], tw_ref[:, 1:2]
    x = dot(fb_ref[...], jnp.concatenate([are * tr - aim * ti, are * ti + aim * tr], axis=0))
    xre, xim = x[:r], x[r:]
    kre, kim = ks_ref[0], ks_ref[1]
    c = dot(fc_ref[...], jnp.concatenate([xre * kre - xim * kim, xre * kim + xim * kre], axis=0))
    cre, cim = c[:r], c[r:]
    o_ref[0] = cre * tr + cim * ti
    o_ref[1] = cim * tr - cre * ti


def _fft_stage_m(a, ks, tw, fbig, fconj, r):
    b, _, _, c = a.shape
    return pl.pallas_call(
        functools.partial(_fft_m_kernel, r=r),
        out_shape=_sds(a.shape, F32),
        grid=(r, b),
        in_specs=[pl.BlockSpec((None, 2, r, c), lambda k, bi: (bi, 0, k, 0)),
                  pl.BlockSpec((2, r, c), lambda k, bi: (0, k, 0)),
                  pl.BlockSpec((None, r, 2), lambda k, bi: (k, 0, 0)),
                  pl.BlockSpec((2 * r, 2 * r), lambda k, bi: (0, 0)),
                  pl.BlockSpec((2 * r, 2 * r), lambda k, bi: (0, 0))],
        out_specs=pl.BlockSpec((None, 2, r, c), lambda k, bi: (bi, 0, k, 0)),
        compiler_params=_params(("arbitrary", "arbitrary")),
        name="fft_stage_m",
    )(a, ks, tw, fbig, fconj)


def _fft_f_kernel(d_ref, g_ref, o_ref, *, r):
    g = g_ref[...]

    def body(n2, carry):
        dcat = jnp.concatenate([d_ref[0, pl.ds(n2, r, stride=r), :], d_ref[1, pl.ds(n2, r, stride=r), :]], axis=0)
        o_ref[pl.ds(n2, r // 2, stride=r), :] = jnp.dot(g, dcat, precision=HIGHEST, preferred_element_type=F32)
        return carry

    lax.fori_loop(0, r, body, 0)


def _fft_stage_f(dmat, gfin, r):
    b, _, n, c = dmat.shape
    return pl.pallas_call(
        functools.partial(_fft_f_kernel, r=r),
        out_shape=_sds((b, n // 2, c), F32),
        grid=(b, c // LANES),
        in_specs=[pl.BlockSpec((None, 2, n, LANES), lambda bi, ci: (bi, 0, 0, ci)),
                  pl.BlockSpec((r // 2, 2 * r), lambda bi, ci: (0, 0))],
        out_specs=pl.BlockSpec((None, n // 2, LANES), lambda bi, ci: (bi, 0, ci)),
        compiler_params=_params(("arbitrary", "arbitrary")),
        name="fft_stage_f",
    )(dmat, gfin)


def _long_conv(zin, h_fwd, h_bwd):
    _, length, c = zin.shape
    r = int(round(math.sqrt(2 * length)))
    assert r * r == 2 * length, "sequence length must make 2L a perfect square"
    fa_half, fa_full, fbig, fconj, gfin, tw = _dft_tables(r)
    k2 = jnp.concatenate([h_fwd, jnp.zeros((1, c), F32), h_bwd[1:][::-1]], axis=0)
    ks = _fft_filter_spectrum(_fft_stage_a(k2[None], fa_full, r), tw, fbig, r)
    a = _fft_stage_a(zin, fa_half, r)
    dmat = _fft_stage_m(a, ks, tw, fbig, fconj, r)
    return _fft_stage_f(dmat, gfin, r)


def _alibi_slopes(n_heads):
    def pow2_slopes(m):
        start = 2.0 ** (-8.0 / m)
        return [start ** (i + 1) for i in range(m)]
    base = 2 ** int(math.floor(math.log2(n_heads)))
    slopes = pow2_slopes(base)
    if base < n_heads:
        slopes = slopes + pow2_slopes(2 * base)[0::2][: n_heads - base]
    return np.array(sorted(slopes, reverse=True), dtype=np.float32)


def _attn_kernel(q_ref, k_ref, kp_ref, kn_ref, v_ref, vp_ref, vn_ref, o_ref, l_ref, *, tq, n, dil, slopes):
    i = pl.program_id(1)
    side = ATTN_SIDE
    nk = tq + 2 * side
    row = lax.broadcasted_iota(jnp.int32, (tq, nk), 0)
    col = lax.broadcasted_iota(jnp.int32, (tq, nk), 1)
    rel = jnp.abs(col - side - row)
    kglob = i * tq + col - side
    valid = (rel <= side) & (kglob >= 0) & (kglob < n)
    dist = (rel * dil).astype(F32)
    scale = HEAD_DIM ** -0.5
    for h in range(HEADS_PER_GROUP):
        hs = slice(h * HEAD_DIM, (h + 1) * HEAD_DIM)
        q = q_ref[:, hs]
        kc = jnp.concatenate([kp_ref[:, hs], k_ref[:, hs], kn_ref[:, hs]], axis=0)
        vc = jnp.concatenate([vp_ref[:, hs], v_ref[:, hs], vn_ref[:, hs]], axis=0)
        s = lax.dot_general(q, kc, (((1,), (1,)), ((), ())), preferred_element_type=F32) * scale
        s = jnp.where(valid, s - float(slopes[h]) * dist, NEG_INF)
        m = jnp.max(s, axis=-1, keepdims=True)
        p = jnp.exp(s - m)
        den = jnp.sum(p, axis=-1, keepdims=True)
        o = jnp.dot(p.astype(BF16), vc, preferred_element_type=F32) / den
        o_ref[:, hs] = o.astype(o_ref.dtype)
        l_ref[:, hs] = jnp.broadcast_to(m + jnp.log(den), (tq, HEAD_DIM))


def _dilated_attention(qkv, n, dil, slopes, col0=0):
    streams = qkv.shape[0]
    tq = min(256, n)
    side = ATTN_SIDE
    nh = n // side
    gw = GROUP_WIDTH
    main = lambda cb: pl.BlockSpec((None, tq, gw), lambda s, i: (s, i, col0 + cb))
    prev = lambda cb: pl.BlockSpec((None, side, gw),
                                   lambda s, i: (s, jnp.maximum(i * (tq // side) - 1, 0), col0 + cb))
    nxt = lambda cb: pl.BlockSpec((None, side, gw),
                                  lambda s, i: (s, jnp.minimum((i + 1) * (tq // side), nh - 1), col0 + cb))
    return pl.pallas_call(
        functools.partial(_attn_kernel, tq=tq, n=n, dil=dil, slopes=tuple(float(v) for v in slopes)),
        out_shape=(_sds((streams, n, gw), BF16), _sds((streams, n, gw), F32)),
        grid=(streams, n // tq),
        in_specs=[main(0), main(1), prev(1), nxt(1), main(2), prev(2), nxt(2)],
        out_specs=(pl.BlockSpec((None, tq, gw), lambda s, i: (s, i, 0)),
                   pl.BlockSpec((None, tq, gw), lambda s, i: (s, i, 0))),
        compiler_params=_params(("arbitrary", "arbitrary")),
        name=f"dilated_attention_d{dil}",
    )(qkv, qkv, qkv, qkv, qkv, qkv, qkv)


def _merge_kernel(y_ref, z_ref, x1_ref, skip_ref, o0_ref, l0_ref, o1_ref, l1_ref, o2_ref, l2_ref,
                  ghy_ref, gat_ref, wh_ref, wa_ref, out_ref, hy_s, at_s, so1, sl1, so2, sl2, *, tm):
    j = pl.program_id(1)
    nct = GROUP_WIDTH // LANES

    @pl.when(j == 0)
    def _():
        hy_s[...] = ((y_ref[...] + z_ref[...] * skip_ref[...]) * x1_ref[...]).astype(BF16)
        for dil, oref, lref, so, sl in ((4, o1_ref, l1_ref, so1, sl1), (16, o2_ref, l2_ref, so2, sl2)):
            for r in range(dil):
                for c in range(nct):
                    cs = slice(c * LANES, (c + 1) * LANES)
                    so[c, pl.ds(r, tm // dil, stride=dil), :] = oref[r, :, cs].astype(F32)
                    sl[c, pl.ds(r, tm // dil, stride=dil), :] = lref[r, :, cs]
        for c in range(nct):
            cs = slice(c * LANES, (c + 1) * LANES)
            a0, a1, a2 = l0_ref[:, cs], sl1[c], sl2[c]
            m = jnp.maximum(jnp.maximum(a0, a1), a2)
            e0, e1, e2 = jnp.exp(a0 - m), jnp.exp(a1 - m), jnp.exp(a2 - m)
            at = (e0 * o0_ref[:, cs].astype(F32) + e1 * so1[c] + e2 * so2[c]) / (e0 + e1 + e2)
            at_s[:, cs] = at.astype(BF16)

    acc_h = jnp.dot(hy_s[...], wh_ref[...], preferred_element_type=F32)
    acc_a = jnp.dot(at_s[...], wa_ref[...], preferred_element_type=F32)
    out = (jax.nn.sigmoid(ghy_ref[...].astype(F32)) * acc_h + jax.nn.sigmoid(gat_ref[...].astype(F32)) * acc_a)
    out_ref[...] = out.astype(out_ref.dtype)


def _merge(yconv, zin, x1c, skip, o0, l0, o1, l1, o2, l2, pn, ghy_block0, gat_block0, wh, wa, batch, seq):
    n_tok, hw = yconv.shape
    d_model = wh.shape[1]
    gw = GROUP_WIDTH
    tm = 256
    tn = COL_BLOCK
    spb = seq // tm
    row = lambda width: pl.BlockSpec((tm, width), lambda i, j: (i, 0))
    res = lambda dil: pl.BlockSpec((None, dil, tm // dil, gw), lambda i, j: (i // spb, 0, i % spb, 0))
    return pl.pallas_call(
        functools.partial(_merge_kernel, tm=tm),
        out_shape=_sds((n_tok, d_model), BF16),
        grid=(n_tok // tm, d_model // tn),
        in_specs=[row(hw), row(hw), row(hw), pl.BlockSpec((1, hw), lambda i, j: (0, 0)),
                  row(gw), row(gw), res(4), res(4), res(16), res(16),
                  pl.BlockSpec((tm, tn), lambda i, j: (i, ghy_block0 + j)),
                  pl.BlockSpec((tm, tn), lambda i, j: (i, gat_block0 + j)),
                  pl.BlockSpec((hw, tn), lambda i, j: (0, j)),
                  pl.BlockSpec((gw, tn), lambda i, j: (0, j))],
        out_specs=pl.BlockSpec((tm, tn), lambda i, j: (i, j)),
        scratch_shapes=[pltpu.VMEM((tm, hw), BF16), pltpu.VMEM((tm, gw), BF16)]
                       + [pltpu.VMEM((gw // LANES, tm, LANES), F32)] * 4,
        compiler_params=_params(("arbitrary", "arbitrary")),
        name="gated_merge",
    )(yconv, zin, x1c, skip, o0, l0, o1, l1, o2, l2, pn, pn, wh, wa)


def _router_kernel(mo_ref, x_ref, gm_ref, gpost_ref, gpre_ref, sc_ref, sh_ref, wr_ref, br_ref,
                   x1_ref, h2_ref, idx_ref, tw_ref, *, ts, d_model, n_experts):
    mo = mo_ref[...]
    y = mo * lax.rsqrt(jnp.mean(mo * mo, axis=-1, keepdims=True) + RMS_EPS) * gpost_ref[...]
    x1 = x_ref[...] + gm_ref[...] * y
    x1_ref[...] = x1
    h2 = (x1 * lax.rsqrt(jnp.mean(x1 * x1, axis=-1, keepdims=True) + RMS_EPS) * gpre_ref[...]
          * (1.0 + sc_ref[...]) + sh_ref[...])
    nct = d_model // LANES
    for c in range(nct):
        h2_ref[pl.ds(c, ts, stride=nct), :] = h2[:, c * LANES:(c + 1) * LANES]
    logits = jnp.dot(h2, wr_ref[...], precision=HIGHEST, preferred_element_type=F32) + br_ref[...]
    lane = lax.broadcasted_iota(jnp.int32, logits.shape, 1)
    lane_f = lane.astype(F32)
    logits = jnp.where(lane < n_experts, logits, -jnp.inf)
    idx_out = jnp.zeros(logits.shape, jnp.int32)
    val_out = jnp.zeros(logits.shape, F32)
    top0 = None
    den = None
    for k in range(TOP_K):
        m = jnp.max(logits, axis=-1, keepdims=True)
        idx = jnp.min(jnp.where(logits == m, lane_f, float(LANES)), axis=-1, keepdims=True).astype(jnp.int32)
        if k == 0:
            top0 = m
        e = jnp.exp(m - top0)
        den = e if den is None else den + e
        idx_out = jnp.where(lane == k, idx, idx_out)
        val_out = jnp.where(lane == k, e, val_out)
        logits = jnp.where(lane == idx, -jnp.inf, logits)
    idx_ref[...] = idx_out
    tw_ref[...] = val_out / den


def _post_mix_and_route(mo, x, gate_m, g_post, g_pre, scale_f, shift_f, w_router, b_router):
    b, s, d = x.shape
    e = w_router.shape[1]
    ts = 256
    wr = jnp.zeros((d, LANES), F32).at[:, :e].set(w_router)
    br = jnp.zeros((1, LANES), F32).at[0, :e].set(b_router)
    nct = d // LANES
    spb = s // ts
    rowblk = lambda width: pl.BlockSpec((ts, width), lambda i: (i, 0))
    per_batch = pl.BlockSpec((None, 1, d), lambda i: (i // spb, 0, 0))
    vec = pl.BlockSpec((1, d), lambda i: (0, 0))
    n_tok = b * s
    return pl.pallas_call(
        functools.partial(_router_kernel, ts=ts, d_model=d, n_experts=e),
        out_shape=(_sds((n_tok, d), F32), _sds((n_tok * nct, LANES), F32),
                   _sds((n_tok, LANES), jnp.int32), _sds((n_tok, LANES), F32)),
        grid=(n_tok // ts,),
        in_specs=[rowblk(d), rowblk(d), per_batch, vec, vec, per_batch, per_batch,
                  pl.BlockSpec((d, LANES), lambda i: (0, 0)), pl.BlockSpec((1, LANES), lambda i: (0, 0))],
        out_specs=(rowblk(d), pl.BlockSpec((ts * nct, LANES), lambda i: (i, 0)), rowblk(LANES), rowblk(LANES)),
        compiler_params=_params(("arbitrary",)),
        name="post_mix_route",
    )(mo, x.reshape(n_tok, d), gate_m, g_post, g_pre, scale_f, shift_f, wr, br)


def _expert_up_kernel(be_ref, nu_ref, tok0_ref, tok1_ref, h2_hbm, wg_ref, bg_ref, wu_ref, bu_ref, act_ref,
                      xbuf, xb, sem, *, tb, nct):
    i = pl.program_id(0)
    n_used = nu_ref[0]
    slab = nct

    def gather(tok_ref, slot):
        def body(j, carry):
            t = tok_ref[0, j]
            pltpu.make_async_copy(h2_hbm.at[pl.ds(pl.multiple_of(t * slab, slab), slab), :],
                                  xbuf.at[slot, pl.ds(pl.multiple_of(j * slab, slab), slab), :],
                                  sem.at[slot]).start()
            return carry
        lax.fori_loop(0, tb, body, 0)

    def wait(slot):
        pltpu.make_async_copy(h2_hbm.at[pl.ds(0, tb * slab), :], xbuf.at[slot], sem.at[slot]).wait()

    @pl.when((i == 0) & (n_used > 0))
    def _():
        gather(tok0_ref, 0)

    @pl.when(i + 1 < n_used)
    def _():
        gather(tok1_ref, (i + 1) % 2)

    @pl.when(i < n_used)
    def _():
        slot = i % 2
        wait(slot)
        for c in range(nct):
            xb[:, c * LANES:(c + 1) * LANES] = xbuf[slot, pl.ds(c, tb, stride=nct), :].astype(BF16)
        x = xb[...]
        g = jnp.dot(x, wg_ref[...], preferred_element_type=F32) + bg_ref[...]
        u = jnp.dot(x, wu_ref[...], preferred_element_type=F32) + bu_ref[...]
        g = jnp.minimum(g, SWIGLU_LIMIT)
        u = jnp.clip(u, -SWIGLU_LIMIT, SWIGLU_LIMIT)
        act_ref[...] = (g * jax.nn.sigmoid(SWIGLU_ALPHA * g) * (u + 1.0)).astype(act_ref.dtype)

    @pl.when(i >= n_used)
    def _():
        act_ref[...] = jnp.zeros(act_ref.shape, act_ref.dtype)


def _expert_up(block_e, n_used, row_tok3, h2s, wg, bg, wu, bu, nct):
    n_blocks, _, tb = row_tok3.shape
    _, d, f = wg.shape
    grid_spec = pltpu.PrefetchScalarGridSpec(
        num_scalar_prefetch=2,
        grid=(n_blocks,),
        in_specs=[pl.BlockSpec((None, 1, tb), lambda i, be, nu: (i, 0, 0), memory_space=pltpu.SMEM),
                  pl.BlockSpec((None, 1, tb), lambda i, be, nu: (jnp.minimum(i + 1, n_blocks - 1), 0, 0),
                               memory_space=pltpu.SMEM),
                  pl.BlockSpec(memory_space=pl.ANY),
                  pl.BlockSpec((None, d, f), lambda i, be, nu: (be[i], 0, 0)),
                  pl.BlockSpec((None, 1, f), lambda i, be, nu: (be[i], 0, 0)),
                  pl.BlockSpec((None, d, f), lambda i, be, nu: (be[i], 0, 0)),
                  pl.BlockSpec((None, 1, f), lambda i, be, nu: (be[i], 0, 0))],
        out_specs=pl.BlockSpec((tb, f), lambda i, be, nu: (i, 0)),
        scratch_shapes=[pltpu.VMEM((2, tb * nct, LANES), F32), pltpu.VMEM((tb, d), BF16),
                        pltpu.SemaphoreType.DMA((2,))],
    )
    return pl.pallas_call(
        functools.partial(_expert_up_kernel, tb=tb, nct=nct),
        out_shape=_sds((n_blocks * tb, f), BF16),
        grid_spec=grid_spec,
        compiler_params=_params(("arbitrary",)),
        name="expert_up",
    )(block_e, n_used, row_tok3, row_tok3, h2s, wg, bg, wu, bu)


def _expert_down_kernel(be_ref, nu_ref, act_ref, wd_ref, bd_ref, rw_ref, ys_ref, *, tb, nct):
    i = pl.program_id(0)

    @pl.when(i < nu_ref[0])
    def _():
        y = (jnp.dot(act_ref[...], wd_ref[...], preferred_element_type=F32) + bd_ref[...]) * rw_ref[...]
        for c in range(nct):
            ys_ref[pl.ds(c, tb, stride=nct), :] = y[:, c * LANES:(c + 1) * LANES]

    @pl.when(i >= nu_ref[0])
    def _():
        ys_ref[...] = jnp.zeros(ys_ref.shape, ys_ref.dtype)


def _expert_down(block_e, n_used, act, wd, bd, row_w, tb):
    _, f, d = wd.shape
    n_blocks = act.shape[0] // tb
    nct = d // LANES
    grid_spec = pltpu.PrefetchScalarGridSpec(
        num_scalar_prefetch=2,
        grid=(n_blocks,),
        in_specs=[pl.BlockSpec((tb, f), lambda i, be, nu: (i, 0)),
                  pl.BlockSpec((None, f, d), lambda i, be, nu: (be[i], 0, 0)),
                  pl.BlockSpec((None, 1, d), lambda i, be, nu: (be[i], 0, 0)),
                  pl.BlockSpec((tb, 1), lambda i, be, nu: (i, 0))],
        out_specs=pl.BlockSpec((tb * nct, LANES), lambda i, be, nu: (i, 0)),
    )
    return pl.pallas_call(
        functools.partial(_expert_down_kernel, tb=tb, nct=nct),
        out_shape=_sds((n_blocks * tb * nct, LANES), F32),
        grid_spec=grid_spec,
        compiler_params=_params(("arbitrary",)),
        name="expert_down",
    )(block_e, n_used, act, wd, bd, row_w)


def _combine_kernel(d0_ref, d1_ref, ys_hbm, x1_ref, gf_ref, gpost_ref, o_ref, buf, ff, sem, *, tc, nct):
    i = pl.program_id(0)
    last = pl.num_programs(0) - 1
    n_rows = TOP_K * tc

    def gather(dref, slot):
        def body(j, carry):
            r = dref[0, j]
            pltpu.make_async_copy(ys_hbm.at[pl.ds(pl.multiple_of(r * nct, nct), nct), :],
                                  buf.at[slot, pl.ds(pl.multiple_of(j * nct, nct), nct), :],
                                  sem.at[slot]).start()
            return carry
        lax.fori_loop(0, n_rows, body, 0)

    @pl.when(i == 0)
    def _():
        gather(d0_ref, 0)

    @pl.when(i < last)
    def _():
        gather(d1_ref, (i + 1) % 2)

    slot = i % 2
    pltpu.make_async_copy(ys_hbm.at[pl.ds(0, n_rows * nct), :], buf.at[slot], sem.at[slot]).wait()
    for c in range(nct):
        acc = buf[slot, pl.ds(c, tc, stride=nct), :]
        for k in range(1, TOP_K):
            acc = acc + buf[slot, pl.ds(k * tc * nct + c, tc, stride=nct), :]
        ff[:, c * LANES:(c + 1) * LANES] = acc
    f = ff[...]
    y = f * lax.rsqrt(jnp.mean(f * f, axis=-1, keepdims=True) + RMS_EPS) * gpost_ref[...]
    o_ref[...] = x1_ref[...] + gf_ref[...] * y


def _combine(dest3, ys, x1, gate_f, g_post, seq):
    n_tok, d = x1.shape
    n_steps, _, n_rows = dest3.shape
    tc = n_rows // TOP_K
    nct = d // LANES
    spb = seq // tc
    return pl.pallas_call(
        functools.partial(_combine_kernel, tc=tc, nct=nct),
        out_shape=_sds((n_tok, d), F32),
        grid=(n_steps,),
        in_specs=[pl.BlockSpec((None, 1, n_rows), lambda i: (i, 0, 0), memory_space=pltpu.SMEM),
                  pl.BlockSpec((None, 1, n_rows), lambda i: (jnp.minimum(i + 1, n_steps - 1), 0, 0),
                               memory_space=pltpu.SMEM),
                  pl.BlockSpec(memory_space=pl.ANY),
                  pl.BlockSpec((tc, d), lambda i: (i, 0)),
                  pl.BlockSpec((None, 1, d), lambda i: (i // spb, 0, 0)),
                  pl.BlockSpec((1, d), lambda i: (0, 0))],
        out_specs=pl.BlockSpec((tc, d), lambda i: (i, 0)),
        scratch_shapes=[pltpu.VMEM((2, n_rows * nct, LANES), F32), pltpu.VMEM((tc, d), F32),
                        pltpu.SemaphoreType.DMA((2,))],
        compiler_params=_params(("arbitrary",)),
        name="expert_combine",
    )(dest3, dest3, ys, x1, gate_f, g_post)


def _routing_tables(top_idx, n_experts, tb):
    n_tok = top_idx.shape[0]
    n_assign = n_tok * TOP_K
    flat_e = top_idx.reshape(-1)
    onehot = (flat_e[:, None] == jnp.arange(n_experts, dtype=jnp.int32)[None, :]).astype(jnp.int32)
    csum = jnp.cumsum(onehot, axis=0)
    rank = jnp.sum(csum * onehot, axis=1) - 1
    sizes = csum[-1]
    padded = (sizes + tb - 1) // tb * tb
    pad_end = jnp.cumsum(padded)
    pad_start = pad_end - padded
    dest = (pad_start[flat_e] + rank).astype(jnp.int32)
    n_rows = -(-n_assign // tb) * tb + n_experts * tb
    n_blocks = n_rows // tb
    tok = (jnp.arange(n_assign, dtype=jnp.int32) // TOP_K)
    row_tok = jnp.zeros((n_rows,), jnp.int32).at[dest].set(tok)
    block_e = jnp.minimum(
        jnp.searchsorted(pad_end, jnp.arange(n_blocks, dtype=jnp.int32) * tb, side="right"),
        n_experts - 1).astype(jnp.int32)
    n_used = (pad_end[-1] // tb).astype(jnp.int32).reshape(1)
    return dest, row_tok, block_e, n_used, n_rows


def _moe(h2s, top_idx, top_w, wg, bg, wu, bu, wd, bd, x1, gate_f, g_post, seq):
    n_tok, d = x1.shape
    n_experts = wg.shape[0]
    tb = MOE_ROWS
    nct = d // LANES
    dest, row_tok, block_e, n_used, n_rows = _routing_tables(top_idx, n_experts, tb)
    row_w = jnp.zeros((n_rows,), F32).at[dest].set(top_w.reshape(-1))
    n_blocks = n_rows // tb
    act = _expert_up(block_e, n_used, row_tok.reshape(n_blocks, 1, tb), h2s, wg, bg[:, None, :], wu,
                     bu[:, None, :], nct)
    ys = _expert_down(block_e, n_used, act, wd, bd[:, None, :], row_w[:, None], tb)
    tc = 128
    dest3 = dest.reshape(n_tok // tc, tc, TOP_K).transpose(0, 2, 1).reshape(n_tok // tc, 1, TOP_K * tc)
    return _combine(dest3, ys, x1, gate_f, g_post, seq)


def _layer(x, c8, p):
    b, s, d = x.shape
    n_tok = b * s
    hw = p["hy_skip"].shape[0]
    gw = GROUP_WIDTH
    assert hw == COL_BLOCK and gw == COL_BLOCK and d % COL_BLOCK == 0

    mod = _adaln(c8, p["w_ada"], p["b_ada"][None])[:b]
    shift_m, scale_m, gate_m, shift_f, scale_f, gate_f = [m[:, None, :] for m in jnp.split(mod, 6, axis=-1)]

    h, h4, h16 = _prenorm_mix(x, p["g_pre_mix"][None], scale_m, shift_m)

    w_in = p["w_in"].astype(BF16)
    nd = d // COL_BLOCK
    nat_blocks = [0, 1, 2, 3, 6, 9] + list(range(12, 12 + 2 * nd))
    pn = _matmul(h.reshape(n_tok, d), w_in, nat_blocks, BF16, "in_proj_natural")
    qkv1 = _matmul(h4.reshape(n_tok, d), w_in, [4, 7, 10], BF16, "in_proj_dil4")
    qkv2 = _matmul(h16.reshape(n_tok, d), w_in, [5, 8, 11], BF16, "in_proj_dil16")

    h_fwd, h_bwd = _hyena_filters(s, p["hy_f_w1"], p["hy_f_b1"], p["hy_f_w2"], p["hy_f_b2"], p["hy_f_w3"],
                                  p["hy_f_b3"], p["hy_f_freq"], p["hy_f_wout"])
    zin, x1c = _shortconv(pn.reshape(b, s, -1), p["hy_conv_w"], p["hy_conv_b"], hw)
    yconv = _long_conv(zin, h_fwd, h_bwd)

    slopes = _alibi_slopes(N_GROUPS * HEADS_PER_GROUP).reshape(N_GROUPS, HEADS_PER_GROUP)
    o0, l0 = _dilated_attention(pn.reshape(b, s, -1), s, 1, slopes[0], col0=3)
    o1, l1 = _dilated_attention(qkv1.reshape(b * 4, s // 4, 3 * gw), s // 4, 4, slopes[1])
    o2, l2 = _dilated_attention(qkv2.reshape(b * 16, s // 16, 3 * gw), s // 16, 16, slopes[2])

    merged = _merge(yconv.reshape(n_tok, hw), zin.reshape(n_tok, hw), x1c.reshape(n_tok, hw), p["hy_skip"][None],
                    o0.reshape(n_tok, gw), l0.reshape(n_tok, gw),
                    o1.reshape(b, 4, s // 4, gw), l1.reshape(b, 4, s // 4, gw),
                    o2.reshape(b, 16, s // 16, gw), l2.reshape(b, 16, s // 16, gw),
                    pn, 6, 6 + nd, p["w_proj_hyena"].astype(BF16), p["w_proj_attn"].astype(BF16), b, s)
    mo = _matmul(merged, p["w_out"].astype(BF16), list(range(nd)), F32, "out_proj")

    x1, h2s, top_idx, top_w = _post_mix_and_route(mo, x, gate_m, p["g_post_mix"][None], p["g_pre_ffn"][None],
                                                  scale_f, shift_f, p["w_router"], p["b_router"])
    out = _moe(h2s, top_idx[:, :TOP_K], top_w[:, :TOP_K], p["w_gate"].astype(BF16), p["b_gate"],
               p["w_up"].astype(BF16), p["b_up"], p["w_down"].astype(BF16), p["b_down"], x1, gate_f,
               p["g_post_ffn"][None], s)
    return out.reshape(b, s, d)


def kernel(x, c, w_ada, b_ada, g_pre_mix, g_post_mix, g_pre_ffn, g_post_ffn, w_in, hy_conv_w, hy_conv_b, hy_skip, hy_f_w1, hy_f_b1, hy_f_w2, hy_f_b2, hy_f_w3, hy_f_b3, hy_f_freq, hy_f_wout, w_proj_hyena, w_proj_attn, w_out, w_router, b_router, w_gate, b_gate, w_up, b_up, w_down, b_down):
    names = ("w_ada", "b_ada", "g_pre_mix", "g_post_mix", "g_pre_ffn", "g_post_ffn", "w_in", "hy_conv_w",
             "hy_conv_b", "hy_skip", "hy_f_w1", "hy_f_b1", "hy_f_w2", "hy_f_b2", "hy_f_w3", "hy_f_b3",
             "hy_f_freq", "hy_f_wout", "w_proj_hyena", "w_proj_attn", "w_out", "w_router", "b_router",
             "w_gate", "b_gate", "w_up", "b_up", "w_down", "b_down")
    stacked = (w_ada, b_ada, g_pre_mix, g_post_mix, g_pre_ffn, g_post_ffn, w_in, hy_conv_w, hy_conv_b, hy_skip,
               hy_f_w1, hy_f_b1, hy_f_w2, hy_f_b2, hy_f_w3, hy_f_b3, hy_f_freq, hy_f_wout, w_proj_hyena,
               w_proj_attn, w_out, w_router, b_router, w_gate, b_gate, w_up, b_up, w_down, b_down)
    depth = w_ada.shape[0]
    b = x.shape[0]
    c8 = jnp.zeros((8, c.shape[1]), F32).at[:b].set(c)
    for l in range(depth):
        x = _layer(x, c8, {k: v[l] for k, v in zip(names, stacked)})
    return x
```

```python
import functools
import math

import jax
import jax.numpy as jnp
import numpy as np
from jax import lax
from jax.experimental import pallas as pl
from jax.experimental.pallas import tpu as pltpu

F32 = jnp.float32
BF16 = jnp.bfloat16
HIGHEST = lax.Precision.HIGHEST

LANES = 128
HEAD_DIM = 128
HEADS_PER_GROUP = 8
DILATED_GROUPS = ((128, 1), (512, 4), (2048, 16))
N_GROUPS = len(DILATED_GROUPS)
GROUP_WIDTH = HEADS_PER_GROUP * HEAD_DIM
ATTN_SIDE = 64
TOP_K = 4
SWIGLU_LIMIT = 7.0
SWIGLU_ALPHA = 1.702
RMS_EPS = 1e-6
NEG_INF = -1e30
HYENA_N_BANDS = 16
HYENA_DECAY_TARGET = 1e-2
HYENA_FAST_DECAY_PCT = 0.3
HYENA_SLOW_DECAY_PCT = 1.5
COL_BLOCK = 1024
MOE_ROWS = 256
SLAB_PAD = 8
VMEM_LIMIT = 56 * 1024 * 1024


def _params(sem, vmem=VMEM_LIMIT):
    return pltpu.CompilerParams(dimension_semantics=sem, vmem_limit_bytes=vmem)


def _sds(shape, dtype):
    return jax.ShapeDtypeStruct(shape, dtype)


def _ada_kernel(c_ref, w_ref, b_ref, o_ref):
    c = c_ref[...]
    sc = (c * jax.nn.sigmoid(c)).astype(BF16)
    o_ref[...] = jnp.dot(sc, w_ref[...].astype(BF16), preferred_element_type=F32) + b_ref[...]


def _adaln(c8, w_ada, b_ada):
    d, cols = w_ada.shape
    tn = 512
    return pl.pallas_call(
        _ada_kernel,
        out_shape=_sds((8, cols), F32),
        grid=(cols // tn,),
        in_specs=[pl.BlockSpec((8, d), lambda j: (0, 0)),
                  pl.BlockSpec((d, tn), lambda j: (0, j)),
                  pl.BlockSpec((1, tn), lambda j: (0, j))],
        out_specs=pl.BlockSpec((8, tn), lambda j: (0, j)),
        compiler_params=_params(("arbitrary",)),
        name="adaln",
    )(c8, w_ada, b_ada)


def _prenorm_kernel(x_ref, g_ref, sc_ref, sh_ref, o_ref, o4_ref, o16_ref, scr_ref, *, ts, d_model):
    x = x_ref[...]
    ms = jnp.mean(x * x, axis=-1, keepdims=True)
    h = x * lax.rsqrt(ms + RMS_EPS) * g_ref[...] * (1.0 + sc_ref[...]) + sh_ref[...]
    o_ref[...] = h.astype(BF16)
    nct = d_model // LANES
    for c in range(nct):
        scr_ref[c] = h[:, c * LANES:(c + 1) * LANES]
    for dil, oref in ((4, o4_ref), (16, o16_ref)):
        for r in range(dil):
            for c in range(nct):
                oref[r, :, c * LANES:(c + 1) * LANES] = scr_ref[c, pl.ds(r, ts // dil, stride=dil), :].astype(BF16)


def _prenorm_mix(x, g, scale, shift):
    b, s, d = x.shape
    ts = 256
    kern = functools.partial(_prenorm_kernel, ts=ts, d_model=d)
    return pl.pallas_call(
        kern,
        out_shape=(_sds((b, s, d), BF16), _sds((b, 4, s // 4, d), BF16), _sds((b, 16, s // 16, d), BF16)),
        grid=(b, s // ts),
        in_specs=[pl.BlockSpec((None, ts, d), lambda bi, i: (bi, i, 0)),
                  pl.BlockSpec((1, d), lambda bi, i: (0, 0)),
                  pl.BlockSpec((None, 1, d), lambda bi, i: (bi, 0, 0)),
                  pl.BlockSpec((None, 1, d), lambda bi, i: (bi, 0, 0))],
        out_specs=(pl.BlockSpec((None, ts, d), lambda bi, i: (bi, i, 0)),
                   pl.BlockSpec((None, 4, ts // 4, d), lambda bi, i: (bi, 0, i, 0)),
                   pl.BlockSpec((None, 16, ts // 16, d), lambda bi, i: (bi, 0, i, 0))),
        scratch_shapes=[pltpu.VMEM((d // LANES, ts, LANES), F32)],
        compiler_params=_params(("arbitrary", "arbitrary")),
        name="prenorm_mix",
    )(x, g, scale, shift)


def _mm_kernel(tbl_ref, a_ref, w_ref, o_ref):
    del tbl_ref
    o_ref[...] = jnp.dot(a_ref[...], w_ref[...], preferred_element_type=F32).astype(o_ref.dtype)


def _matmul(a, w, col_blocks, out_dtype, name, tm=1024, tn=COL_BLOCK):
    m, k = a.shape
    tm = min(tm, m)
    nb = len(col_blocks)
    tbl = jnp.asarray(col_blocks, jnp.int32)
    grid_spec = pltpu.PrefetchScalarGridSpec(
        num_scalar_prefetch=1,
        grid=(m // tm, nb),
        in_specs=[pl.BlockSpec((tm, k), lambda i, j, t: (i, 0)),
                  pl.BlockSpec((k, tn), lambda i, j, t: (0, t[j]))],
        out_specs=pl.BlockSpec((tm, tn), lambda i, j, t: (i, j)),
    )
    return pl.pallas_call(
        _mm_kernel,
        out_shape=_sds((m, nb * tn), out_dtype),
        grid_spec=grid_spec,
        compiler_params=_params(("arbitrary", "arbitrary")),
        name=name,
    )(tbl, a, w)


def _filter_kernel(z_ref, w1_ref, b1_ref, w2_ref, b2_ref, w3_ref, b3_ref, fr_ref, wo_ref, dl_ref,
                   h_ref, *, hw, r, pitch):
    def dot(a, b):
        return jnp.dot(a, b, precision=HIGHEST, preferred_element_type=F32)

    z = z_ref[...]
    fr = fr_ref[...]
    h = jnp.sin(fr * (dot(z, w1_ref[...]) + b1_ref[...]))
    h = jnp.sin(fr * (dot(h, w2_ref[...]) + b2_ref[...]))
    h = jnp.sin(fr * (dot(h, w3_ref[...]) + b3_ref[...]))
    filt = dot(h, wo_ref[...])
    decay = jnp.exp(-z[:, 0:1] * dl_ref[...])
    hf = filt[:, :hw] * decay
    hb = filt[:, hw:] * decay
    pad = jnp.zeros((pitch - r, hw), F32)
    for g in range(z.shape[0] // r):
        h_ref[0, g * pitch:g * pitch + r, :] = hf[g * r:(g + 1) * r]
        h_ref[1, g * pitch:g * pitch + r, :] = hb[g * r:(g + 1) * r]
        h_ref[0, g * pitch + r:(g + 1) * pitch, :] = pad
        h_ref[1, g * pitch + r:(g + 1) * pitch, :] = pad


def _hyena_filters(length, r, pitch, w1, b1, w2, b2, w3, b3, freq, wout):
    emb, fw = w1.shape
    hw = wout.shape[1] // 2
    t = np.linspace(0.0, 1.0, length)[:, None]
    bands = np.linspace(1e-4, HYENA_N_BANDS - 1, HYENA_N_BANDS)[None, :]
    ang = (2.0 * math.pi / length) * np.arange(length)[:, None] * bands
    z = np.concatenate([t, np.cos(ang), -np.sin(ang)], axis=-1)
    zpad = np.zeros((length, LANES), np.float32)
    zpad[:, :emb] = z
    w1p = jnp.zeros((LANES, fw), F32).at[:emb].set(w1)
    min_decay = math.log(HYENA_DECAY_TARGET) / HYENA_FAST_DECAY_PCT
    max_decay = math.log(HYENA_DECAY_TARGET) / HYENA_SLOW_DECAY_PCT
    deltas = np.abs(np.linspace(min_decay, max_decay, hw))[None, :].astype(np.float32)
    tl = min(1024, length)
    full = lambda shape: pl.BlockSpec(shape, lambda i: (0,) * len(shape))
    return pl.pallas_call(
        functools.partial(_filter_kernel, hw=hw, r=r, pitch=pitch),
        out_shape=_sds((2, length // r * pitch, hw), F32),
        grid=(length // tl,),
        in_specs=[pl.BlockSpec((tl, LANES), lambda i: (i, 0)),
                  full((LANES, fw)), full((1, fw)), full((fw, fw)), full((1, fw)),
                  full((fw, fw)), full((1, fw)), full((1, fw)), full((fw, 2 * hw)), full((1, hw))],
        out_specs=pl.BlockSpec((2, tl // r * pitch, hw), lambda i: (0, i, 0)),
        compiler_params=_params(("arbitrary",)),
        name="hyena_filters",
    )(jnp.asarray(zpad), w1p, b1[None], w2, b2[None], w3, b3[None], freq[None], wout, jnp.asarray(deltas))


def _shortconv_kernel(u_ref, up_ref, un_ref, w_ref, b_ref, z_ref, x1_ref, *, ts, hw, r, pitch):
    i = pl.program_id(1)
    last = pl.num_programs(1) - 1
    u = u_ref[...].astype(F32)
    prev_blk = up_ref[...].astype(F32)
    next_blk = un_ref[...].astype(F32)
    prev_row = jnp.where(i > 0, prev_blk[15:16, :], 0.0)
    next_row = jnp.where(i < last, next_blk[0:1, :], 0.0)
    row = lax.broadcasted_iota(jnp.int32, u.shape, 0)
    um = jnp.where(row == 0, prev_row, pltpu.roll(u, 1, 0))
    up = jnp.where(row == ts - 1, next_row, pltpu.roll(u, ts - 1, 0))
    w = w_ref[...]
    uc = w[0:1] * um + w[1:2] * u + w[2:3] * up + b_ref[...]
    x1_ref[...] = uc[:, :hw]
    z = uc[:, 2 * hw:] * uc[:, hw:2 * hw]
    pad = jnp.zeros((pitch - r, hw), F32)
    for g in range(ts // r):
        z_ref[g * pitch:g * pitch + r, :] = z[g * r:(g + 1) * r]
        z_ref[g * pitch + r:(g + 1) * pitch, :] = pad


def _shortconv(pn3, conv_w, conv_b, hw, r, pitch):
    b, s, _ = pn3.shape
    ts = 512
    w3 = 3 * hw
    nh = s // 16
    return pl.pallas_call(
        functools.partial(_shortconv_kernel, ts=ts, hw=hw, r=r, pitch=pitch),
        out_shape=(_sds((b, s // r * pitch, hw), F32), _sds((b, s, hw), F32)),
        grid=(b, s // ts),
        in_specs=[pl.BlockSpec((None, ts, w3), lambda bi, i: (bi, i, 0)),
                  pl.BlockSpec((None, 16, w3), lambda bi, i: (bi, jnp.maximum(i * (ts // 16) - 1, 0), 0)),
                  pl.BlockSpec((None, 16, w3), lambda bi, i: (bi, jnp.minimum((i + 1) * (ts // 16), nh - 1), 0)),
                  pl.BlockSpec((3, w3), lambda bi, i: (0, 0)),
                  pl.BlockSpec((1, w3), lambda bi, i: (0, 0))],
        out_specs=(pl.BlockSpec((None, ts // r * pitch, hw), lambda bi, i: (bi, i, 0)),
                   pl.BlockSpec((None, ts, hw), lambda bi, i: (bi, i, 0))),
        compiler_params=_params(("arbitrary", "arbitrary")),
        name="hyena_shortconv",
    )(pn3, pn3, pn3, conv_w, conv_b[None])


FFT_GROUP = 4


def _split_bf16(a):
    hi = a.astype(BF16)
    return hi, (a - hi.astype(F32)).astype(BF16)


def _dot3(fh, fl, x):
    xh, xl = _split_bf16(x)
    d = lambda a, b: jnp.dot(a, b, preferred_element_type=F32)
    return d(fh, xh) + (d(fh, xl) + d(fl, xh))


def _dft_tables(r):
    idx = np.arange(r)
    ang = 2.0 * np.pi * np.outer(idx, idx) / r
    cos, sin = np.cos(ang), np.sin(ang)
    fa_half = np.concatenate([cos[:, :r // 2], -sin[:, :r // 2]], axis=0)
    fbig = np.block([[cos, sin], [-sin, cos]])
    fconj = np.block([[cos, -sin], [sin, cos]])
    gfin = np.concatenate([cos[:r // 2], -sin[:r // 2]], axis=1)
    tang = 2.0 * np.pi * np.outer(idx, idx) / (r * r)
    tw = np.stack([np.cos(tang), -np.sin(tang)], axis=-1)
    split = lambda a: _split_bf16(jnp.asarray(a.astype(np.float32)))
    return split(fa_half), split(fbig), split(fconj), split(gfin), jnp.asarray(tw.astype(np.float32))


def _fft_a_kernel(x_ref, fh_ref, fl_ref, o_ref, *, r, k1, pitch):
    fh, fl = fh_ref[...], fl_ref[...]
    pad = jnp.zeros((pitch - r, LANES), F32)
    for g in range(r):
        o_ref[0, g * pitch + r:(g + 1) * pitch, :] = pad
        o_ref[1, g * pitch + r:(g + 1) * pitch, :] = pad

    def body(g, carry):
        n2 = g * FFT_GROUP
        xs = jnp.concatenate([x_ref[pl.ds(n2 + k, k1, stride=pitch), :] for k in range(FFT_GROUP)], axis=1)
        a = _dot3(fh, fl, xs)
        for k in range(FFT_GROUP):
            o_ref[0, pl.ds(n2 + k, r, stride=pitch), :] = a[:r, k * LANES:(k + 1) * LANES]
            o_ref[1, pl.ds(n2 + k, r, stride=pitch), :] = a[r:, k * LANES:(k + 1) * LANES]
        return carry

    lax.fori_loop(0, r // FFT_GROUP, body, 0)


def _fft_stage_a(x, fmat, r, pitch):
    bx, rows, c = x.shape
    k1 = rows // pitch
    return pl.pallas_call(
        functools.partial(_fft_a_kernel, r=r, k1=k1, pitch=pitch),
        out_shape=_sds((bx, 2, r * pitch, c), F32),
        grid=(bx, c // LANES),
        in_specs=[pl.BlockSpec((None, rows, LANES), lambda b, ci: (b, 0, ci)),
                  pl.BlockSpec((2 * r, k1), lambda b, ci: (0, 0)),
                  pl.BlockSpec((2 * r, k1), lambda b, ci: (0, 0))],
        out_specs=pl.BlockSpec((None, 2, r * pitch, LANES), lambda b, ci: (b, 0, 0, ci)),
        compiler_params=_params(("arbitrary", "arbitrary")),
        name="fft_stage_a",
    )(x, *fmat)


def _twiddled(a_ref, b, tr, ti, r):
    are, aim = a_ref[b, 0, :r, :], a_ref[b, 1, :r, :]
    return jnp.concatenate([are * tr - aim * ti, are * ti + aim * tr], axis=0)


def _fft_mk_kernel(a_ref, tw_ref, h0_ref, fbh_ref, fbl_ref, o_ref, *, r, scale):
    tr, ti = tw_ref[:, 0:1], tw_ref[:, 1:2]
    fbh, fbl = fbh_ref[...], fbl_ref[...]
    xf = _dot3(fbh, fbl, _twiddled(a_ref, 0, tr, ti, r))
    xb = _dot3(fbh, fbl, _twiddled(a_ref, 1, tr, ti, r))
    o_ref[0] = (xf[:r] + xb[:r] - h0_ref[...]) * scale
    o_ref[1] = (xf[r:] - xb[r:]) * scale


def _fft_filter_spectrum(a, tw, h0, fbig, r, pitch):
    c = a.shape[-1]
    return pl.pallas_call(
        functools.partial(_fft_mk_kernel, r=r, scale=1.0 / (r * r)),
        out_shape=_sds((2, r * r, c), F32),
        grid=(r,),
        in_specs=[pl.BlockSpec((2, 2, pitch, c), lambda k: (0, 0, k, 0)),
                  pl.BlockSpec((None, r, 2), lambda k: (k, 0, 0)),
                  pl.BlockSpec((1, c), lambda k: (0, 0)),
                  pl.BlockSpec((2 * r, 2 * r), lambda k: (0, 0)),
                  pl.BlockSpec((2 * r, 2 * r), lambda k: (0, 0))],
        out_specs=pl.BlockSpec((2, r, c), lambda k: (0, k, 0)),
        compiler_params=_params(("arbitrary",)),
        name="fft_filter_spectrum",
    )(a, tw, h0, *fbig)


def _fft_m_kernel(a_ref, ks_ref, tw_ref, fbh_ref, fbl_ref, fch_ref, fcl_ref, o_ref, *, r):
    tr, ti = tw_ref[:, 0:1], tw_ref[:, 1:2]
    x = _dot3(fbh_ref[...], fbl_ref[...], _twiddled(a_ref, 0, tr, ti, r))
    xre, xim = x[:r], x[r:]
    kre, kim = ks_ref[0], ks_ref[1]
    c = _dot3(fch_ref[...], fcl_ref[...],
              jnp.concatenate([xre * kre - xim * kim, xre * kim + xim * kre], axis=0))
    cre, cim = c[:r], c[r:]
    o_ref[0, :r, :] = cre * tr + cim * ti
    o_ref[1, :r, :] = cim * tr - cre * ti
    o_ref[:, r:, :] = jnp.zeros((2,) + (o_ref.shape[1] - r, o_ref.shape[2]), F32)


def _fft_stage_m(a, ks, tw, fbig, fconj, r, pitch):
    b, _, _, c = a.shape
    a5 = a.reshape(b, 1, 2, r * pitch, c)
    mat = pl.BlockSpec((2 * r, 2 * r), lambda k, bi: (0, 0))
    return pl.pallas_call(
        functools.partial(_fft_m_kernel, r=r),
        out_shape=_sds(a.shape, F32),
        grid=(r, b),
        in_specs=[pl.BlockSpec((None, 1, 2, pitch, c), lambda k, bi: (bi, 0, 0, k, 0)),
                  pl.BlockSpec((2, r, c), lambda k, bi: (0, k, 0)),
                  pl.BlockSpec((None, r, 2), lambda k, bi: (k, 0, 0)),
                  mat, mat, mat, mat],
        out_specs=pl.BlockSpec((None, 2, pitch, c), lambda k, bi: (bi, 0, k, 0)),
        compiler_params=_params(("arbitrary", "arbitrary")),
        name="fft_stage_m",
    )(a5, ks, tw, *fbig, *fconj)


def _fft_f_kernel(d_ref, gh_ref, gl_ref, o_ref, *, r, pitch):
    gh, gl = gh_ref[...], gl_ref[...]
    pad = jnp.zeros((pitch - r, LANES), F32)
    for g in range(r // 2):
        o_ref[g * pitch + r:(g + 1) * pitch, :] = pad

    def body(g, carry):
        n2 = g * FFT_GROUP
        dcat = jnp.concatenate(
            [jnp.concatenate([d_ref[0, pl.ds(n2 + k, r, stride=pitch), :],
                              d_ref[1, pl.ds(n2 + k, r, stride=pitch), :]], axis=0) for k in range(FFT_GROUP)],
            axis=1)
        y = _dot3(gh, gl, dcat)
        for k in range(FFT_GROUP):
            o_ref[pl.ds(n2 + k, r // 2, stride=pitch), :] = y[:, k * LANES:(k + 1) * LANES]
        return carry

    lax.fori_loop(0, r // FFT_GROUP, body, 0)


def _fft_stage_f(dmat, gfin, r, pitch):
    b, _, rows, c = dmat.shape
    return pl.pallas_call(
        functools.partial(_fft_f_kernel, r=r, pitch=pitch),
        out_shape=_sds((b, rows // 2, c), F32),
        grid=(b, c // LANES),
        in_specs=[pl.BlockSpec((None, 2, rows, LANES), lambda bi, ci: (bi, 0, 0, ci)),
                  pl.BlockSpec((r // 2, 2 * r), lambda bi, ci: (0, 0)),
                  pl.BlockSpec((r // 2, 2 * r), lambda bi, ci: (0, 0))],
        out_specs=pl.BlockSpec((None, rows // 2, LANES), lambda bi, ci: (bi, 0, ci)),
        compiler_params=_params(("arbitrary", "arbitrary")),
        name="fft_stage_f",
    )(dmat, *gfin)


def _fft_radix(length):
    r = int(round(math.sqrt(2 * length)))
    assert r * r == 2 * length, "sequence length must make 2L a perfect square"
    return r, r + 8


def _long_conv(zin, hfb, r, pitch):
    fa_half, fbig, fconj, gfin, tw = _dft_tables(r)
    ks = _fft_filter_spectrum(_fft_stage_a(hfb, fa_half, r, pitch), tw, hfb[1, 0:1, :], fbig, r, pitch)
    a = _fft_stage_a(zin, fa_half, r, pitch)
    dmat = _fft_stage_m(a, ks, tw, fbig, fconj, r, pitch)
    return _fft_stage_f(dmat, gfin, r, pitch)


def _alibi_slopes(n_heads):
    def pow2_slopes(m):
        start = 2.0 ** (-8.0 / m)
        return [start ** (i + 1) for i in range(m)]
    base = 2 ** int(math.floor(math.log2(n_heads)))
    slopes = pow2_slopes(base)
    if base < n_heads:
        slopes = slopes + pow2_slopes(2 * base)[0::2][: n_heads - base]
    return np.array(sorted(slopes, reverse=True), dtype=np.float32)


def _attn_kernel(q_ref, k_ref, kp_ref, kn_ref, v_ref, vp_ref, vn_ref, o_ref, l_ref, *, tq, n, dil, slopes):
    i = pl.program_id(1)
    side = ATTN_SIDE
    nk = tq + 2 * side
    row = lax.broadcasted_iota(jnp.int32, (tq, nk), 0)
    col = lax.broadcasted_iota(jnp.int32, (tq, nk), 1)
    rel = jnp.abs(col - side - row)
    kglob = i * tq + col - side
    valid = (rel <= side) & (kglob >= 0) & (kglob < n)
    dist = (rel * dil).astype(F32)
    scale = HEAD_DIM ** -0.5
    for h in range(HEADS_PER_GROUP):
        hs = slice(h * HEAD_DIM, (h + 1) * HEAD_DIM)
        q = q_ref[:, hs]
        kc = jnp.concatenate([kp_ref[:, hs], k_ref[:, hs], kn_ref[:, hs]], axis=0)
        vc = jnp.concatenate([vp_ref[:, hs], v_ref[:, hs], vn_ref[:, hs]], axis=0)
        s = lax.dot_general(q, kc, (((1,), (1,)), ((), ())), preferred_element_type=F32) * scale
        s = jnp.where(valid, s - float(slopes[h]) * dist, NEG_INF)
        m = jnp.max(s, axis=-1, keepdims=True)
        p = jnp.exp(s - m)
        den = jnp.sum(p, axis=-1, keepdims=True)
        o = jnp.dot(p.astype(BF16), vc, preferred_element_type=F32) / den
        o_ref[:, hs] = o.astype(o_ref.dtype)
        l_ref[:, hs] = jnp.broadcast_to(m + jnp.log(den), (tq, HEAD_DIM))


def _dilated_attention(qkv, n, dil, slopes, col0=0):
    streams = qkv.shape[0]
    tq = min(256, n)
    side = ATTN_SIDE
    nh = n // side
    gw = GROUP_WIDTH
    main = lambda cb: pl.BlockSpec((None, tq, gw), lambda s, i: (s, i, col0 + cb))
    prev = lambda cb: pl.BlockSpec((None, side, gw),
                                   lambda s, i: (s, jnp.maximum(i * (tq // side) - 1, 0), col0 + cb))
    nxt = lambda cb: pl.BlockSpec((None, side, gw),
                                  lambda s, i: (s, jnp.minimum((i + 1) * (tq // side), nh - 1), col0 + cb))
    return pl.pallas_call(
        functools.partial(_attn_kernel, tq=tq, n=n, dil=dil, slopes=tuple(float(v) for v in slopes)),
        out_shape=(_sds((streams, n, gw), BF16), _sds((streams, n, gw), F32)),
        grid=(streams, n // tq),
        in_specs=[main(0), main(1), prev(1), nxt(1), main(2), prev(2), nxt(2)],
        out_specs=(pl.BlockSpec((None, tq, gw), lambda s, i: (s, i, 0)),
                   pl.BlockSpec((None, tq, gw), lambda s, i: (s, i, 0))),
        compiler_params=_params(("arbitrary", "arbitrary")),
        name=f"dilated_attention_d{dil}",
    )(qkv, qkv, qkv, qkv, qkv, qkv, qkv)


def _merge_kernel(y_ref, z_ref, x1_ref, skip_ref, o0_ref, l0_ref, o1_ref, l1_ref, o2_ref, l2_ref,
                  ghy_ref, gat_ref, wh_ref, wa_ref, out_ref, hy_s, at_s, so1, sl1, so2, sl2, *, tm, r, pitch):
    j = pl.program_id(1)
    nct = GROUP_WIDTH // LANES

    @pl.when(j == 0)
    def _():
        for g in range(tm // r):
            rows = slice(g * pitch, g * pitch + r)
            hy_s[g * r:(g + 1) * r, :] = ((y_ref[rows, :] + z_ref[rows, :] * skip_ref[...])
                                          * x1_ref[g * r:(g + 1) * r, :]).astype(BF16)
        for dil, oref, lref, so, sl in ((4, o1_ref, l1_ref, so1, sl1), (16, o2_ref, l2_ref, so2, sl2)):
            for res in range(dil):
                for c in range(nct):
                    cs = slice(c * LANES, (c + 1) * LANES)
                    so[c, pl.ds(res, tm // dil, stride=dil), :] = oref[res, :, cs].astype(F32)
                    sl[c, pl.ds(res, tm // dil, stride=dil), :] = lref[res, :, cs]
        for c in range(nct):
            cs = slice(c * LANES, (c + 1) * LANES)
            a0, a1, a2 = l0_ref[:, cs], sl1[c], sl2[c]
            m = jnp.maximum(jnp.maximum(a0, a1), a2)
            e0, e1, e2 = jnp.exp(a0 - m), jnp.exp(a1 - m), jnp.exp(a2 - m)
            at = (e0 * o0_ref[:, cs].astype(F32) + e1 * so1[c] + e2 * so2[c]) / (e0 + e1 + e2)
            at_s[:, cs] = at.astype(BF16)

    acc_h = jnp.dot(hy_s[...], wh_ref[...], preferred_element_type=F32)
    acc_a = jnp.dot(at_s[...], wa_ref[...], preferred_element_type=F32)
    out = (jax.nn.sigmoid(ghy_ref[...].astype(F32)) * acc_h + jax.nn.sigmoid(gat_ref[...].astype(F32)) * acc_a)
    out_ref[...] = out.astype(out_ref.dtype)


def _merge(yconv, zin, x1c, skip, o0, l0, o1, l1, o2, l2, pn, ghy_block0, gat_block0, wh, wa, batch, seq, r, pitch):
    n_tok, hw = x1c.shape
    d_model = wh.shape[1]
    gw = GROUP_WIDTH
    tm = 256
    tn = COL_BLOCK
    spb = seq // tm
    row = lambda width: pl.BlockSpec((tm, width), lambda i, j: (i, 0))
    prow = pl.BlockSpec((tm // r * pitch, hw), lambda i, j: (i, 0))
    res = lambda dil: pl.BlockSpec((None, dil, tm // dil, gw), lambda i, j: (i // spb, 0, i % spb, 0))
    return pl.pallas_call(
        functools.partial(_merge_kernel, tm=tm, r=r, pitch=pitch),
        out_shape=_sds((n_tok, d_model), BF16),
        grid=(n_tok // tm, d_model // tn),
        in_specs=[prow, prow, row(hw), pl.BlockSpec((1, hw), lambda i, j: (0, 0)),
                  row(gw), row(gw), res(4), res(4), res(16), res(16),
                  pl.BlockSpec((tm, tn), lambda i, j: (i, ghy_block0 + j)),
                  pl.BlockSpec((tm, tn), lambda i, j: (i, gat_block0 + j)),
                  pl.BlockSpec((hw, tn), lambda i, j: (0, j)),
                  pl.BlockSpec((gw, tn), lambda i, j: (0, j))],
        out_specs=pl.BlockSpec((tm, tn), lambda i, j: (i, j)),
        scratch_shapes=[pltpu.VMEM((tm, hw), BF16), pltpu.VMEM((tm, gw), BF16)]
                       + [pltpu.VMEM((gw // LANES, tm, LANES), F32)] * 4,
        compiler_params=_params(("arbitrary", "arbitrary")),
        name="gated_merge",
    )(yconv, zin, x1c, skip, o0, l0, o1, l1, o2, l2, pn, pn, wh, wa)


def _router_kernel(mo_ref, x_ref, gm_ref, gpost_ref, gpre_ref, sc_ref, sh_ref, wr_ref, br_ref,
                   x1_ref, h2_ref, idx_ref, tw_ref, *, ts, d_model, n_experts):
    mo = mo_ref[...]
    y = mo * lax.rsqrt(jnp.mean(mo * mo, axis=-1, keepdims=True) + RMS_EPS) * gpost_ref[...]
    x1 = x_ref[...] + gm_ref[...] * y
    x1_ref[...] = x1
    h2 = (x1 * lax.rsqrt(jnp.mean(x1 * x1, axis=-1, keepdims=True) + RMS_EPS) * gpre_ref[...]
          * (1.0 + sc_ref[...]) + sh_ref[...])
    nct = d_model // LANES
    sp = nct + SLAB_PAD
    for c in range(nct):
        h2_ref[pl.ds(c, ts, stride=sp), :] = h2[:, c * LANES:(c + 1) * LANES]
    for c in range(nct, sp):
        h2_ref[pl.ds(c, ts, stride=sp), :] = jnp.zeros((ts, LANES), F32)
    logits = jnp.dot(h2, wr_ref[...], precision=HIGHEST, preferred_element_type=F32) + br_ref[...]
    lane = lax.broadcasted_iota(jnp.int32, logits.shape, 1)
    lane_f = lane.astype(F32)
    logits = jnp.where(lane < n_experts, logits, -jnp.inf)
    idx_out = jnp.zeros(logits.shape, jnp.int32)
    val_out = jnp.zeros(logits.shape, F32)
    top0 = None
    den = None
    for k in range(TOP_K):
        m = jnp.max(logits, axis=-1, keepdims=True)
        idx = jnp.min(jnp.where(logits == m, lane_f, float(LANES)), axis=-1, keepdims=True).astype(jnp.int32)
        if k == 0:
            top0 = m
        e = jnp.exp(m - top0)
        den = e if den is None else den + e
        idx_out = jnp.where(lane == k, idx, idx_out)
        val_out = jnp.where(lane == k, e, val_out)
        logits = jnp.where(lane == idx, -jnp.inf, logits)
    idx_ref[...] = idx_out
    tw_ref[...] = val_out / den


def _post_mix_and_route(mo, x, gate_m, g_post, g_pre, scale_f, shift_f, w_router, b_router):
    b, s, d = x.shape
    e = w_router.shape[1]
    ts = 256
    wr = jnp.zeros((d, LANES), F32).at[:, :e].set(w_router)
    br = jnp.zeros((1, LANES), F32).at[0, :e].set(b_router)
    nct = d // LANES
    sp = nct + SLAB_PAD
    spb = s // ts
    rowblk = lambda width: pl.BlockSpec((ts, width), lambda i: (i, 0))
    per_batch = pl.BlockSpec((None, 1, d), lambda i: (i // spb, 0, 0))
    vec = pl.BlockSpec((1, d), lambda i: (0, 0))
    n_tok = b * s
    return pl.pallas_call(
        functools.partial(_router_kernel, ts=ts, d_model=d, n_experts=e),
        out_shape=(_sds((n_tok, d), F32), _sds((n_tok * sp, LANES), F32),
                   _sds((n_tok, LANES), jnp.int32), _sds((n_tok, LANES), F32)),
        grid=(n_tok // ts,),
        in_specs=[rowblk(d), rowblk(d), per_batch, vec, vec, per_batch, per_batch,
                  pl.BlockSpec((d, LANES), lambda i: (0, 0)), pl.BlockSpec((1, LANES), lambda i: (0, 0))],
        out_specs=(rowblk(d), pl.BlockSpec((ts * sp, LANES), lambda i: (i, 0)), rowblk(LANES), rowblk(LANES)),
        compiler_params=_params(("arbitrary",)),
        name="post_mix_route",
    )(mo, x.reshape(n_tok, d), gate_m, g_post, g_pre, scale_f, shift_f, wr, br)


def _expert_up_kernel(be_ref, nu_ref, tok0_ref, tok1_ref, h2_hbm, wg_ref, bg_ref, wu_ref, bu_ref, act_ref,
                      xbuf, xb, sem, *, tb, nct):
    i = pl.program_id(0)
    n_used = nu_ref[0]
    sp = nct + SLAB_PAD

    def row_copy(t, j, slot):
        return pltpu.make_async_copy(h2_hbm.at[pl.ds(pl.multiple_of(t * sp, 8), nct), :],
                                     xbuf.at[slot, pl.ds(pl.multiple_of(j * sp, 8), nct), :], sem.at[slot])

    def gather(tok_ref, slot):
        def body(j, carry):
            row_copy(tok_ref[0, j], j, slot).start()
            return carry
        lax.fori_loop(0, tb, body, 0, unroll=8)

    def wait(slot):
        pltpu.make_async_copy(h2_hbm.at[pl.ds(0, tb * nct), :], xbuf.at[slot, pl.ds(0, tb * nct), :],
                              sem.at[slot]).wait()

    @pl.when((i == 0) & (n_used > 0))
    def _():
        gather(tok0_ref, 0)

    @pl.when(i + 1 < n_used)
    def _():
        gather(tok1_ref, (i + 1) % 2)

    @pl.when(i < n_used)
    def _():
        slot = i % 2
        wait(slot)
        for c in range(nct):
            xb[:, c * LANES:(c + 1) * LANES] = xbuf[slot, pl.ds(c, tb, stride=sp), :].astype(BF16)
        x = xb[...]
        g = jnp.dot(x, wg_ref[...], preferred_element_type=F32) + bg_ref[...]
        u = jnp.dot(x, wu_ref[...], preferred_element_type=F32) + bu_ref[...]
        g = jnp.minimum(g, SWIGLU_LIMIT)
        u = jnp.clip(u, -SWIGLU_LIMIT, SWIGLU_LIMIT)
        act_ref[...] = (g * jax.nn.sigmoid(SWIGLU_ALPHA * g) * (u + 1.0)).astype(act_ref.dtype)

    @pl.when(i >= n_used)
    def _():
        act_ref[...] = jnp.zeros(act_ref.shape, act_ref.dtype)


def _expert_up(block_e, n_used, row_tok3, h2s, wg, bg, wu, bu, nct):
    n_blocks, _, tb = row_tok3.shape
    _, d, f = wg.shape
    grid_spec = pltpu.PrefetchScalarGridSpec(
        num_scalar_prefetch=2,
        grid=(n_blocks,),
        in_specs=[pl.BlockSpec((None, 1, tb), lambda i, be, nu: (i, 0, 0), memory_space=pltpu.SMEM),
                  pl.BlockSpec((None, 1, tb), lambda i, be, nu: (jnp.minimum(i + 1, n_blocks - 1), 0, 0),
                               memory_space=pltpu.SMEM),
                  pl.BlockSpec(memory_space=pl.ANY),
                  pl.BlockSpec((None, d, f), lambda i, be, nu: (be[i], 0, 0)),
                  pl.BlockSpec((None, 1, f), lambda i, be, nu: (be[i], 0, 0)),
                  pl.BlockSpec((None, d, f), lambda i, be, nu: (be[i], 0, 0)),
                  pl.BlockSpec((None, 1, f), lambda i, be, nu: (be[i], 0, 0))],
        out_specs=pl.BlockSpec((tb, f), lambda i, be, nu: (i, 0)),
        scratch_shapes=[pltpu.VMEM((2, tb * (nct + SLAB_PAD), LANES), F32), pltpu.VMEM((tb, d), BF16),
                        pltpu.SemaphoreType.DMA((2,))],
    )
    return pl.pallas_call(
        functools.partial(_expert_up_kernel, tb=tb, nct=nct),
        out_shape=_sds((n_blocks * tb, f), BF16),
        grid_spec=grid_spec,
        compiler_params=_params(("arbitrary",)),
        name="expert_up",
    )(block_e, n_used, row_tok3, row_tok3, h2s, wg, bg, wu, bu)


def _expert_down_kernel(be_ref, nu_ref, act_ref, wd_ref, bd_ref, rw_ref, ys_ref, *, tb, nct):
    i = pl.program_id(0)

    @pl.when(i < nu_ref[0])
    def _():
        y = (jnp.dot(act_ref[...], wd_ref[...], preferred_element_type=F32) + bd_ref[...]) * rw_ref[...]
        sp = nct + SLAB_PAD
        for c in range(nct):
            ys_ref[pl.ds(c, tb, stride=sp), :] = y[:, c * LANES:(c + 1) * LANES]
        for c in range(nct, sp):
            ys_ref[pl.ds(c, tb, stride=sp), :] = jnp.zeros((tb, LANES), F32)

    @pl.when(i >= nu_ref[0])
    def _():
        ys_ref[...] = jnp.zeros(ys_ref.shape, ys_ref.dtype)


def _expert_down(block_e, n_used, act, wd, bd, row_w, tb):
    _, f, d = wd.shape
    n_blocks = act.shape[0] // tb
    nct = d // LANES
    grid_spec = pltpu.PrefetchScalarGridSpec(
        num_scalar_prefetch=2,
        grid=(n_blocks,),
        in_specs=[pl.BlockSpec((tb, f), lambda i, be, nu: (i, 0)),
                  pl.BlockSpec((None, f, d), lambda i, be, nu: (be[i], 0, 0)),
                  pl.BlockSpec((None, 1, d), lambda i, be, nu: (be[i], 0, 0)),
                  pl.BlockSpec((tb, 1), lambda i, be, nu: (i, 0))],
        out_specs=pl.BlockSpec((tb * (nct + SLAB_PAD), LANES), lambda i, be, nu: (i, 0)),
    )
    return pl.pallas_call(
        functools.partial(_expert_down_kernel, tb=tb, nct=nct),
        out_shape=_sds((n_blocks * tb * (nct + SLAB_PAD), LANES), F32),
        grid_spec=grid_spec,
        compiler_params=_params(("arbitrary",)),
        name="expert_down",
    )(block_e, n_used, act, wd, bd, row_w)


def _combine_kernel(d0_ref, d1_ref, ys_hbm, x1_ref, gf_ref, gpost_ref, o_ref, buf, ff, sem, *, tc, nct):
    i = pl.program_id(0)
    last = pl.num_programs(0) - 1
    n_rows = TOP_K * tc
    sp = nct + SLAB_PAD

    def row_copy(r, j, slot):
        return pltpu.make_async_copy(ys_hbm.at[pl.ds(pl.multiple_of(r * sp, 8), nct), :],
                                     buf.at[slot, pl.ds(pl.multiple_of(j * sp, 8), nct), :], sem.at[slot])

    def gather(dref, slot):
        def body(j, carry):
            row_copy(dref[0, j], j, slot).start()
            return carry
        lax.fori_loop(0, n_rows, body, 0, unroll=8)

    def wait(slot):
        pltpu.make_async_copy(ys_hbm.at[pl.ds(0, n_rows * nct), :], buf.at[slot, pl.ds(0, n_rows * nct), :],
                              sem.at[slot]).wait()

    @pl.when(i == 0)
    def _():
        gather(d0_ref, 0)

    @pl.when(i < last)
    def _():
        gather(d1_ref, (i + 1) % 2)

    slot = i % 2
    wait(slot)
    for c in range(nct):
        acc = buf[slot, pl.ds(c, tc, stride=sp), :]
        for k in range(1, TOP_K):
            acc = acc + buf[slot, pl.ds(k * tc * sp + c, tc, stride=sp), :]
        ff[:, c * LANES:(c + 1) * LANES] = acc
    f = ff[...]
    y = f * lax.rsqrt(jnp.mean(f * f, axis=-1, keepdims=True) + RMS_EPS) * gpost_ref[...]
    o_ref[...] = x1_ref[...] + gf_ref[...] * y


def _combine(dest3, ys, x1, gate_f, g_post, seq):
    n_tok, d = x1.shape
    n_steps, _, n_rows = dest3.shape
    tc = n_rows // TOP_K
    nct = d // LANES
    spb = seq // tc
    return pl.pallas_call(
        functools.partial(_combine_kernel, tc=tc, nct=nct),
        out_shape=_sds((n_tok, d), F32),
        grid=(n_steps,),
        in_specs=[pl.BlockSpec((None, 1, n_rows), lambda i: (i, 0, 0), memory_space=pltpu.SMEM),
                  pl.BlockSpec((None, 1, n_rows), lambda i: (jnp.minimum(i + 1, n_steps - 1), 0, 0),
                               memory_space=pltpu.SMEM),
                  pl.BlockSpec(memory_space=pl.ANY),
                  pl.BlockSpec((tc, d), lambda i: (i, 0)),
                  pl.BlockSpec((None, 1, d), lambda i: (i // spb, 0, 0)),
                  pl.BlockSpec((1, d), lambda i: (0, 0))],
        out_specs=pl.BlockSpec((tc, d), lambda i: (i, 0)),
        scratch_shapes=[pltpu.VMEM((2, n_rows * (nct + SLAB_PAD), LANES), F32), pltpu.VMEM((tc, d), F32),
                        pltpu.SemaphoreType.DMA((2,))],
        compiler_params=_params(("arbitrary",)),
        name="expert_combine",
    )(dest3, dest3, ys, x1, gate_f, g_post)


def _routing_tables(top_idx, n_experts, tb):
    n_tok = top_idx.shape[0]
    n_assign = n_tok * TOP_K
    flat_e = top_idx.reshape(-1)
    onehot = (flat_e[:, None] == jnp.arange(n_experts, dtype=jnp.int32)[None, :]).astype(jnp.int32)
    csum = jnp.cumsum(onehot, axis=0)
    rank = jnp.sum(csum * onehot, axis=1) - 1
    sizes = csum[-1]
    padded = (sizes + tb - 1) // tb * tb
    pad_end = jnp.cumsum(padded)
    pad_start = pad_end - padded
    dest = (pad_start[flat_e] + rank).astype(jnp.int32)
    n_rows = -(-n_assign // tb) * tb + n_experts * tb
    n_blocks = n_rows // tb
    tok = (jnp.arange(n_assign, dtype=jnp.int32) // TOP_K)
    row_tok = jnp.zeros((n_rows,), jnp.int32).at[dest].set(tok)
    block_e = jnp.minimum(
        jnp.searchsorted(pad_end, jnp.arange(n_blocks, dtype=jnp.int32) * tb, side="right"),
        n_experts - 1).astype(jnp.int32)
    n_used = (pad_end[-1] // tb).astype(jnp.int32).reshape(1)
    return dest, row_tok, block_e, n_used, n_rows


def _moe(h2s, top_idx, top_w, wg, bg, wu, bu, wd, bd, x1, gate_f, g_post, seq):
    n_tok, d = x1.shape
    n_experts = wg.shape[0]
    tb = MOE_ROWS
    nct = d // LANES
    dest, row_tok, block_e, n_used, n_rows = _routing_tables(top_idx, n_experts, tb)
    row_w = jnp.zeros((n_rows,), F32).at[dest].set(top_w.reshape(-1))
    n_blocks = n_rows // tb
    act = _expert_up(block_e, n_used, row_tok.reshape(n_blocks, 1, tb), h2s, wg, bg[:, None, :], wu,
                     bu[:, None, :], nct)
    ys = _expert_down(block_e, n_used, act, wd, bd[:, None, :], row_w[:, None], tb)
    tc = 128
    dest3 = dest.reshape(n_tok // tc, tc, TOP_K).transpose(0, 2, 1).reshape(n_tok // tc, 1, TOP_K * tc)
    return _combine(dest3, ys, x1, gate_f, g_post, seq)


def _layer(x, c8, p):
    b, s, d = x.shape
    n_tok = b * s
    hw = p["hy_skip"].shape[0]
    gw = GROUP_WIDTH
    assert hw == COL_BLOCK and gw == COL_BLOCK and d % COL_BLOCK == 0

    mod = _adaln(c8, p["w_ada"], p["b_ada"][None])[:b]
    shift_m, scale_m, gate_m, shift_f, scale_f, gate_f = [m[:, None, :] for m in jnp.split(mod, 6, axis=-1)]

    h, h4, h16 = _prenorm_mix(x, p["g_pre_mix"][None], scale_m, shift_m)

    w_in = p["w_in"].astype(BF16)
    nd = d // COL_BLOCK
    nat_blocks = [0, 1, 2, 3, 6, 9] + list(range(12, 12 + 2 * nd))
    pn = _matmul(h.reshape(n_tok, d), w_in, nat_blocks, BF16, "in_proj_natural")
    qkv1 = _matmul(h4.reshape(n_tok, d), w_in, [4, 7, 10], BF16, "in_proj_dil4")
    qkv2 = _matmul(h16.reshape(n_tok, d), w_in, [5, 8, 11], BF16, "in_proj_dil16")

    r, pitch = _fft_radix(s)
    hfb = _hyena_filters(s, r, pitch, p["hy_f_w1"], p["hy_f_b1"], p["hy_f_w2"], p["hy_f_b2"], p["hy_f_w3"],
                         p["hy_f_b3"], p["hy_f_freq"], p["hy_f_wout"])
    zin, x1c = _shortconv(pn.reshape(b, s, -1), p["hy_conv_w"], p["hy_conv_b"], hw, r, pitch)
    yconv = _long_conv(zin, hfb, r, pitch)

    slopes = _alibi_slopes(N_GROUPS * HEADS_PER_GROUP).reshape(N_GROUPS, HEADS_PER_GROUP)
    o0, l0 = _dilated_attention(pn.reshape(b, s, -1), s, 1, slopes[0], col0=3)
    o1, l1 = _dilated_attention(qkv1.reshape(b * 4, s // 4, 3 * gw), s // 4, 4, slopes[1])
    o2, l2 = _dilated_attention(qkv2.reshape(b * 16, s // 16, 3 * gw), s // 16, 16, slopes[2])

    merged = _merge(yconv.reshape(-1, hw), zin.reshape(-1, hw), x1c.reshape(n_tok, hw), p["hy_skip"][None],
                    o0.reshape(n_tok, gw), l0.reshape(n_tok, gw),
                    o1.reshape(b, 4, s // 4, gw), l1.reshape(b, 4, s // 4, gw),
                    o2.reshape(b, 16, s // 16, gw), l2.reshape(b, 16, s // 16, gw),
                    pn, 6, 6 + nd, p["w_proj_hyena"].astype(BF16), p["w_proj_attn"].astype(BF16), b, s, r, pitch)
    mo = _matmul(merged, p["w_out"].astype(BF16), list(range(nd)), F32, "out_proj")

    x1, h2s, top_idx, top_w = _post_mix_and_route(mo, x, gate_m, p["g_post_mix"][None], p["g_pre_ffn"][None],
                                                  scale_f, shift_f, p["w_router"], p["b_router"])
    out = _moe(h2s, top_idx[:, :TOP_K], top_w[:, :TOP_K], p["w_gate"].astype(BF16), p["b_gate"],
               p["w_up"].astype(BF16), p["b_up"], p["w_down"].astype(BF16), p["b_down"], x1, gate_f,
               p["g_post_ffn"][None], s)
    return out.reshape(b, s, d)


def kernel(x, c, w_ada, b_ada, g_pre_mix, g_post_mix, g_pre_ffn, g_post_ffn, w_in, hy_conv_w, hy_conv_b, hy_skip, hy_f_w1, hy_f_b1, hy_f_w2, hy_f_b2, hy_f_w3, hy_f_b3, hy_f_freq, hy_f_wout, w_proj_hyena, w_proj_attn, w_out, w_router, b_router, w_gate, b_gate, w_up, b_up, w_down, b_down):
    names = ("w_ada", "b_ada", "g_pre_mix", "g_post_mix", "g_pre_ffn", "g_post_ffn", "w_in", "hy_conv_w",
             "hy_conv_b", "hy_skip", "hy_f_w1", "hy_f_b1", "hy_f_w2", "hy_f_b2", "hy_f_w3", "hy_f_b3",
             "hy_f_freq", "hy_f_wout", "w_proj_hyena", "w_proj_attn", "w_out", "w_router", "b_router",
             "w_gate", "b_gate", "w_up", "b_up", "w_down", "b_down")
    stacked = (w_ada, b_ada, g_pre_mix, g_post_mix, g_pre_ffn, g_post_ffn, w_in, hy_conv_w, hy_conv_b, hy_skip,
               hy_f_w1, hy_f_b1, hy_f_w2, hy_f_b2, hy_f_w3, hy_f_b3, hy_f_freq, hy_f_wout, w_proj_hyena,
               w_proj_attn, w_out, w_router, b_router, w_gate, b_gate, w_up, b_up, w_down, b_down)
    depth = w_ada.shape[0]
    b = x.shape[0]
    c8 = jnp.zeros((8, c.shape[1]), F32).at[:b].set(c)
    for l in range(depth):
        x = _layer(x, c8, {k: v[l] for k, v in zip(names, stacked)})
    return x
```

```python
import functools
import math

import jax
import jax.numpy as jnp
import numpy as np
from jax import lax
from jax.experimental import pallas as pl
from jax.experimental.pallas import tpu as pltpu

F32 = jnp.float32
BF16 = jnp.bfloat16
HIGHEST = lax.Precision.HIGHEST

LANES = 128
HEAD_DIM = 128
HEADS_PER_GROUP = 8
DILATED_GROUPS = ((128, 1), (512, 4), (2048, 16))
N_GROUPS = len(DILATED_GROUPS)
GROUP_WIDTH = HEADS_PER_GROUP * HEAD_DIM
ATTN_SIDE = 64
TOP_K = 4
SWIGLU_LIMIT = 7.0
SWIGLU_ALPHA = 1.702
RMS_EPS = 1e-6
NEG_INF = -1e30
HYENA_N_BANDS = 16
HYENA_DECAY_TARGET = 1e-2
HYENA_FAST_DECAY_PCT = 0.3
HYENA_SLOW_DECAY_PCT = 1.5
COL_BLOCK = 1024
MOE_ROWS = 256
SLAB_PAD = 8
VMEM_LIMIT = 56 * 1024 * 1024


def _pack_bf16_pairs(x):
    half = x.shape[1] // 2
    bits = lax.bitcast_convert_type(x.astype(BF16).astype(F32), jnp.uint32)
    return bits[:, :half] | (bits[:, half:] >> 16)


def _unpack_bf16_pairs(w):
    hi = lax.bitcast_convert_type(w & jnp.uint32(0xFFFF0000), F32)
    lo = lax.bitcast_convert_type(w << 16, F32)
    return hi, lo


def _params(sem, vmem=VMEM_LIMIT):
    return pltpu.CompilerParams(dimension_semantics=sem, vmem_limit_bytes=vmem)


def _sds(shape, dtype):
    return jax.ShapeDtypeStruct(shape, dtype)


def _ada_kernel(c_ref, w_ref, b_ref, o_ref):
    c = c_ref[...]
    sc = (c * jax.nn.sigmoid(c)).astype(BF16)
    o_ref[...] = jnp.dot(sc, w_ref[...].astype(BF16), preferred_element_type=F32) + b_ref[...]


def _adaln(c8, w_ada, b_ada):
    d, cols = w_ada.shape
    tn = 512
    return pl.pallas_call(
        _ada_kernel,
        out_shape=_sds((8, cols), F32),
        grid=(cols // tn,),
        in_specs=[pl.BlockSpec((8, d), lambda j: (0, 0)),
                  pl.BlockSpec((d, tn), lambda j: (0, j)),
                  pl.BlockSpec((1, tn), lambda j: (0, j))],
        out_specs=pl.BlockSpec((8, tn), lambda j: (0, j)),
        compiler_params=_params(("arbitrary",)),
        name="adaln",
    )(c8, w_ada, b_ada)


def _prenorm_kernel(x_ref, g_ref, sc_ref, sh_ref, o_ref, o4_ref, o16_ref, scr_ref, *, ts, d_model):
    x = x_ref[...]
    ms = jnp.mean(x * x, axis=-1, keepdims=True)
    h = x * lax.rsqrt(ms + RMS_EPS) * g_ref[...] * (1.0 + sc_ref[...]) + sh_ref[...]
    o_ref[...] = h.astype(BF16)
    nct = d_model // LANES
    for c in range(nct):
        scr_ref[c] = h[:, c * LANES:(c + 1) * LANES]
    for dil, oref in ((4, o4_ref), (16, o16_ref)):
        for r in range(dil):
            for c in range(nct):
                oref[r, :, c * LANES:(c + 1) * LANES] = scr_ref[c, pl.ds(r, ts // dil, stride=dil), :].astype(BF16)


def _prenorm_mix(x, g, scale, shift):
    b, s, d = x.shape
    ts = 256
    kern = functools.partial(_prenorm_kernel, ts=ts, d_model=d)
    return pl.pallas_call(
        kern,
        out_shape=(_sds((b, s, d), BF16), _sds((b, 4, s // 4, d), BF16), _sds((b, 16, s // 16, d), BF16)),
        grid=(b, s // ts),
        in_specs=[pl.BlockSpec((None, ts, d), lambda bi, i: (bi, i, 0)),
                  pl.BlockSpec((1, d), lambda bi, i: (0, 0)),
                  pl.BlockSpec((None, 1, d), lambda bi, i: (bi, 0, 0)),
                  pl.BlockSpec((None, 1, d), lambda bi, i: (bi, 0, 0))],
        out_specs=(pl.BlockSpec((None, ts, d), lambda bi, i: (bi, i, 0)),
                   pl.BlockSpec((None, 4, ts // 4, d), lambda bi, i: (bi, 0, i, 0)),
                   pl.BlockSpec((None, 16, ts // 16, d), lambda bi, i: (bi, 0, i, 0))),
        scratch_shapes=[pltpu.VMEM((d // LANES, ts, LANES), F32)],
        compiler_params=_params(("arbitrary", "arbitrary")),
        name="prenorm_mix",
    )(x, g, scale, shift)


def _mm_kernel(tbl_ref, a_ref, w_ref, o_ref):
    del tbl_ref
    o_ref[...] = jnp.dot(a_ref[...], w_ref[...], preferred_element_type=F32).astype(o_ref.dtype)


def _matmul(a, w, col_blocks, out_dtype, name, tm=1024, tn=COL_BLOCK):
    m, k = a.shape
    tm = min(tm, m)
    nb = len(col_blocks)
    tbl = jnp.asarray(col_blocks, jnp.int32)
    grid_spec = pltpu.PrefetchScalarGridSpec(
        num_scalar_prefetch=1,
        grid=(m // tm, nb),
        in_specs=[pl.BlockSpec((tm, k), lambda i, j, t: (i, 0)),
                  pl.BlockSpec((k, tn), lambda i, j, t: (0, t[j]))],
        out_specs=pl.BlockSpec((tm, tn), lambda i, j, t: (i, j)),
    )
    return pl.pallas_call(
        _mm_kernel,
        out_shape=_sds((m, nb * tn), out_dtype),
        grid_spec=grid_spec,
        compiler_params=_params(("arbitrary", "arbitrary")),
        name=name,
    )(tbl, a, w)


def _filter_kernel(z_ref, w1_ref, b1_ref, w2_ref, b2_ref, w3_ref, b3_ref, fr_ref, wo_ref, dl_ref,
                   h_ref, *, hw, r, pitch):
    def dot(a, b):
        return jnp.dot(a, b, precision=HIGHEST, preferred_element_type=F32)

    z = z_ref[...]
    fr = fr_ref[...]
    h = jnp.sin(fr * (dot(z, w1_ref[...]) + b1_ref[...]))
    h = jnp.sin(fr * (dot(h, w2_ref[...]) + b2_ref[...]))
    h = jnp.sin(fr * (dot(h, w3_ref[...]) + b3_ref[...]))
    filt = dot(h, wo_ref[...])
    decay = jnp.exp(-z[:, 0:1] * dl_ref[...])
    hf = filt[:, :hw] * decay
    hb = filt[:, hw:] * decay
    pad = jnp.zeros((pitch - r, hw), F32)
    for g in range(z.shape[0] // r):
        h_ref[0, g * pitch:g * pitch + r, :] = hf[g * r:(g + 1) * r]
        h_ref[1, g * pitch:g * pitch + r, :] = hb[g * r:(g + 1) * r]
        h_ref[0, g * pitch + r:(g + 1) * pitch, :] = pad
        h_ref[1, g * pitch + r:(g + 1) * pitch, :] = pad


def _hyena_filters(length, r, pitch, w1, b1, w2, b2, w3, b3, freq, wout):
    emb, fw = w1.shape
    hw = wout.shape[1] // 2
    t = np.linspace(0.0, 1.0, length)[:, None]
    bands = np.linspace(1e-4, HYENA_N_BANDS - 1, HYENA_N_BANDS)[None, :]
    ang = (2.0 * math.pi / length) * np.arange(length)[:, None] * bands
    z = np.concatenate([t, np.cos(ang), -np.sin(ang)], axis=-1)
    zpad = np.zeros((length, LANES), np.float32)
    zpad[:, :emb] = z
    w1p = jnp.zeros((LANES, fw), F32).at[:emb].set(w1)
    min_decay = math.log(HYENA_DECAY_TARGET) / HYENA_FAST_DECAY_PCT
    max_decay = math.log(HYENA_DECAY_TARGET) / HYENA_SLOW_DECAY_PCT
    deltas = np.abs(np.linspace(min_decay, max_decay, hw))[None, :].astype(np.float32)
    tl = min(1024, length)
    full = lambda shape: pl.BlockSpec(shape, lambda i: (0,) * len(shape))
    return pl.pallas_call(
        functools.partial(_filter_kernel, hw=hw, r=r, pitch=pitch),
        out_shape=_sds((2, length // r * pitch, hw), F32),
        grid=(length // tl,),
        in_specs=[pl.BlockSpec((tl, LANES), lambda i: (i, 0)),
                  full((LANES, fw)), full((1, fw)), full((fw, fw)), full((1, fw)),
                  full((fw, fw)), full((1, fw)), full((1, fw)), full((fw, 2 * hw)), full((1, hw))],
        out_specs=pl.BlockSpec((2, tl // r * pitch, hw), lambda i: (0, i, 0)),
        compiler_params=_params(("arbitrary",)),
        name="hyena_filters",
    )(jnp.asarray(zpad), w1p, b1[None], w2, b2[None], w3, b3[None], freq[None], wout, jnp.asarray(deltas))


def _shortconv_kernel(u_ref, up_ref, un_ref, w_ref, b_ref, z_ref, x1_ref, *, ts, hw, r, pitch):
    i = pl.program_id(1)
    last = pl.num_programs(1) - 1
    u = u_ref[...].astype(F32)
    prev_blk = up_ref[...].astype(F32)
    next_blk = un_ref[...].astype(F32)
    prev_row = jnp.where(i > 0, prev_blk[15:16, :], 0.0)
    next_row = jnp.where(i < last, next_blk[0:1, :], 0.0)
    row = lax.broadcasted_iota(jnp.int32, u.shape, 0)
    um = jnp.where(row == 0, prev_row, pltpu.roll(u, 1, 0))
    up = jnp.where(row == ts - 1, next_row, pltpu.roll(u, ts - 1, 0))
    w = w_ref[...]
    uc = w[0:1] * um + w[1:2] * u + w[2:3] * up + b_ref[...]
    x1_ref[...] = uc[:, :hw]
    z = uc[:, 2 * hw:] * uc[:, hw:2 * hw]
    pad = jnp.zeros((pitch - r, hw), F32)
    for g in range(ts // r):
        z_ref[g * pitch:g * pitch + r, :] = z[g * r:(g + 1) * r]
        z_ref[g * pitch + r:(g + 1) * pitch, :] = pad


def _shortconv(pn3, conv_w, conv_b, hw, r, pitch):
    b, s, _ = pn3.shape
    ts = 512
    w3 = 3 * hw
    nh = s // 16
    return pl.pallas_call(
        functools.partial(_shortconv_kernel, ts=ts, hw=hw, r=r, pitch=pitch),
        out_shape=(_sds((b, s // r * pitch, hw), F32), _sds((b, s, hw), F32)),
        grid=(b, s // ts),
        in_specs=[pl.BlockSpec((None, ts, w3), lambda bi, i: (bi, i, 0)),
                  pl.BlockSpec((None, 16, w3), lambda bi, i: (bi, jnp.maximum(i * (ts // 16) - 1, 0), 0)),
                  pl.BlockSpec((None, 16, w3), lambda bi, i: (bi, jnp.minimum((i + 1) * (ts // 16), nh - 1), 0)),
                  pl.BlockSpec((3, w3), lambda bi, i: (0, 0)),
                  pl.BlockSpec((1, w3), lambda bi, i: (0, 0))],
        out_specs=(pl.BlockSpec((None, ts // r * pitch, hw), lambda bi, i: (bi, i, 0)),
                   pl.BlockSpec((None, ts, hw), lambda bi, i: (bi, i, 0))),
        compiler_params=_params(("arbitrary", "arbitrary")),
        name="hyena_shortconv",
    )(pn3, pn3, pn3, conv_w, conv_b[None])


FFT_GROUP = 4


def _split_bf16(a):
    hi = a.astype(BF16)
    return hi, (a - hi.astype(F32)).astype(BF16)


def _dot3(fh, fl, x):
    xh, xl = _split_bf16(x)
    d = lambda a, b: jnp.dot(a, b, preferred_element_type=F32)
    return d(fh, xh) + (d(fh, xl) + d(fl, xh))


def _dft_tables(r):
    idx = np.arange(r)
    ang = 2.0 * np.pi * np.outer(idx, idx) / r
    cos, sin = np.cos(ang), np.sin(ang)
    fa_half = np.concatenate([cos[:, :r // 2], -sin[:, :r // 2]], axis=0)
    fbig = np.block([[cos, sin], [-sin, cos]])
    fconj = np.block([[cos, -sin], [sin, cos]])
    gfin = np.concatenate([cos[:r // 2], -sin[:r // 2]], axis=1)
    tang = 2.0 * np.pi * np.outer(idx, idx) / (r * r)
    tw = np.stack([np.cos(tang), -np.sin(tang)], axis=-1)
    split = lambda a: _split_bf16(jnp.asarray(a.astype(np.float32)))
    return split(fa_half), split(fbig), split(fconj), split(gfin), jnp.asarray(tw.astype(np.float32))


def _fft_a_kernel(x_ref, fh_ref, fl_ref, o_ref, *, r, k1, pitch):
    fh, fl = fh_ref[...], fl_ref[...]
    pad = jnp.zeros((pitch - r, LANES), F32)
    for g in range(r):
        o_ref[0, g * pitch + r:(g + 1) * pitch, :] = pad
        o_ref[1, g * pitch + r:(g + 1) * pitch, :] = pad

    def body(g, carry):
        n2 = g * FFT_GROUP
        xs = jnp.concatenate([x_ref[pl.ds(n2 + k, k1, stride=pitch), :] for k in range(FFT_GROUP)], axis=1)
        a = _dot3(fh, fl, xs)
        for k in range(FFT_GROUP):
            o_ref[0, pl.ds(n2 + k, r, stride=pitch), :] = a[:r, k * LANES:(k + 1) * LANES]
            o_ref[1, pl.ds(n2 + k, r, stride=pitch), :] = a[r:, k * LANES:(k + 1) * LANES]
        return carry

    lax.fori_loop(0, r // FFT_GROUP, body, 0)


def _fft_stage_a(x, fmat, r, pitch):
    bx, rows, c = x.shape
    k1 = rows // pitch
    return pl.pallas_call(
        functools.partial(_fft_a_kernel, r=r, k1=k1, pitch=pitch),
        out_shape=_sds((bx, 2, r * pitch, c), F32),
        grid=(bx, c // LANES),
        in_specs=[pl.BlockSpec((None, rows, LANES), lambda b, ci: (b, 0, ci)),
                  pl.BlockSpec((2 * r, k1), lambda b, ci: (0, 0)),
                  pl.BlockSpec((2 * r, k1), lambda b, ci: (0, 0))],
        out_specs=pl.BlockSpec((None, 2, r * pitch, LANES), lambda b, ci: (b, 0, 0, ci)),
        compiler_params=_params(("arbitrary", "arbitrary")),
        name="fft_stage_a",
    )(x, *fmat)


def _twiddled(a_ref, b, tr, ti, r):
    are, aim = a_ref[b, 0, :r, :], a_ref[b, 1, :r, :]
    return jnp.concatenate([are * tr - aim * ti, are * ti + aim * tr], axis=0)


def _fft_mk_kernel(a_ref, tw_ref, h0_ref, fbh_ref, fbl_ref, o_ref, *, r, scale):
    tr, ti = tw_ref[:, 0:1], tw_ref[:, 1:2]
    fbh, fbl = fbh_ref[...], fbl_ref[...]
    xf = _dot3(fbh, fbl, _twiddled(a_ref, 0, tr, ti, r))
    xb = _dot3(fbh, fbl, _twiddled(a_ref, 1, tr, ti, r))
    o_ref[0] = (xf[:r] + xb[:r] - h0_ref[...]) * scale
    o_ref[1] = (xf[r:] - xb[r:]) * scale


def _fft_filter_spectrum(a, tw, h0, fbig, r, pitch):
    c = a.shape[-1]
    return pl.pallas_call(
        functools.partial(_fft_mk_kernel, r=r, scale=1.0 / (r * r)),
        out_shape=_sds((2, r * r, c), F32),
        grid=(r,),
        in_specs=[pl.BlockSpec((2, 2, pitch, c), lambda k: (0, 0, k, 0)),
                  pl.BlockSpec((None, r, 2), lambda k: (k, 0, 0)),
                  pl.BlockSpec((1, c), lambda k: (0, 0)),
                  pl.BlockSpec((2 * r, 2 * r), lambda k: (0, 0)),
                  pl.BlockSpec((2 * r, 2 * r), lambda k: (0, 0))],
        out_specs=pl.BlockSpec((2, r, c), lambda k: (0, k, 0)),
        compiler_params=_params(("arbitrary",)),
        name="fft_filter_spectrum",
    )(a, tw, h0, *fbig)


def _fft_m_kernel(a_ref, ks_ref, tw_ref, fbh_ref, fbl_ref, fch_ref, fcl_ref, o_ref, *, r):
    tr, ti = tw_ref[:, 0:1], tw_ref[:, 1:2]
    x = _dot3(fbh_ref[...], fbl_ref[...], _twiddled(a_ref, 0, tr, ti, r))
    xre, xim = x[:r], x[r:]
    kre, kim = ks_ref[0], ks_ref[1]
    c = _dot3(fch_ref[...], fcl_ref[...],
              jnp.concatenate([xre * kre - xim * kim, xre * kim + xim * kre], axis=0))
    cre, cim = c[:r], c[r:]
    o_ref[0, :r, :] = cre * tr + cim * ti
    o_ref[1, :r, :] = cim * tr - cre * ti
    o_ref[:, r:, :] = jnp.zeros((2,) + (o_ref.shape[1] - r, o_ref.shape[2]), F32)


def _fft_stage_m(a, ks, tw, fbig, fconj, r, pitch):
    b, _, _, c = a.shape
    a5 = a.reshape(b, 1, 2, r * pitch, c)
    mat = pl.BlockSpec((2 * r, 2 * r), lambda k, bi: (0, 0))
    return pl.pallas_call(
        functools.partial(_fft_m_kernel, r=r),
        out_shape=_sds(a.shape, F32),
        grid=(r, b),
        in_specs=[pl.BlockSpec((None, 1, 2, pitch, c), lambda k, bi: (bi, 0, 0, k, 0)),
                  pl.BlockSpec((2, r, c), lambda k, bi: (0, k, 0)),
                  pl.BlockSpec((None, r, 2), lambda k, bi: (k, 0, 0)),
                  mat, mat, mat, mat],
        out_specs=pl.BlockSpec((None, 2, pitch, c), lambda k, bi: (bi, 0, k, 0)),
        compiler_params=_params(("arbitrary", "arbitrary")),
        name="fft_stage_m",
    )(a5, ks, tw, *fbig, *fconj)


def _fft_f_kernel(d_ref, gh_ref, gl_ref, o_ref, *, r, pitch):
    gh, gl = gh_ref[...], gl_ref[...]
    pad = jnp.zeros((pitch - r, LANES), F32)
    for g in range(r // 2):
        o_ref[g * pitch + r:(g + 1) * pitch, :] = pad

    def body(g, carry):
        n2 = g * FFT_GROUP
        dcat = jnp.concatenate(
            [jnp.concatenate([d_ref[0, pl.ds(n2 + k, r, stride=pitch), :],
                              d_ref[1, pl.ds(n2 + k, r, stride=pitch), :]], axis=0) for k in range(FFT_GROUP)],
            axis=1)
        y = _dot3(gh, gl, dcat)
        for k in range(FFT_GROUP):
            o_ref[pl.ds(n2 + k, r // 2, stride=pitch), :] = y[:, k * LANES:(k + 1) * LANES]
        return carry

    lax.fori_loop(0, r // FFT_GROUP, body, 0)


def _fft_stage_f(dmat, gfin, r, pitch):
    b, _, rows, c = dmat.shape
    return pl.pallas_call(
        functools.partial(_fft_f_kernel, r=r, pitch=pitch),
        out_shape=_sds((b, rows // 2, c), F32),
        grid=(b, c // LANES),
        in_specs=[pl.BlockSpec((None, 2, rows, LANES), lambda bi, ci: (bi, 0, 0, ci)),
                  pl.BlockSpec((r // 2, 2 * r), lambda bi, ci: (0, 0)),
                  pl.BlockSpec((r // 2, 2 * r), lambda bi, ci: (0, 0))],
        out_specs=pl.BlockSpec((None, rows // 2, LANES), lambda bi, ci: (bi, 0, ci)),
        compiler_params=_params(("arbitrary", "arbitrary")),
        name="fft_stage_f",
    )(dmat, *gfin)


def _fft_radix(length):
    r = int(round(math.sqrt(2 * length)))
    assert r * r == 2 * length, "sequence length must make 2L a perfect square"
    return r, r + 8


def _long_conv(zin, hfb, r, pitch):
    fa_half, fbig, fconj, gfin, tw = _dft_tables(r)
    ks = _fft_filter_spectrum(_fft_stage_a(hfb, fa_half, r, pitch), tw, hfb[1, 0:1, :], fbig, r, pitch)
    a = _fft_stage_a(zin, fa_half, r, pitch)
    dmat = _fft_stage_m(a, ks, tw, fbig, fconj, r, pitch)
    return _fft_stage_f(dmat, gfin, r, pitch)


def _alibi_slopes(n_heads):
    def pow2_slopes(m):
        start = 2.0 ** (-8.0 / m)
        return [start ** (i + 1) for i in range(m)]
    base = 2 ** int(math.floor(math.log2(n_heads)))
    slopes = pow2_slopes(base)
    if base < n_heads:
        slopes = slopes + pow2_slopes(2 * base)[0::2][: n_heads - base]
    return np.array(sorted(slopes, reverse=True), dtype=np.float32)


def _attn_kernel(q_ref, k_ref, kp_ref, kn_ref, v_ref, vp_ref, vn_ref, o_ref, l_ref, *, tq, n, dil, slopes):
    i = pl.program_id(1)
    side = ATTN_SIDE
    nk = tq + 2 * side
    row = lax.broadcasted_iota(jnp.int32, (tq, nk), 0)
    col = lax.broadcasted_iota(jnp.int32, (tq, nk), 1)
    rel = jnp.abs(col - side - row)
    kglob = i * tq + col - side
    valid = (rel <= side) & (kglob >= 0) & (kglob < n)
    dist = (rel * dil).astype(F32)
    scale = HEAD_DIM ** -0.5
    for h in range(HEADS_PER_GROUP):
        hs = slice(h * HEAD_DIM, (h + 1) * HEAD_DIM)
        q = q_ref[:, hs]
        kc = jnp.concatenate([kp_ref[:, hs], k_ref[:, hs], kn_ref[:, hs]], axis=0)
        vc = jnp.concatenate([vp_ref[:, hs], v_ref[:, hs], vn_ref[:, hs]], axis=0)
        s = lax.dot_general(q, kc, (((1,), (1,)), ((), ())), preferred_element_type=F32) * scale
        s = jnp.where(valid, s - float(slopes[h]) * dist, NEG_INF)
        m = jnp.max(s, axis=-1, keepdims=True)
        p = jnp.exp(s - m)
        den = jnp.sum(p, axis=-1, keepdims=True)
        o = jnp.dot(p.astype(BF16), vc, preferred_element_type=F32) / den
        o_ref[:, hs] = o.astype(o_ref.dtype)
        l_ref[:, hs] = jnp.broadcast_to(m + jnp.log(den), (tq, HEAD_DIM))


def _dilated_attention(qkv, n, dil, slopes, col0=0):
    streams = qkv.shape[0]
    tq = min(256, n)
    side = ATTN_SIDE
    nh = n // side
    gw = GROUP_WIDTH
    main = lambda cb: pl.BlockSpec((None, tq, gw), lambda s, i: (s, i, col0 + cb))
    prev = lambda cb: pl.BlockSpec((None, side, gw),
                                   lambda s, i: (s, jnp.maximum(i * (tq // side) - 1, 0), col0 + cb))
    nxt = lambda cb: pl.BlockSpec((None, side, gw),
                                  lambda s, i: (s, jnp.minimum((i + 1) * (tq // side), nh - 1), col0 + cb))
    return pl.pallas_call(
        functools.partial(_attn_kernel, tq=tq, n=n, dil=dil, slopes=tuple(float(v) for v in slopes)),
        out_shape=(_sds((streams, n, gw), BF16), _sds((streams, n, gw), F32)),
        grid=(streams, n // tq),
        in_specs=[main(0), main(1), prev(1), nxt(1), main(2), prev(2), nxt(2)],
        out_specs=(pl.BlockSpec((None, tq, gw), lambda s, i: (s, i, 0)),
                   pl.BlockSpec((None, tq, gw), lambda s, i: (s, i, 0))),
        compiler_params=_params(("arbitrary", "arbitrary")),
        name=f"dilated_attention_d{dil}",
    )(qkv, qkv, qkv, qkv, qkv, qkv, qkv)


def _merge_kernel(y_ref, z_ref, x1_ref, skip_ref, o0_ref, l0_ref, o1_ref, l1_ref, o2_ref, l2_ref,
                  ghy_ref, gat_ref, wh_ref, wa_ref, out_ref, hy_s, at_s, so1, sl1, so2, sl2, *, tm, r, pitch):
    j = pl.program_id(1)
    nct = GROUP_WIDTH // LANES

    @pl.when(j == 0)
    def _():
        for g in range(tm // r):
            rows = slice(g * pitch, g * pitch + r)
            hy_s[g * r:(g + 1) * r, :] = ((y_ref[rows, :] + z_ref[rows, :] * skip_ref[...])
                                          * x1_ref[g * r:(g + 1) * r, :]).astype(BF16)
        for dil, oref, lref, so, sl in ((4, o1_ref, l1_ref, so1, sl1), (16, o2_ref, l2_ref, so2, sl2)):
            for res in range(dil):
                for c in range(nct):
                    cs = slice(c * LANES, (c + 1) * LANES)
                    so[c, pl.ds(res, tm // dil, stride=dil), :] = oref[res, :, cs].astype(F32)
                    sl[c, pl.ds(res, tm // dil, stride=dil), :] = lref[res, :, cs]
        for c in range(nct):
            cs = slice(c * LANES, (c + 1) * LANES)
            a0, a1, a2 = l0_ref[:, cs], sl1[c], sl2[c]
            m = jnp.maximum(jnp.maximum(a0, a1), a2)
            e0, e1, e2 = jnp.exp(a0 - m), jnp.exp(a1 - m), jnp.exp(a2 - m)
            at = (e0 * o0_ref[:, cs].astype(F32) + e1 * so1[c] + e2 * so2[c]) / (e0 + e1 + e2)
            at_s[:, cs] = at.astype(BF16)

    acc_h = jnp.dot(hy_s[...], wh_ref[j], preferred_element_type=F32)
    acc_a = jnp.dot(at_s[...], wa_ref[j], preferred_element_type=F32)
    out = (jax.nn.sigmoid(ghy_ref[...].astype(F32)) * acc_h + jax.nn.sigmoid(gat_ref[...].astype(F32)) * acc_a)
    out_ref[...] = out.astype(out_ref.dtype)


def _merge(yconv, zin, x1c, skip, o0, l0, o1, l1, o2, l2, pn, ghy_block0, gat_block0, wh, wa, batch, seq, r, pitch):
    n_tok, hw = x1c.shape
    d_model = wh.shape[1]
    gw = GROUP_WIDTH
    tm = 256
    tn = COL_BLOCK
    spb = seq // tm
    row = lambda width: pl.BlockSpec((tm, width), lambda i, j: (i, 0))
    prow = pl.BlockSpec((tm // r * pitch, hw), lambda i, j: (i, 0))
    res = lambda dil: pl.BlockSpec((None, dil, tm // dil, gw), lambda i, j: (i // spb, 0, i % spb, 0))
    nj = d_model // tn
    wh3 = wh.reshape(hw, nj, tn).transpose(1, 0, 2)
    wa3 = wa.reshape(gw, nj, tn).transpose(1, 0, 2)
    resident = lambda rows: pl.BlockSpec((nj, rows, tn), lambda i, j: (0, 0, 0), pipeline_mode=pl.Buffered(1))
    return pl.pallas_call(
        functools.partial(_merge_kernel, tm=tm, r=r, pitch=pitch),
        out_shape=_sds((n_tok, d_model), BF16),
        grid=(n_tok // tm, d_model // tn),
        in_specs=[prow, prow, row(hw), pl.BlockSpec((1, hw), lambda i, j: (0, 0)),
                  row(gw), row(gw), res(4), res(4), res(16), res(16),
                  pl.BlockSpec((tm, tn), lambda i, j: (i, ghy_block0 + j)),
                  pl.BlockSpec((tm, tn), lambda i, j: (i, gat_block0 + j)),
                  resident(hw), resident(gw)],
        out_specs=pl.BlockSpec((tm, tn), lambda i, j: (i, j)),
        scratch_shapes=[pltpu.VMEM((tm, hw), BF16), pltpu.VMEM((tm, gw), BF16)]
                       + [pltpu.VMEM((gw // LANES, tm, LANES), F32)] * 4,
        compiler_params=_params(("arbitrary", "arbitrary")),
        name="gated_merge",
    )(yconv, zin, x1c, skip, o0, l0, o1, l1, o2, l2, pn, pn, wh3, wa3)


def _router_kernel(mo_ref, x_ref, gm_ref, gpost_ref, gpre_ref, sc_ref, sh_ref, wr_ref, br_ref,
                   x1_ref, h2_ref, idx_ref, tw_ref, *, ts, d_model, n_experts):
    mo = mo_ref[...]
    y = mo * lax.rsqrt(jnp.mean(mo * mo, axis=-1, keepdims=True) + RMS_EPS) * gpost_ref[...]
    x1 = x_ref[...] + gm_ref[...] * y
    x1_ref[...] = x1
    h2 = (x1 * lax.rsqrt(jnp.mean(x1 * x1, axis=-1, keepdims=True) + RMS_EPS) * gpre_ref[...]
          * (1.0 + sc_ref[...]) + sh_ref[...])
    nct = d_model // 2 // LANES
    sp = nct + SLAB_PAD
    words = _pack_bf16_pairs(h2)
    for c in range(nct):
        h2_ref[pl.ds(c, ts, stride=sp), :] = words[:, c * LANES:(c + 1) * LANES]
    for c in range(nct, sp):
        h2_ref[pl.ds(c, ts, stride=sp), :] = jnp.zeros((ts, LANES), jnp.uint32)
    logits = jnp.dot(h2, wr_ref[...], precision=HIGHEST, preferred_element_type=F32) + br_ref[...]
    lane = lax.broadcasted_iota(jnp.int32, logits.shape, 1)
    lane_f = lane.astype(F32)
    logits = jnp.where(lane < n_experts, logits, -jnp.inf)
    idx_out = jnp.zeros(logits.shape, jnp.int32)
    val_out = jnp.zeros(logits.shape, F32)
    top0 = None
    den = None
    for k in range(TOP_K):
        m = jnp.max(logits, axis=-1, keepdims=True)
        idx = jnp.min(jnp.where(logits == m, lane_f, float(LANES)), axis=-1, keepdims=True).astype(jnp.int32)
        if k == 0:
            top0 = m
        e = jnp.exp(m - top0)
        den = e if den is None else den + e
        idx_out = jnp.where(lane == k, idx, idx_out)
        val_out = jnp.where(lane == k, e, val_out)
        logits = jnp.where(lane == idx, -jnp.inf, logits)
    idx_ref[...] = idx_out
    tw_ref[...] = val_out / den


def _post_mix_and_route(mo, x, gate_m, g_post, g_pre, scale_f, shift_f, w_router, b_router):
    b, s, d = x.shape
    e = w_router.shape[1]
    ts = 256
    wr = jnp.zeros((d, LANES), F32).at[:, :e].set(w_router)
    br = jnp.zeros((1, LANES), F32).at[0, :e].set(b_router)
    sp = d // 2 // LANES + SLAB_PAD
    spb = s // ts
    rowblk = lambda width: pl.BlockSpec((ts, width), lambda i: (i, 0))
    per_batch = pl.BlockSpec((None, 1, d), lambda i: (i // spb, 0, 0))
    vec = pl.BlockSpec((1, d), lambda i: (0, 0))
    n_tok = b * s
    return pl.pallas_call(
        functools.partial(_router_kernel, ts=ts, d_model=d, n_experts=e),
        out_shape=(_sds((n_tok, d), F32), _sds((n_tok * sp, LANES), jnp.uint32),
                   _sds((n_tok, LANES), jnp.int32), _sds((n_tok, LANES), F32)),
        grid=(n_tok // ts,),
        in_specs=[rowblk(d), rowblk(d), per_batch, vec, vec, per_batch, per_batch,
                  pl.BlockSpec((d, LANES), lambda i: (0, 0)), pl.BlockSpec((1, LANES), lambda i: (0, 0))],
        out_specs=(rowblk(d), pl.BlockSpec((ts * sp, LANES), lambda i: (i, 0)), rowblk(LANES), rowblk(LANES)),
        compiler_params=_params(("arbitrary",)),
        name="post_mix_route",
    )(mo, x.reshape(n_tok, d), gate_m, g_post, g_pre, scale_f, shift_f, wr, br)


def _expert_up_kernel(be_ref, nu_ref, tok0_ref, tok1_ref, h2_hbm, wg_ref, bg_ref, wu_ref, bu_ref, act_ref,
                      xbuf, xb, sem, *, tb, nct):
    i = pl.program_id(0)
    n_used = nu_ref[0]
    sp = nct + SLAB_PAD

    def row_copy(t, j, slot):
        return pltpu.make_async_copy(h2_hbm.at[pl.ds(pl.multiple_of(t * sp, 8), nct), :],
                                     xbuf.at[slot, pl.ds(pl.multiple_of(j * sp, 8), nct), :], sem.at[slot])

    def gather(tok_ref, slot):
        def body(j, carry):
            row_copy(tok_ref[0, j], j, slot).start()
            return carry
        lax.fori_loop(0, tb, body, 0, unroll=8)

    def wait(slot):
        pltpu.make_async_copy(h2_hbm.at[pl.ds(0, tb * nct), :], xbuf.at[slot, pl.ds(0, tb * nct), :],
                              sem.at[slot]).wait()

    @pl.when((i == 0) & (n_used > 0))
    def _():
        gather(tok0_ref, 0)

    @pl.when(i + 1 < n_used)
    def _():
        gather(tok1_ref, (i + 1) % 2)

    @pl.when(i < n_used)
    def _():
        slot = i % 2
        wait(slot)
        half = nct * LANES
        for c in range(nct):
            hi, lo = _unpack_bf16_pairs(xbuf[slot, pl.ds(c, tb, stride=sp), :])
            xb[:, c * LANES:(c + 1) * LANES] = hi.astype(BF16)
            xb[:, half + c * LANES:half + (c + 1) * LANES] = lo.astype(BF16)
        x = xb[...]
        g = jnp.dot(x, wg_ref[...], preferred_element_type=F32) + bg_ref[...]
        u = jnp.dot(x, wu_ref[...], preferred_element_type=F32) + bu_ref[...]
        g = jnp.minimum(g, SWIGLU_LIMIT)
        u = jnp.clip(u, -SWIGLU_LIMIT, SWIGLU_LIMIT)
        act_ref[...] = (g * jax.nn.sigmoid(SWIGLU_ALPHA * g) * (u + 1.0)).astype(act_ref.dtype)

    @pl.when(i >= n_used)
    def _():
        act_ref[...] = jnp.zeros(act_ref.shape, act_ref.dtype)


def _expert_up(block_e, n_used, row_tok3, h2s, wg, bg, wu, bu, nct):
    n_blocks, _, tb = row_tok3.shape
    _, d, f = wg.shape
    grid_spec = pltpu.PrefetchScalarGridSpec(
        num_scalar_prefetch=2,
        grid=(n_blocks,),
        in_specs=[pl.BlockSpec((None, 1, tb), lambda i, be, nu: (i, 0, 0), memory_space=pltpu.SMEM),
                  pl.BlockSpec((None, 1, tb), lambda i, be, nu: (jnp.minimum(i + 1, n_blocks - 1), 0, 0),
                               memory_space=pltpu.SMEM),
                  pl.BlockSpec(memory_space=pl.ANY),
                  pl.BlockSpec((None, d, f), lambda i, be, nu: (be[i], 0, 0)),
                  pl.BlockSpec((None, 1, f), lambda i, be, nu: (be[i], 0, 0)),
                  pl.BlockSpec((None, d, f), lambda i, be, nu: (be[i], 0, 0)),
                  pl.BlockSpec((None, 1, f), lambda i, be, nu: (be[i], 0, 0))],
        out_specs=pl.BlockSpec((tb, f), lambda i, be, nu: (i, 0)),
        scratch_shapes=[pltpu.VMEM((2, tb * (nct + SLAB_PAD), LANES), jnp.uint32), pltpu.VMEM((tb, d), BF16),
                        pltpu.SemaphoreType.DMA((2,))],
    )
    return pl.pallas_call(
        functools.partial(_expert_up_kernel, tb=tb, nct=nct),
        out_shape=_sds((n_blocks * tb, f), BF16),
        grid_spec=grid_spec,
        compiler_params=_params(("arbitrary",)),
        name="expert_up",
    )(block_e, n_used, row_tok3, row_tok3, h2s, wg, bg, wu, bu)


def _expert_down_kernel(be_ref, nu_ref, act_ref, wd_ref, bd_ref, ys_ref, wb, *, tb, nct):
    i = pl.program_id(0)

    @pl.when((i < nu_ref[0]) & ((i == 0) | (be_ref[i] != be_ref[jnp.maximum(i - 1, 0)])))
    def _():
        wb[...] = wd_ref[...].astype(BF16)

    @pl.when(i < nu_ref[0])
    def _():
        y = jnp.dot(act_ref[...], wb[...], preferred_element_type=F32) + bd_ref[...]
        words = _pack_bf16_pairs(y)
        sp = nct + SLAB_PAD
        for c in range(nct):
            ys_ref[pl.ds(c, tb, stride=sp), :] = words[:, c * LANES:(c + 1) * LANES]
        for c in range(nct, sp):
            ys_ref[pl.ds(c, tb, stride=sp), :] = jnp.zeros((tb, LANES), jnp.uint32)

    @pl.when(i >= nu_ref[0])
    def _():
        ys_ref[...] = jnp.zeros(ys_ref.shape, ys_ref.dtype)


def _expert_down(block_e, n_used, act, wd, bd, tb):
    _, f, d = wd.shape
    n_blocks = act.shape[0] // tb
    nct = d // 2 // LANES
    grid_spec = pltpu.PrefetchScalarGridSpec(
        num_scalar_prefetch=2,
        grid=(n_blocks,),
        in_specs=[pl.BlockSpec((tb, f), lambda i, be, nu: (i, 0)),
                  pl.BlockSpec((None, f, d), lambda i, be, nu: (be[i], 0, 0)),
                  pl.BlockSpec((None, 1, d), lambda i, be, nu: (be[i], 0, 0))],
        out_specs=pl.BlockSpec((tb * (nct + SLAB_PAD), LANES), lambda i, be, nu: (i, 0)),
        scratch_shapes=[pltpu.VMEM((f, d), BF16)],
    )
    return pl.pallas_call(
        functools.partial(_expert_down_kernel, tb=tb, nct=nct),
        out_shape=_sds((n_blocks * tb * (nct + SLAB_PAD), LANES), jnp.uint32),
        grid_spec=grid_spec,
        compiler_params=_params(("arbitrary",)),
        name="expert_down",
    )(block_e, n_used, act, wd, bd)


def _combine_kernel(d0_ref, d1_ref, ys_hbm, tw_ref, x1_ref, gf_ref, gpost_ref, o_ref, buf, ff, sem, *, tc, nct):
    i = pl.program_id(0)
    last = pl.num_programs(0) - 1
    n_rows = TOP_K * tc
    sp = nct + SLAB_PAD

    def row_copy(r, j, slot):
        return pltpu.make_async_copy(ys_hbm.at[pl.ds(pl.multiple_of(r * sp, 8), nct), :],
                                     buf.at[slot, pl.ds(pl.multiple_of(j * sp, 8), nct), :], sem.at[slot])

    def gather(dref, slot):
        def body(j, carry):
            row_copy(dref[0, j], j, slot).start()
            return carry
        lax.fori_loop(0, n_rows, body, 0, unroll=8)

    def wait(slot):
        pltpu.make_async_copy(ys_hbm.at[pl.ds(0, n_rows * nct), :], buf.at[slot, pl.ds(0, n_rows * nct), :],
                              sem.at[slot]).wait()

    @pl.when(i == 0)
    def _():
        gather(d0_ref, 0)

    @pl.when(i < last)
    def _():
        gather(d1_ref, (i + 1) % 2)

    slot = i % 2
    wait(slot)
    half = nct * LANES
    wk = [tw_ref[:, k:k + 1] for k in range(TOP_K)]
    for c in range(nct):
        acc_hi = acc_lo = None
        for k in range(TOP_K):
            hi, lo = _unpack_bf16_pairs(buf[slot, pl.ds(k * tc * sp + c, tc, stride=sp), :])
            acc_hi = wk[k] * hi if acc_hi is None else acc_hi + wk[k] * hi
            acc_lo = wk[k] * lo if acc_lo is None else acc_lo + wk[k] * lo
        ff[:, c * LANES:(c + 1) * LANES] = acc_hi
        ff[:, half + c * LANES:half + (c + 1) * LANES] = acc_lo
    f = ff[...]
    y = f * lax.rsqrt(jnp.mean(f * f, axis=-1, keepdims=True) + RMS_EPS) * gpost_ref[...]
    o_ref[...] = x1_ref[...] + gf_ref[...] * y


def _combine(dest3, ys, top_w, x1, gate_f, g_post, seq):
    n_tok, d = x1.shape
    n_steps, _, n_rows = dest3.shape
    tc = n_rows // TOP_K
    nct = d // 2 // LANES
    spb = seq // tc
    return pl.pallas_call(
        functools.partial(_combine_kernel, tc=tc, nct=nct),
        out_shape=_sds((n_tok, d), F32),
        grid=(n_steps,),
        in_specs=[pl.BlockSpec((None, 1, n_rows), lambda i: (i, 0, 0), memory_space=pltpu.SMEM),
                  pl.BlockSpec((None, 1, n_rows), lambda i: (jnp.minimum(i + 1, n_steps - 1), 0, 0),
                               memory_space=pltpu.SMEM),
                  pl.BlockSpec(memory_space=pl.ANY),
                  pl.BlockSpec((tc, LANES), lambda i: (i, 0)),
                  pl.BlockSpec((tc, d), lambda i: (i, 0)),
                  pl.BlockSpec((None, 1, d), lambda i: (i // spb, 0, 0)),
                  pl.BlockSpec((1, d), lambda i: (0, 0))],
        out_specs=pl.BlockSpec((tc, d), lambda i: (i, 0)),
        scratch_shapes=[pltpu.VMEM((2, n_rows * (nct + SLAB_PAD), LANES), jnp.uint32), pltpu.VMEM((tc, d), F32),
                        pltpu.SemaphoreType.DMA((2,))],
        compiler_params=_params(("arbitrary",)),
        name="expert_combine",
    )(dest3, dest3, ys, top_w, x1, gate_f, g_post)


def _rank_kernel(idx_ref, tri_ref, rank_ref, cnt_ref, carry, *, tr):
    @pl.when(pl.program_id(0) == 0)
    def _():
        carry[...] = jnp.zeros(carry.shape, F32)

    idx = idx_ref[...]
    lane = lax.broadcasted_iota(jnp.int32, idx.shape, 1)
    base = carry[0:1, :]
    out = jnp.zeros(idx.shape, jnp.int32)
    for k in range(TOP_K):
        onehot = jnp.where(lane == idx[:, k:k + 1], 1.0, 0.0)
        csum = jnp.dot(tri_ref[...], onehot.astype(BF16), preferred_element_type=F32)
        rank = jnp.sum(onehot * (csum + base), axis=-1, keepdims=True) - 1.0
        out = jnp.where(lane == k, rank.astype(jnp.int32), out)
        base = base + csum[tr - 1:tr, :]
    rank_ref[...] = out
    carry[...] = jnp.broadcast_to(base, carry.shape)
    cnt_ref[...] = jnp.broadcast_to(base, cnt_ref.shape).astype(jnp.int32)


def _expert_ranks(top_idx_padded):
    n_tok = top_idx_padded.shape[0]
    tr = 512
    tri = jnp.asarray(np.tril(np.ones((tr, tr), np.float32)), BF16)
    return pl.pallas_call(
        functools.partial(_rank_kernel, tr=tr),
        out_shape=(_sds((n_tok, LANES), jnp.int32), _sds((8, LANES), jnp.int32)),
        grid=(n_tok // tr,),
        in_specs=[pl.BlockSpec((tr, LANES), lambda i: (i, 0)), pl.BlockSpec((tr, tr), lambda i: (0, 0))],
        out_specs=(pl.BlockSpec((tr, LANES), lambda i: (i, 0)), pl.BlockSpec((8, LANES), lambda i: (0, 0))),
        scratch_shapes=[pltpu.VMEM((8, LANES), F32)],
        compiler_params=_params(("arbitrary",)),
        name="expert_ranks",
    )(top_idx_padded, tri)


def _routing_tables(top_idx_padded, n_experts, tb):
    n_tok = top_idx_padded.shape[0]
    n_assign = n_tok * TOP_K
    ranks, counts = _expert_ranks(top_idx_padded)
    sizes = counts[0, :n_experts]
    padded = (sizes + tb - 1) // tb * tb
    pad_end = jnp.cumsum(padded)
    pad_start = pad_end - padded
    top_idx = top_idx_padded[:, :TOP_K]
    experts = jnp.arange(n_experts, dtype=jnp.int32)
    start_of = jnp.sum(jnp.where(top_idx[:, :, None] == experts, pad_start, 0), axis=-1)
    dest = (start_of + ranks[:, :TOP_K]).astype(jnp.int32).reshape(-1)
    n_rows = -(-n_assign // tb) * tb + n_experts * tb
    n_blocks = n_rows // tb
    tok = (jnp.arange(n_assign, dtype=jnp.int32) // TOP_K)
    row_tok = jnp.zeros((n_rows,), jnp.int32).at[dest].set(tok)
    block_e = jnp.minimum(
        jnp.searchsorted(pad_end, jnp.arange(n_blocks, dtype=jnp.int32) * tb, side="right"),
        n_experts - 1).astype(jnp.int32)
    n_used = (pad_end[-1] // tb).astype(jnp.int32).reshape(1)
    return dest, row_tok, block_e, n_used, n_rows


def _moe(h2s, top_idx, top_w, wg, bg, wu, bu, wd, bd, x1, gate_f, g_post, seq):
    n_tok, d = x1.shape
    n_experts = wg.shape[0]
    tb = MOE_ROWS
    nct = d // 2 // LANES
    dest, row_tok, block_e, n_used, n_rows = _routing_tables(top_idx, n_experts, tb)
    n_blocks = n_rows // tb
    act = _expert_up(block_e, n_used, row_tok.reshape(n_blocks, 1, tb), h2s, wg, bg[:, None, :], wu,
                     bu[:, None, :], nct)
    ys = _expert_down(block_e, n_used, act, wd, bd[:, None, :], tb)
    tc = 128
    dest3 = dest.reshape(n_tok // tc, tc, TOP_K).transpose(0, 2, 1).reshape(n_tok // tc, 1, TOP_K * tc)
    return _combine(dest3, ys, top_w, x1, gate_f, g_post, seq)


def _layer(x, c8, p):
    b, s, d = x.shape
    n_tok = b * s
    hw = p["hy_skip"].shape[0]
    gw = GROUP_WIDTH
    assert hw == COL_BLOCK and gw == COL_BLOCK and d % COL_BLOCK == 0

    mod = _adaln(c8, p["w_ada"], p["b_ada"][None])[:b]
    shift_m, scale_m, gate_m, shift_f, scale_f, gate_f = [m[:, None, :] for m in jnp.split(mod, 6, axis=-1)]

    h, h4, h16 = _prenorm_mix(x, p["g_pre_mix"][None], scale_m, shift_m)

    w_in = p["w_in"].astype(BF16)
    nd = d // COL_BLOCK
    nat_blocks = [0, 1, 2, 3, 6, 9] + list(range(12, 12 + 2 * nd))
    pn = _matmul(h.reshape(n_tok, d), w_in, nat_blocks, BF16, "in_proj_natural")
    qkv1 = _matmul(h4.reshape(n_tok, d), w_in, [4, 7, 10], BF16, "in_proj_dil4")
    qkv2 = _matmul(h16.reshape(n_tok, d), w_in, [5, 8, 11], BF16, "in_proj_dil16")

    r, pitch = _fft_radix(s)
    hfb = _hyena_filters(s, r, pitch, p["hy_f_w1"], p["hy_f_b1"], p["hy_f_w2"], p["hy_f_b2"], p["hy_f_w3"],
                         p["hy_f_b3"], p["hy_f_freq"], p["hy_f_wout"])
    zin, x1c = _shortconv(pn.reshape(b, s, -1), p["hy_conv_w"], p["hy_conv_b"], hw, r, pitch)
    yconv = _long_conv(zin, hfb, r, pitch)

    slopes = _alibi_slopes(N_GROUPS * HEADS_PER_GROUP).reshape(N_GROUPS, HEADS_PER_GROUP)
    o0, l0 = _dilated_attention(pn.reshape(b, s, -1), s, 1, slopes[0], col0=3)
    o1, l1 = _dilated_attention(qkv1.reshape(b * 4, s // 4, 3 * gw), s // 4, 4, slopes[1])
    o2, l2 = _dilated_attention(qkv2.reshape(b * 16, s // 16, 3 * gw), s // 16, 16, slopes[2])

    merged = _merge(yconv.reshape(-1, hw), zin.reshape(-1, hw), x1c.reshape(n_tok, hw), p["hy_skip"][None],
                    o0.reshape(n_tok, gw), l0.reshape(n_tok, gw),
                    o1.reshape(b, 4, s // 4, gw), l1.reshape(b, 4, s // 4, gw),
                    o2.reshape(b, 16, s // 16, gw), l2.reshape(b, 16, s // 16, gw),
                    pn, 6, 6 + nd, p["w_proj_hyena"].astype(BF16), p["w_proj_attn"].astype(BF16), b, s, r, pitch)
    mo = _matmul(merged, p["w_out"].astype(BF16), list(range(nd)), F32, "out_proj")

    x1, h2s, top_idx, top_w = _post_mix_and_route(mo, x, gate_m, p["g_post_mix"][None], p["g_pre_ffn"][None],
                                                  scale_f, shift_f, p["w_router"], p["b_router"])
    out = _moe(h2s, top_idx, top_w, p["w_gate"].astype(BF16), p["b_gate"],
               p["w_up"].astype(BF16), p["b_up"], p["w_down"], p["b_down"], x1, gate_f,
               p["g_post_ffn"][None], s)
    return out.reshape(b, s, d)


def kernel(x, c, w_ada, b_ada, g_pre_mix, g_post_mix, g_pre_ffn, g_post_ffn, w_in, hy_conv_w, hy_conv_b, hy_skip, hy_f_w1, hy_f_b1, hy_f_w2, hy_f_b2, hy_f_w3, hy_f_b3, hy_f_freq, hy_f_wout, w_proj_hyena, w_proj_attn, w_out, w_router, b_router, w_gate, b_gate, w_up, b_up, w_down, b_down):
    names = ("w_ada", "b_ada", "g_pre_mix", "g_post_mix", "g_pre_ffn", "g_post_ffn", "w_in", "hy_conv_w",
             "hy_conv_b", "hy_skip", "hy_f_w1", "hy_f_b1", "hy_f_w2", "hy_f_b2", "hy_f_w3", "hy_f_b3",
             "hy_f_freq", "hy_f_wout", "w_proj_hyena", "w_proj_attn", "w_out", "w_router", "b_router",
             "w_gate", "b_gate", "w_up", "b_up", "w_down", "b_down")
    stacked = (w_ada, b_ada, g_pre_mix, g_post_mix, g_pre_ffn, g_post_ffn, w_in, hy_conv_w, hy_conv_b, hy_skip,
               hy_f_w1, hy_f_b1, hy_f_w2, hy_f_b2, hy_f_w3, hy_f_b3, hy_f_freq, hy_f_wout, w_proj_hyena,
               w_proj_attn, w_out, w_router, b_router, w_gate, b_gate, w_up, b_up, w_down, b_down)
    depth = w_ada.shape[0]
    b = x.shape[0]
    c8 = jnp.zeros((8, c.shape[1]), F32).at[:b].set(c)
    for l in range(depth):
        x = _layer(x, c8, {k: v[l] for k, v in zip(names, stacked)})
    return x
```

```python
import functools
import math

import jax
import jax.numpy as jnp
import numpy as np
from jax import lax
from jax.experimental import pallas as pl
from jax.experimental.pallas import tpu as pltpu

F32 = jnp.float32
BF16 = jnp.bfloat16
HIGHEST = lax.Precision.HIGHEST

LANES = 128
HEAD_DIM = 128
HEADS_PER_GROUP = 8
DILATED_GROUPS = ((128, 1), (512, 4), (2048, 16))
N_GROUPS = len(DILATED_GROUPS)
GROUP_WIDTH = HEADS_PER_GROUP * HEAD_DIM
ATTN_SIDE = 64
TOP_K = 4
SWIGLU_LIMIT = 7.0
SWIGLU_ALPHA = 1.702
RMS_EPS = 1e-6
NEG_INF = -1e30
HYENA_N_BANDS = 16
HYENA_DECAY_TARGET = 1e-2
HYENA_FAST_DECAY_PCT = 0.3
HYENA_SLOW_DECAY_PCT = 1.5
COL_BLOCK = 1024
MOE_ROWS = 256
SLAB_PAD = 8
VMEM_LIMIT = 56 * 1024 * 1024


def _pack_bf16_pairs(x):
    bits = lax.bitcast_convert_type(x.astype(BF16).astype(F32), jnp.uint32)
    groups = [bits[:, j:j + LANES] | (bits[:, j + LANES:j + 2 * LANES] >> 16)
              for j in range(0, x.shape[1], 2 * LANES)]
    return groups[0] if len(groups) == 1 else jnp.concatenate(groups, axis=1)


def _unpack_bf16_pairs(w):
    hi = lax.bitcast_convert_type(w & jnp.uint32(0xFFFF0000), F32)
    lo = lax.bitcast_convert_type(w << 16, F32)
    return hi, lo


def _params(sem, vmem=VMEM_LIMIT):
    return pltpu.CompilerParams(dimension_semantics=sem, vmem_limit_bytes=vmem)


def _sds(shape, dtype):
    return jax.ShapeDtypeStruct(shape, dtype)


def _ada_kernel(c_ref, w_ref, b_ref, o_ref):
    c = c_ref[...]
    sc = (c * jax.nn.sigmoid(c)).astype(BF16)
    o_ref[...] = jnp.dot(sc, w_ref[...].astype(BF16), preferred_element_type=F32) + b_ref[...]


def _adaln(c8, w_ada, b_ada):
    d, cols = w_ada.shape
    tn = 512
    return pl.pallas_call(
        _ada_kernel,
        out_shape=_sds((8, cols), F32),
        grid=(cols // tn,),
        in_specs=[pl.BlockSpec((8, d), lambda j: (0, 0)),
                  pl.BlockSpec((d, tn), lambda j: (0, j)),
                  pl.BlockSpec((1, tn), lambda j: (0, j))],
        out_specs=pl.BlockSpec((8, tn), lambda j: (0, j)),
        compiler_params=_params(("arbitrary",)),
        name="adaln",
    )(c8, w_ada, b_ada)


def _prenorm_kernel(x_ref, g_ref, sc_ref, sh_ref, o_ref, o4_ref, o16_ref, scr_ref, *, ts, d_model):
    x = x_ref[...]
    ms = jnp.mean(x * x, axis=-1, keepdims=True)
    h = x * lax.rsqrt(ms + RMS_EPS) * g_ref[...] * (1.0 + sc_ref[...]) + sh_ref[...]
    o_ref[...] = h.astype(BF16)
    nct = d_model // LANES
    for c in range(nct):
        scr_ref[c] = h[:, c * LANES:(c + 1) * LANES]
    for dil, oref in ((4, o4_ref), (16, o16_ref)):
        for r in range(dil):
            for c in range(nct):
                oref[r, :, c * LANES:(c + 1) * LANES] = scr_ref[c, pl.ds(r, ts // dil, stride=dil), :].astype(BF16)


def _prenorm_mix(x, g, scale, shift):
    b, s, d = x.shape
    ts = 256
    kern = functools.partial(_prenorm_kernel, ts=ts, d_model=d)
    return pl.pallas_call(
        kern,
        out_shape=(_sds((b, s, d), BF16), _sds((b, 4, s // 4, d), BF16), _sds((b, 16, s // 16, d), BF16)),
        grid=(b, s // ts),
        in_specs=[pl.BlockSpec((None, ts, d), lambda bi, i: (bi, i, 0)),
                  pl.BlockSpec((1, d), lambda bi, i: (0, 0)),
                  pl.BlockSpec((None, 1, d), lambda bi, i: (bi, 0, 0)),
                  pl.BlockSpec((None, 1, d), lambda bi, i: (bi, 0, 0))],
        out_specs=(pl.BlockSpec((None, ts, d), lambda bi, i: (bi, i, 0)),
                   pl.BlockSpec((None, 4, ts // 4, d), lambda bi, i: (bi, 0, i, 0)),
                   pl.BlockSpec((None, 16, ts // 16, d), lambda bi, i: (bi, 0, i, 0))),
        scratch_shapes=[pltpu.VMEM((d // LANES, ts, LANES), F32)],
        compiler_params=_params(("arbitrary", "arbitrary")),
        name="prenorm_mix",
    )(x, g, scale, shift)


def _mm_kernel(tbl_ref, a_ref, w_ref, o_ref):
    del tbl_ref
    o_ref[...] = jnp.dot(a_ref[...], w_ref[...], preferred_element_type=F32).astype(o_ref.dtype)


def _matmul(a, w, col_blocks, out_dtype, name, tm=1024, tn=COL_BLOCK):
    m, k = a.shape
    tm = min(tm, m)
    nb = len(col_blocks)
    tbl = jnp.asarray(col_blocks, jnp.int32)
    grid_spec = pltpu.PrefetchScalarGridSpec(
        num_scalar_prefetch=1,
        grid=(m // tm, nb),
        in_specs=[pl.BlockSpec((tm, k), lambda i, j, t: (i, 0)),
                  pl.BlockSpec((k, tn), lambda i, j, t: (0, t[j]))],
        out_specs=pl.BlockSpec((tm, tn), lambda i, j, t: (i, j)),
    )
    return pl.pallas_call(
        _mm_kernel,
        out_shape=_sds((m, nb * tn), out_dtype),
        grid_spec=grid_spec,
        compiler_params=_params(("arbitrary", "arbitrary")),
        name=name,
    )(tbl, a, w)


def _filter_kernel(z_ref, w1_ref, b1_ref, w2_ref, b2_ref, w3_ref, b3_ref, fr_ref, wo_ref, dl_ref,
                   h_ref, *, hw, r, pitch):
    def dot(a, b):
        return jnp.dot(a, b, precision=HIGHEST, preferred_element_type=F32)

    z = z_ref[...]
    fr = fr_ref[...]
    h = jnp.sin(fr * (dot(z, w1_ref[...]) + b1_ref[...]))
    h = jnp.sin(fr * (dot(h, w2_ref[...]) + b2_ref[...]))
    h = jnp.sin(fr * (dot(h, w3_ref[...]) + b3_ref[...]))
    filt = dot(h, wo_ref[...])
    decay = jnp.exp(-z[:, 0:1] * dl_ref[...])
    hf = filt[:, :hw] * decay
    hb = filt[:, hw:] * decay
    pad = jnp.zeros((pitch - r, hw), F32)
    for g in range(z.shape[0] // r):
        h_ref[0, g * pitch:g * pitch + r, :] = hf[g * r:(g + 1) * r]
        h_ref[1, g * pitch:g * pitch + r, :] = hb[g * r:(g + 1) * r]
        h_ref[0, g * pitch + r:(g + 1) * pitch, :] = pad
        h_ref[1, g * pitch + r:(g + 1) * pitch, :] = pad


def _hyena_filters(length, r, pitch, w1, b1, w2, b2, w3, b3, freq, wout):
    emb, fw = w1.shape
    hw = wout.shape[1] // 2
    t = np.linspace(0.0, 1.0, length)[:, None]
    bands = np.linspace(1e-4, HYENA_N_BANDS - 1, HYENA_N_BANDS)[None, :]
    ang = (2.0 * math.pi / length) * np.arange(length)[:, None] * bands
    z = np.concatenate([t, np.cos(ang), -np.sin(ang)], axis=-1)
    zpad = np.zeros((length, LANES), np.float32)
    zpad[:, :emb] = z
    w1p = jnp.zeros((LANES, fw), F32).at[:emb].set(w1)
    min_decay = math.log(HYENA_DECAY_TARGET) / HYENA_FAST_DECAY_PCT
    max_decay = math.log(HYENA_DECAY_TARGET) / HYENA_SLOW_DECAY_PCT
    deltas = np.abs(np.linspace(min_decay, max_decay, hw))[None, :].astype(np.float32)
    tl = min(1024, length)
    full = lambda shape: pl.BlockSpec(shape, lambda i: (0,) * len(shape))
    return pl.pallas_call(
        functools.partial(_filter_kernel, hw=hw, r=r, pitch=pitch),
        out_shape=_sds((2, length // r * pitch, hw), F32),
        grid=(length // tl,),
        in_specs=[pl.BlockSpec((tl, LANES), lambda i: (i, 0)),
                  full((LANES, fw)), full((1, fw)), full((fw, fw)), full((1, fw)),
                  full((fw, fw)), full((1, fw)), full((1, fw)), full((fw, 2 * hw)), full((1, hw))],
        out_specs=pl.BlockSpec((2, tl // r * pitch, hw), lambda i: (0, i, 0)),
        compiler_params=_params(("arbitrary",)),
        name="hyena_filters",
    )(jnp.asarray(zpad), w1p, b1[None], w2, b2[None], w3, b3[None], freq[None], wout, jnp.asarray(deltas))


def _shortconv_kernel(u_ref, up_ref, un_ref, w_ref, b_ref, z_ref, x1_ref, *, ts, hw, r, pitch):
    i = pl.program_id(1)
    last = pl.num_programs(1) - 1
    u = u_ref[...].astype(F32)
    prev_blk = up_ref[...].astype(F32)
    next_blk = un_ref[...].astype(F32)
    prev_row = jnp.where(i > 0, prev_blk[15:16, :], 0.0)
    next_row = jnp.where(i < last, next_blk[0:1, :], 0.0)
    row = lax.broadcasted_iota(jnp.int32, u.shape, 0)
    um = jnp.where(row == 0, prev_row, pltpu.roll(u, 1, 0))
    up = jnp.where(row == ts - 1, next_row, pltpu.roll(u, ts - 1, 0))
    w = w_ref[...]
    uc = w[0:1] * um + w[1:2] * u + w[2:3] * up + b_ref[...]
    x1_ref[...] = uc[:, :hw]
    z = uc[:, 2 * hw:] * uc[:, hw:2 * hw]
    pad = jnp.zeros((pitch - r, hw), F32)
    for g in range(ts // r):
        z_ref[g * pitch:g * pitch + r, :] = z[g * r:(g + 1) * r]
        z_ref[g * pitch + r:(g + 1) * pitch, :] = pad


def _shortconv(pn3, conv_w, conv_b, hw, r, pitch):
    b, s, _ = pn3.shape
    ts = 512
    w3 = 3 * hw
    nh = s // 16
    return pl.pallas_call(
        functools.partial(_shortconv_kernel, ts=ts, hw=hw, r=r, pitch=pitch),
        out_shape=(_sds((b, s // r * pitch, hw), F32), _sds((b, s, hw), F32)),
        grid=(b, s // ts),
        in_specs=[pl.BlockSpec((None, ts, w3), lambda bi, i: (bi, i, 0)),
                  pl.BlockSpec((None, 16, w3), lambda bi, i: (bi, jnp.maximum(i * (ts // 16) - 1, 0), 0)),
                  pl.BlockSpec((None, 16, w3), lambda bi, i: (bi, jnp.minimum((i + 1) * (ts // 16), nh - 1), 0)),
                  pl.BlockSpec((3, w3), lambda bi, i: (0, 0)),
                  pl.BlockSpec((1, w3), lambda bi, i: (0, 0))],
        out_specs=(pl.BlockSpec((None, ts // r * pitch, hw), lambda bi, i: (bi, i, 0)),
                   pl.BlockSpec((None, ts, hw), lambda bi, i: (bi, i, 0))),
        compiler_params=_params(("arbitrary", "arbitrary")),
        name="hyena_shortconv",
    )(pn3, pn3, pn3, conv_w, conv_b[None])


FFT_GROUP = 4


def _split_bf16(a):
    hi = a.astype(BF16)
    return hi, (a - hi.astype(F32)).astype(BF16)


def _dot3(fh, fl, x):
    xh, xl = _split_bf16(x)
    d = lambda a, b: jnp.dot(a, b, preferred_element_type=F32)
    return d(fh, xh) + (d(fh, xl) + d(fl, xh))


def _fft_kept(r):
    return (r // 2 + 1 + 7) // 8 * 8


def _dft_tables(r):
    idx = np.arange(r)
    ang = 2.0 * np.pi * np.outer(idx, idx) / r
    cos, sin = np.cos(ang), np.sin(ang)
    kp = _fft_kept(r)
    fa_half = np.concatenate([cos[:kp, :r // 2], -sin[:kp, :r // 2]], axis=0)
    fbig = np.block([[cos, sin], [-sin, cos]])
    fconj = np.block([[cos, -sin], [sin, cos]])
    wgt = np.zeros(kp)
    wgt[0] = wgt[r // 2] = 1.0
    wgt[1:r // 2] = 2.0
    gfin = np.concatenate([cos[:r // 2, :kp] * wgt, -sin[:r // 2, :kp] * wgt], axis=1)
    tang = 2.0 * np.pi * np.outer(idx, idx) / (r * r)
    tw = np.stack([np.cos(tang), -np.sin(tang)], axis=-1)
    split = lambda a: _split_bf16(jnp.asarray(a.astype(np.float32)))
    return split(fa_half), split(fbig), split(fconj), split(gfin), jnp.asarray(tw.astype(np.float32))


def _fft_a_kernel(x_ref, fh_ref, fl_ref, o_ref, *, r, k1, kp, pitch):
    fh, fl = fh_ref[...], fl_ref[...]
    pad = jnp.zeros((pitch - r, LANES), F32)
    for g in range(kp):
        o_ref[0, g * pitch + r:(g + 1) * pitch, :] = pad
        o_ref[1, g * pitch + r:(g + 1) * pitch, :] = pad

    def body(g, carry):
        n2 = g * FFT_GROUP
        xs = jnp.concatenate([x_ref[pl.ds(n2 + k, k1, stride=pitch), :] for k in range(FFT_GROUP)], axis=1)
        a = _dot3(fh, fl, xs)
        for k in range(FFT_GROUP):
            o_ref[0, pl.ds(n2 + k, kp, stride=pitch), :] = a[:kp, k * LANES:(k + 1) * LANES]
            o_ref[1, pl.ds(n2 + k, kp, stride=pitch), :] = a[kp:, k * LANES:(k + 1) * LANES]
        return carry

    lax.fori_loop(0, r // FFT_GROUP, body, 0)


def _fft_stage_a(x, fmat, r, pitch):
    bx, rows, c = x.shape
    k1 = rows // pitch
    kp = _fft_kept(r)
    return pl.pallas_call(
        functools.partial(_fft_a_kernel, r=r, k1=k1, kp=kp, pitch=pitch),
        out_shape=_sds((bx, 2, kp * pitch, c), F32),
        grid=(bx, c // LANES),
        in_specs=[pl.BlockSpec((None, rows, LANES), lambda b, ci: (b, 0, ci)),
                  pl.BlockSpec((2 * kp, k1), lambda b, ci: (0, 0)),
                  pl.BlockSpec((2 * kp, k1), lambda b, ci: (0, 0))],
        out_specs=pl.BlockSpec((None, 2, kp * pitch, LANES), lambda b, ci: (b, 0, 0, ci)),
        compiler_params=_params(("arbitrary", "arbitrary")),
        name="fft_stage_a",
    )(x, *fmat)


def _twiddled(a_ref, b, tr, ti, r):
    are, aim = a_ref[b, 0, :r, :], a_ref[b, 1, :r, :]
    return jnp.concatenate([are * tr - aim * ti, are * ti + aim * tr], axis=0)


def _fft_mk_kernel(a_ref, tw_ref, h0_ref, fbh_ref, fbl_ref, o_ref, *, r, scale):
    tr, ti = tw_ref[:, 0:1], tw_ref[:, 1:2]
    fbh, fbl = fbh_ref[...], fbl_ref[...]
    xf = _dot3(fbh, fbl, _twiddled(a_ref, 0, tr, ti, r))
    xb = _dot3(fbh, fbl, _twiddled(a_ref, 1, tr, ti, r))
    o_ref[0] = (xf[:r] + xb[:r] - h0_ref[...]) * scale
    o_ref[1] = (xf[r:] - xb[r:]) * scale


def _fft_filter_spectrum(a, tw, h0, fbig, r, pitch):
    c = a.shape[-1]
    kp = _fft_kept(r)
    return pl.pallas_call(
        functools.partial(_fft_mk_kernel, r=r, scale=1.0 / (r * r)),
        out_shape=_sds((2, kp * r, c), F32),
        grid=(kp,),
        in_specs=[pl.BlockSpec((2, 2, pitch, c), lambda k: (0, 0, k, 0)),
                  pl.BlockSpec((None, r, 2), lambda k: (k, 0, 0)),
                  pl.BlockSpec((1, c), lambda k: (0, 0)),
                  pl.BlockSpec((2 * r, 2 * r), lambda k: (0, 0)),
                  pl.BlockSpec((2 * r, 2 * r), lambda k: (0, 0))],
        out_specs=pl.BlockSpec((2, r, c), lambda k: (0, k, 0)),
        compiler_params=_params(("arbitrary",)),
        name="fft_filter_spectrum",
    )(a, tw, h0, *fbig)


def _fft_m_kernel(a_ref, ks_ref, tw_ref, fbh_ref, fbl_ref, fch_ref, fcl_ref, o_ref, *, r):
    tr, ti = tw_ref[:, 0:1], tw_ref[:, 1:2]
    x = _dot3(fbh_ref[...], fbl_ref[...], _twiddled(a_ref, 0, tr, ti, r))
    xre, xim = x[:r], x[r:]
    kre, kim = ks_ref[0], ks_ref[1]
    c = _dot3(fch_ref[...], fcl_ref[...],
              jnp.concatenate([xre * kre - xim * kim, xre * kim + xim * kre], axis=0))
    cre, cim = c[:r], c[r:]
    o_ref[0, :r, :] = cre * tr + cim * ti
    o_ref[1, :r, :] = cim * tr - cre * ti
    o_ref[:, r:, :] = jnp.zeros((2,) + (o_ref.shape[1] - r, o_ref.shape[2]), F32)


def _fft_stage_m(a, ks, tw, fbig, fconj, r, pitch):
    b, _, rows, c = a.shape
    a5 = a.reshape(b, 1, 2, rows, c)
    mat = pl.BlockSpec((2 * r, 2 * r), lambda k, bi: (0, 0))
    return pl.pallas_call(
        functools.partial(_fft_m_kernel, r=r),
        out_shape=_sds(a.shape, F32),
        grid=(rows // pitch, b),
        in_specs=[pl.BlockSpec((None, 1, 2, pitch, c), lambda k, bi: (bi, 0, 0, k, 0)),
                  pl.BlockSpec((2, r, c), lambda k, bi: (0, k, 0)),
                  pl.BlockSpec((None, r, 2), lambda k, bi: (k, 0, 0)),
                  mat, mat, mat, mat],
        out_specs=pl.BlockSpec((None, 2, pitch, c), lambda k, bi: (bi, 0, k, 0)),
        compiler_params=_params(("arbitrary", "arbitrary")),
        name="fft_stage_m",
    )(a5, ks, tw, *fbig, *fconj)


def _fft_f_kernel(d_ref, gh_ref, gl_ref, o_ref, *, r, kp, pitch):
    gh, gl = gh_ref[...], gl_ref[...]
    pad = jnp.zeros((pitch - r, LANES), F32)
    for g in range(r // 2):
        o_ref[g * pitch + r:(g + 1) * pitch, :] = pad

    def body(g, carry):
        n2 = g * FFT_GROUP
        dcat = jnp.concatenate(
            [jnp.concatenate([d_ref[0, pl.ds(n2 + k, kp, stride=pitch), :],
                              d_ref[1, pl.ds(n2 + k, kp, stride=pitch), :]], axis=0) for k in range(FFT_GROUP)],
            axis=1)
        y = _dot3(gh, gl, dcat)
        for k in range(FFT_GROUP):
            o_ref[pl.ds(n2 + k, r // 2, stride=pitch), :] = y[:, k * LANES:(k + 1) * LANES]
        return carry

    lax.fori_loop(0, r // FFT_GROUP, body, 0)


def _fft_stage_f(dmat, gfin, r, pitch):
    b, _, rows, c = dmat.shape
    kp = rows // pitch
    out_rows = r // 2 * pitch
    return pl.pallas_call(
        functools.partial(_fft_f_kernel, r=r, kp=kp, pitch=pitch),
        out_shape=_sds((b, out_rows, c), F32),
        grid=(b, c // LANES),
        in_specs=[pl.BlockSpec((None, 2, rows, LANES), lambda bi, ci: (bi, 0, 0, ci)),
                  pl.BlockSpec((r // 2, 2 * kp), lambda bi, ci: (0, 0)),
                  pl.BlockSpec((r // 2, 2 * kp), lambda bi, ci: (0, 0))],
        out_specs=pl.BlockSpec((None, out_rows, LANES), lambda bi, ci: (bi, 0, ci)),
        compiler_params=_params(("arbitrary", "arbitrary")),
        name="fft_stage_f",
    )(dmat, *gfin)


def _fft_radix(length):
    r = int(round(math.sqrt(2 * length)))
    assert r * r == 2 * length, "sequence length must make 2L a perfect square"
    return r, r + 8


def _long_conv(zin, hfb, r, pitch):
    fa_half, fbig, fconj, gfin, tw = _dft_tables(r)
    ks = _fft_filter_spectrum(_fft_stage_a(hfb, fa_half, r, pitch), tw, hfb[1, 0:1, :], fbig, r, pitch)
    a = _fft_stage_a(zin, fa_half, r, pitch)
    dmat = _fft_stage_m(a, ks, tw, fbig, fconj, r, pitch)
    return _fft_stage_f(dmat, gfin, r, pitch)


def _alibi_slopes(n_heads):
    def pow2_slopes(m):
        start = 2.0 ** (-8.0 / m)
        return [start ** (i + 1) for i in range(m)]
    base = 2 ** int(math.floor(math.log2(n_heads)))
    slopes = pow2_slopes(base)
    if base < n_heads:
        slopes = slopes + pow2_slopes(2 * base)[0::2][: n_heads - base]
    return np.array(sorted(slopes, reverse=True), dtype=np.float32)


def _attn_kernel(q_ref, k_ref, kp_ref, kn_ref, v_ref, vp_ref, vn_ref, o_ref, l_ref, *, tq, n, dil, slopes):
    i = pl.program_id(1)
    side = ATTN_SIDE
    nk = tq + 2 * side
    row = lax.broadcasted_iota(jnp.int32, (tq, nk), 0)
    col = lax.broadcasted_iota(jnp.int32, (tq, nk), 1)
    rel = jnp.abs(col - side - row)
    kglob = i * tq + col - side
    valid = (rel <= side) & (kglob >= 0) & (kglob < n)
    dist = (rel * dil).astype(F32)
    scale = HEAD_DIM ** -0.5
    for h in range(HEADS_PER_GROUP):
        hs = slice(h * HEAD_DIM, (h + 1) * HEAD_DIM)
        q = q_ref[:, hs]
        kc = jnp.concatenate([kp_ref[:, hs], k_ref[:, hs], kn_ref[:, hs]], axis=0)
        vc = jnp.concatenate([vp_ref[:, hs], v_ref[:, hs], vn_ref[:, hs]], axis=0)
        s = lax.dot_general(q, kc, (((1,), (1,)), ((), ())), preferred_element_type=F32) * scale
        s = jnp.where(valid, s - float(slopes[h]) * dist, NEG_INF)
        m = jnp.max(s, axis=-1, keepdims=True)
        p = jnp.exp(s - m)
        den = jnp.sum(p, axis=-1, keepdims=True)
        o = jnp.dot(p.astype(BF16), vc, preferred_element_type=F32) / den
        o_ref[:, hs] = o.astype(o_ref.dtype)
        l_ref[:, hs] = jnp.broadcast_to(m + jnp.log(den), (tq, HEAD_DIM))


def _dilated_attention(qkv, n, dil, slopes, col0=0):
    streams = qkv.shape[0]
    tq = min(256, n)
    side = ATTN_SIDE
    nh = n // side
    gw = GROUP_WIDTH
    main = lambda cb: pl.BlockSpec((None, tq, gw), lambda s, i: (s, i, col0 + cb))
    prev = lambda cb: pl.BlockSpec((None, side, gw),
                                   lambda s, i: (s, jnp.maximum(i * (tq // side) - 1, 0), col0 + cb))
    nxt = lambda cb: pl.BlockSpec((None, side, gw),
                                  lambda s, i: (s, jnp.minimum((i + 1) * (tq // side), nh - 1), col0 + cb))
    return pl.pallas_call(
        functools.partial(_attn_kernel, tq=tq, n=n, dil=dil, slopes=tuple(float(v) for v in slopes)),
        out_shape=(_sds((streams, n, gw), BF16), _sds((streams, n, gw), F32)),
        grid=(streams, n // tq),
        in_specs=[main(0), main(1), prev(1), nxt(1), main(2), prev(2), nxt(2)],
        out_specs=(pl.BlockSpec((None, tq, gw), lambda s, i: (s, i, 0)),
                   pl.BlockSpec((None, tq, gw), lambda s, i: (s, i, 0))),
        compiler_params=_params(("arbitrary", "arbitrary")),
        name=f"dilated_attention_d{dil}",
    )(qkv, qkv, qkv, qkv, qkv, qkv, qkv)


def _merge_kernel(y_ref, z_ref, x1_ref, skip_ref, o0_ref, l0_ref, o1_ref, l1_ref, o2_ref, l2_ref,
                  ghy_ref, gat_ref, wh_ref, wa_ref, out_ref, hy_s, at_s, so1, sl1, so2, sl2, *, tm, r, pitch):
    j = pl.program_id(1)
    nct = GROUP_WIDTH // LANES

    @pl.when(j == 0)
    def _():
        for g in range(tm // r):
            rows = slice(g * pitch, g * pitch + r)
            hy_s[g * r:(g + 1) * r, :] = ((y_ref[rows, :] + z_ref[rows, :] * skip_ref[...])
                                          * x1_ref[g * r:(g + 1) * r, :]).astype(BF16)
        for dil, oref, lref, so, sl in ((4, o1_ref, l1_ref, so1, sl1), (16, o2_ref, l2_ref, so2, sl2)):
            for res in range(dil):
                for c in range(nct):
                    cs = slice(c * LANES, (c + 1) * LANES)
                    so[c, pl.ds(res, tm // dil, stride=dil), :] = oref[res, :, cs].astype(F32)
                    sl[c, pl.ds(res, tm // dil, stride=dil), :] = lref[res, :, cs]
        for c in range(nct):
            cs = slice(c * LANES, (c + 1) * LANES)
            a0, a1, a2 = l0_ref[:, cs], sl1[c], sl2[c]
            m = jnp.maximum(jnp.maximum(a0, a1), a2)
            e0, e1, e2 = jnp.exp(a0 - m), jnp.exp(a1 - m), jnp.exp(a2 - m)
            at = (e0 * o0_ref[:, cs].astype(F32) + e1 * so1[c] + e2 * so2[c]) / (e0 + e1 + e2)
            at_s[:, cs] = at.astype(BF16)

    acc_h = jnp.dot(hy_s[...], wh_ref[j], preferred_element_type=F32)
    acc_a = jnp.dot(at_s[...], wa_ref[j], preferred_element_type=F32)
    out = (jax.nn.sigmoid(ghy_ref[...].astype(F32)) * acc_h + jax.nn.sigmoid(gat_ref[...].astype(F32)) * acc_a)
    out_ref[...] = out.astype(out_ref.dtype)


def _merge(yconv, zin, x1c, skip, o0, l0, o1, l1, o2, l2, pn, ghy_block0, gat_block0, wh, wa, batch, seq, r, pitch):
    n_tok, hw = x1c.shape
    d_model = wh.shape[1]
    gw = GROUP_WIDTH
    tm = 256
    tn = COL_BLOCK
    spb = seq // tm
    row = lambda width: pl.BlockSpec((tm, width), lambda i, j: (i, 0))
    prow = pl.BlockSpec((tm // r * pitch, hw), lambda i, j: (i, 0))
    res = lambda dil: pl.BlockSpec((None, dil, tm // dil, gw), lambda i, j: (i // spb, 0, i % spb, 0))
    nj = d_model // tn
    wh3 = wh.reshape(hw, nj, tn).transpose(1, 0, 2)
    wa3 = wa.reshape(gw, nj, tn).transpose(1, 0, 2)
    resident = lambda rows: pl.BlockSpec((nj, rows, tn), lambda i, j: (0, 0, 0), pipeline_mode=pl.Buffered(1))
    return pl.pallas_call(
        functools.partial(_merge_kernel, tm=tm, r=r, pitch=pitch),
        out_shape=_sds((n_tok, d_model), BF16),
        grid=(n_tok // tm, d_model // tn),
        in_specs=[prow, prow, row(hw), pl.BlockSpec((1, hw), lambda i, j: (0, 0)),
                  row(gw), row(gw), res(4), res(4), res(16), res(16),
                  pl.BlockSpec((tm, tn), lambda i, j: (i, ghy_block0 + j)),
                  pl.BlockSpec((tm, tn), lambda i, j: (i, gat_block0 + j)),
                  resident(hw), resident(gw)],
        out_specs=pl.BlockSpec((tm, tn), lambda i, j: (i, j)),
        scratch_shapes=[pltpu.VMEM((tm, hw), BF16), pltpu.VMEM((tm, gw), BF16)]
                       + [pltpu.VMEM((gw // LANES, tm, LANES), F32)] * 4,
        compiler_params=_params(("arbitrary", "arbitrary")),
        name="gated_merge",
    )(yconv, zin, x1c, skip, o0, l0, o1, l1, o2, l2, pn, pn, wh3, wa3)


def _router_kernel(mo_ref, x_ref, gm_ref, gpost_ref, gpre_ref, sc_ref, sh_ref, wr_ref, br_ref,
                   x1_ref, h2_ref, idx_ref, tw_ref, *, ts, d_model, n_experts):
    mo = mo_ref[...]
    y = mo * lax.rsqrt(jnp.mean(mo * mo, axis=-1, keepdims=True) + RMS_EPS) * gpost_ref[...]
    x1 = x_ref[...] + gm_ref[...] * y
    x1_ref[...] = x1
    h2 = (x1 * lax.rsqrt(jnp.mean(x1 * x1, axis=-1, keepdims=True) + RMS_EPS) * gpre_ref[...]
          * (1.0 + sc_ref[...]) + sh_ref[...])
    nct = d_model // 2 // LANES
    sp = nct + SLAB_PAD
    words = _pack_bf16_pairs(h2)
    for c in range(nct):
        h2_ref[pl.ds(c, ts, stride=sp), :] = words[:, c * LANES:(c + 1) * LANES]
    for c in range(nct, sp):
        h2_ref[pl.ds(c, ts, stride=sp), :] = jnp.zeros((ts, LANES), jnp.uint32)
    logits = jnp.dot(h2, wr_ref[...], precision=HIGHEST, preferred_element_type=F32) + br_ref[...]
    lane = lax.broadcasted_iota(jnp.int32, logits.shape, 1)
    lane_f = lane.astype(F32)
    logits = jnp.where(lane < n_experts, logits, -jnp.inf)
    idx_out = jnp.zeros(logits.shape, jnp.int32)
    val_out = jnp.zeros(logits.shape, F32)
    top0 = None
    den = None
    for k in range(TOP_K):
        m = jnp.max(logits, axis=-1, keepdims=True)
        idx = jnp.min(jnp.where(logits == m, lane_f, float(LANES)), axis=-1, keepdims=True).astype(jnp.int32)
        if k == 0:
            top0 = m
        e = jnp.exp(m - top0)
        den = e if den is None else den + e
        idx_out = jnp.where(lane == k, idx, idx_out)
        val_out = jnp.where(lane == k, e, val_out)
        logits = jnp.where(lane == idx, -jnp.inf, logits)
    idx_ref[...] = idx_out
    tw_ref[...] = val_out / den


def _post_mix_and_route(mo, x, gate_m, g_post, g_pre, scale_f, shift_f, w_router, b_router):
    b, s, d = x.shape
    e = w_router.shape[1]
    ts = 256
    wr = jnp.zeros((d, LANES), F32).at[:, :e].set(w_router)
    br = jnp.zeros((1, LANES), F32).at[0, :e].set(b_router)
    sp = d // 2 // LANES + SLAB_PAD
    spb = s // ts
    rowblk = lambda width: pl.BlockSpec((ts, width), lambda i: (i, 0))
    per_batch = pl.BlockSpec((None, 1, d), lambda i: (i // spb, 0, 0))
    vec = pl.BlockSpec((1, d), lambda i: (0, 0))
    n_tok = b * s
    return pl.pallas_call(
        functools.partial(_router_kernel, ts=ts, d_model=d, n_experts=e),
        out_shape=(_sds((n_tok, d), F32), _sds((n_tok * sp, LANES), jnp.uint32),
                   _sds((n_tok, LANES), jnp.int32), _sds((n_tok, LANES), F32)),
        grid=(n_tok // ts,),
        in_specs=[rowblk(d), rowblk(d), per_batch, vec, vec, per_batch, per_batch,
                  pl.BlockSpec((d, LANES), lambda i: (0, 0)), pl.BlockSpec((1, LANES), lambda i: (0, 0))],
        out_specs=(rowblk(d), pl.BlockSpec((ts * sp, LANES), lambda i: (i, 0)), rowblk(LANES), rowblk(LANES)),
        compiler_params=_params(("arbitrary",)),
        name="post_mix_route",
    )(mo, x.reshape(n_tok, d), gate_m, g_post, g_pre, scale_f, shift_f, wr, br)


def _expert_up_kernel(be_ref, nu_ref, tok0_ref, tok1_ref, h2_hbm, wg_ref, bg_ref, wu_ref, bu_ref, act_ref,
                      xbuf, xb, sem, *, tb, nct):
    i = pl.program_id(0)
    n_used = nu_ref[0]
    sp = nct + SLAB_PAD

    def row_copy(t, j, slot):
        return pltpu.make_async_copy(h2_hbm.at[pl.ds(pl.multiple_of(t * sp, 8), nct), :],
                                     xbuf.at[slot, pl.ds(pl.multiple_of(j * sp, 8), nct), :], sem.at[slot])

    def gather(tok_ref, slot):
        def body(j, carry):
            row_copy(tok_ref[0, j], j, slot).start()
            return carry
        lax.fori_loop(0, tb, body, 0, unroll=8)

    def wait(slot):
        pltpu.make_async_copy(h2_hbm.at[pl.ds(0, tb * nct), :], xbuf.at[slot, pl.ds(0, tb * nct), :],
                              sem.at[slot]).wait()

    @pl.when((i == 0) & (n_used > 0))
    def _():
        gather(tok0_ref, 0)

    @pl.when(i < n_used)
    def _():
        slot = i % 2
        wait(slot)
        for j in range(tb):
            row_copy(tok1_ref[0, j], j, 1 - slot).start()
        for c in range(nct):
            hi, lo = _unpack_bf16_pairs(xbuf[slot, pl.ds(c, tb, stride=sp), :])
            xb[:, 2 * c * LANES:(2 * c + 1) * LANES] = hi.astype(BF16)
            xb[:, (2 * c + 1) * LANES:(2 * c + 2) * LANES] = lo.astype(BF16)
        x = xb[...]
        g = jnp.dot(x, wg_ref[...], preferred_element_type=F32) + bg_ref[...]
        u = jnp.dot(x, wu_ref[...], preferred_element_type=F32) + bu_ref[...]
        g = jnp.minimum(g, SWIGLU_LIMIT)
        u = jnp.clip(u, -SWIGLU_LIMIT, SWIGLU_LIMIT)
        act_ref[...] = (g * jax.nn.sigmoid(SWIGLU_ALPHA * g) * (u + 1.0)).astype(act_ref.dtype)

    @pl.when((i == n_used) & (n_used > 0))
    def _():
        wait(i % 2)

    @pl.when(i >= n_used)
    def _():
        act_ref[...] = jnp.zeros(act_ref.shape, act_ref.dtype)


def _expert_up(block_e, n_used, row_tok3, h2s, wg, bg, wu, bu, nct):
    n_blocks, _, tb = row_tok3.shape
    _, d, f = wg.shape
    grid_spec = pltpu.PrefetchScalarGridSpec(
        num_scalar_prefetch=2,
        grid=(n_blocks,),
        in_specs=[pl.BlockSpec((None, 1, tb), lambda i, be, nu: (i, 0, 0), memory_space=pltpu.SMEM),
                  pl.BlockSpec((None, 1, tb), lambda i, be, nu: (jnp.minimum(i + 1, n_blocks - 1), 0, 0),
                               memory_space=pltpu.SMEM),
                  pl.BlockSpec(memory_space=pl.ANY),
                  pl.BlockSpec((None, d, f), lambda i, be, nu: (be[i], 0, 0)),
                  pl.BlockSpec((None, 1, f), lambda i, be, nu: (be[i], 0, 0)),
                  pl.BlockSpec((None, d, f), lambda i, be, nu: (be[i], 0, 0)),
                  pl.BlockSpec((None, 1, f), lambda i, be, nu: (be[i], 0, 0))],
        out_specs=pl.BlockSpec((tb, f), lambda i, be, nu: (i, 0)),
        scratch_shapes=[pltpu.VMEM((2, tb * (nct + SLAB_PAD), LANES), jnp.uint32), pltpu.VMEM((tb, d), BF16),
                        pltpu.SemaphoreType.DMA((2,))],
    )
    return pl.pallas_call(
        functools.partial(_expert_up_kernel, tb=tb, nct=nct),
        out_shape=_sds((n_blocks * tb, f), BF16),
        grid_spec=grid_spec,
        compiler_params=_params(("arbitrary",)),
        name="expert_up",
    )(block_e, n_used, row_tok3, row_tok3, h2s, wg, bg, wu, bu)


def _expert_down_kernel(be_ref, nu_ref, act_ref, wd_ref, bd_ref, ys_ref, wb, *, tb, nct):
    i = pl.program_id(0)

    @pl.when((i < nu_ref[0]) & ((i == 0) | (be_ref[i] != be_ref[jnp.maximum(i - 1, 0)])))
    def _():
        wb[...] = wd_ref[...].astype(BF16)

    @pl.when(i < nu_ref[0])
    def _():
        act = act_ref[...]
        sp = nct + SLAB_PAD
        for c in range(nct):
            cols = slice(2 * c * LANES, (2 * c + 2) * LANES)
            y = jnp.dot(act, wb[:, cols], preferred_element_type=F32) + bd_ref[:, cols]
            ys_ref[pl.ds(c, tb, stride=sp), :] = _pack_bf16_pairs(y)
        for c in range(nct, sp):
            ys_ref[pl.ds(c, tb, stride=sp), :] = jnp.zeros((tb, LANES), jnp.uint32)

    @pl.when(i >= nu_ref[0])
    def _():
        ys_ref[...] = jnp.zeros(ys_ref.shape, ys_ref.dtype)


def _expert_down(block_e, n_used, act, wd, bd, tb):
    _, f, d = wd.shape
    n_blocks = act.shape[0] // tb
    nct = d // 2 // LANES
    grid_spec = pltpu.PrefetchScalarGridSpec(
        num_scalar_prefetch=2,
        grid=(n_blocks,),
        in_specs=[pl.BlockSpec((tb, f), lambda i, be, nu: (i, 0)),
                  pl.BlockSpec((None, f, d), lambda i, be, nu: (be[i], 0, 0)),
                  pl.BlockSpec((None, 1, d), lambda i, be, nu: (be[i], 0, 0))],
        out_specs=pl.BlockSpec((tb * (nct + SLAB_PAD), LANES), lambda i, be, nu: (i, 0)),
        scratch_shapes=[pltpu.VMEM((f, d), BF16)],
    )
    return pl.pallas_call(
        functools.partial(_expert_down_kernel, tb=tb, nct=nct),
        out_shape=_sds((n_blocks * tb * (nct + SLAB_PAD), LANES), jnp.uint32),
        grid_spec=grid_spec,
        compiler_params=_params(("arbitrary",)),
        name="expert_down",
    )(block_e, n_used, act, wd, bd)


def _combine_kernel(d0_ref, d1_ref, ys_hbm, tw_ref, x1_ref, gf_ref, gpost_ref, o_ref, buf, ff, sem, *, tc, nct):
    i = pl.program_id(0)
    last = pl.num_programs(0) - 1
    n_rows = TOP_K * tc
    sp = nct + SLAB_PAD

    def row_copy(r, j, slot):
        return pltpu.make_async_copy(ys_hbm.at[pl.ds(pl.multiple_of(r * sp, 8), nct), :],
                                     buf.at[slot, pl.ds(pl.multiple_of(j * sp, 8), nct), :], sem.at[slot])

    def gather(dref, slot):
        def body(j, carry):
            row_copy(dref[0, j], j, slot).start()
            return carry
        lax.fori_loop(0, n_rows, body, 0, unroll=8)

    def wait(slot):
        pltpu.make_async_copy(ys_hbm.at[pl.ds(0, n_rows * nct), :], buf.at[slot, pl.ds(0, n_rows * nct), :],
                              sem.at[slot]).wait()

    @pl.when(i == 0)
    def _():
        gather(d0_ref, 0)

    slot = i % 2
    wait(slot)
    for j in range(n_rows):
        row_copy(d1_ref[0, j], j, 1 - slot).start()
    wk =[tw_ref[:, k:k + 1] for k in range(TOP_K)]
    for c in range(nct):
        acc_hi = acc_lo = None
        for k in range(TOP_K):
            hi, lo = _unpack_bf16_pairs(buf[slot, pl.ds(k * tc * sp + c, tc, stride=sp), :])
            acc_hi = wk[k] * hi if acc_hi is None else acc_hi + wk[k] * hi
            acc_lo = wk[k] * lo if acc_lo is None else acc_lo + wk[k] * lo
        ff[:, 2 * c * LANES:(2 * c + 1) * LANES] = acc_hi
        ff[:, (2 * c + 1) * LANES:(2 * c + 2) * LANES] = acc_lo
    f = ff[...]
    y = f * lax.rsqrt(jnp.mean(f * f, axis=-1, keepdims=True) + RMS_EPS) * gpost_ref[...]
    o_ref[...] = x1_ref[...] + gf_ref[...] * y

    @pl.when(i == last)
    def _():
        wait(1 - slot)


def _combine(dest3, ys, top_w, x1, gate_f, g_post, seq):
    n_tok, d = x1.shape
    n_steps, _, n_rows = dest3.shape
    tc = n_rows // TOP_K
    nct = d // 2 // LANES
    spb = seq // tc
    return pl.pallas_call(
        functools.partial(_combine_kernel, tc=tc, nct=nct),
        out_shape=_sds((n_tok, d), F32),
        grid=(n_steps,),
        in_specs=[pl.BlockSpec((None, 1, n_rows), lambda i: (i, 0, 0), memory_space=pltpu.SMEM),
                  pl.BlockSpec((None, 1, n_rows), lambda i: (jnp.minimum(i + 1, n_steps - 1), 0, 0),
                               memory_space=pltpu.SMEM),
                  pl.BlockSpec(memory_space=pl.ANY),
                  pl.BlockSpec((tc, LANES), lambda i: (i, 0)),
                  pl.BlockSpec((tc, d), lambda i: (i, 0)),
                  pl.BlockSpec((None, 1, d), lambda i: (i // spb, 0, 0)),
                  pl.BlockSpec((1, d), lambda i: (0, 0))],
        out_specs=pl.BlockSpec((tc, d), lambda i: (i, 0)),
        scratch_shapes=[pltpu.VMEM((2, n_rows * (nct + SLAB_PAD), LANES), jnp.uint32), pltpu.VMEM((tc, d), F32),
                        pltpu.SemaphoreType.DMA((2,))],
        compiler_params=_params(("arbitrary",)),
        name="expert_combine",
    )(dest3, dest3, ys, top_w, x1, gate_f, g_post)


def _rank_kernel(idx_ref, tri_ref, rank_ref, cnt_ref, carry, *, tr):
    @pl.when(pl.program_id(0) == 0)
    def _():
        carry[...] = jnp.zeros(carry.shape, F32)

    idx = idx_ref[...]
    lane = lax.broadcasted_iota(jnp.int32, idx.shape, 1)
    base = carry[0:1, :]
    out = jnp.zeros(idx.shape, jnp.int32)
    for k in range(TOP_K):
        onehot = jnp.where(lane == idx[:, k:k + 1], 1.0, 0.0)
        csum = jnp.dot(tri_ref[...], onehot.astype(BF16), preferred_element_type=F32)
        rank = jnp.sum(onehot * (csum + base), axis=-1, keepdims=True) - 1.0
        out = jnp.where(lane == k, rank.astype(jnp.int32), out)
        base = base + csum[tr - 1:tr, :]
    rank_ref[...] = out
    carry[...] = jnp.broadcast_to(base, carry.shape)
    cnt_ref[...] = jnp.broadcast_to(base, cnt_ref.shape).astype(jnp.int32)


def _expert_ranks(top_idx_padded):
    n_tok = top_idx_padded.shape[0]
    tr = 512
    tri = jnp.asarray(np.tril(np.ones((tr, tr), np.float32)), BF16)
    return pl.pallas_call(
        functools.partial(_rank_kernel, tr=tr),
        out_shape=(_sds((n_tok, LANES), jnp.int32), _sds((8, LANES), jnp.int32)),
        grid=(n_tok // tr,),
        in_specs=[pl.BlockSpec((tr, LANES), lambda i: (i, 0)), pl.BlockSpec((tr, tr), lambda i: (0, 0))],
        out_specs=(pl.BlockSpec((tr, LANES), lambda i: (i, 0)), pl.BlockSpec((8, LANES), lambda i: (0, 0))),
        scratch_shapes=[pltpu.VMEM((8, LANES), F32)],
        compiler_params=_params(("arbitrary",)),
        name="expert_ranks",
    )(top_idx_padded, tri)


def _routing_tables(top_idx_padded, n_experts, tb):
    n_tok = top_idx_padded.shape[0]
    n_assign = n_tok * TOP_K
    ranks, counts = _expert_ranks(top_idx_padded)
    sizes = counts[0, :n_experts]
    padded = (sizes + tb - 1) // tb * tb
    pad_end = jnp.cumsum(padded)
    pad_start = pad_end - padded
    top_idx = top_idx_padded[:, :TOP_K]
    experts = jnp.arange(n_experts, dtype=jnp.int32)
    start_of = jnp.sum(jnp.where(top_idx[:, :, None] == experts, pad_start, 0), axis=-1)
    dest = (start_of + ranks[:, :TOP_K]).astype(jnp.int32).reshape(-1)
    n_rows = -(-n_assign // tb) * tb + n_experts * tb
    n_blocks = n_rows // tb
    tok = (jnp.arange(n_assign, dtype=jnp.int32) // TOP_K)
    row_tok = jnp.zeros((n_rows,), jnp.int32).at[dest].set(tok)
    block_start = jnp.arange(n_blocks, dtype=jnp.int32) * tb
    block_e = jnp.minimum(jnp.sum((pad_end[None, :] <= block_start[:, None]).astype(jnp.int32), axis=1),
                          n_experts - 1).astype(jnp.int32)
    n_used = (pad_end[-1] // tb).astype(jnp.int32).reshape(1)
    return dest, row_tok, block_e, n_used, n_rows


def _moe(h2s, top_idx, top_w, wg, bg, wu, bu, wd, bd, x1, gate_f, g_post, seq):
    n_tok, d = x1.shape
    n_experts = wg.shape[0]
    tb = MOE_ROWS
    nct = d // 2 // LANES
    dest, row_tok, block_e, n_used, n_rows = _routing_tables(top_idx, n_experts, tb)
    n_blocks = n_rows // tb
    act = _expert_up(block_e, n_used, row_tok.reshape(n_blocks, 1, tb), h2s, wg, bg[:, None, :], wu,
                     bu[:, None, :], nct)
    ys = _expert_down(block_e, n_used, act, wd, bd[:, None, :], tb)
    tc = 128
    dest3 = dest.reshape(n_tok // tc, tc, TOP_K).transpose(0, 2, 1).reshape(n_tok // tc, 1, TOP_K * tc)
    return _combine(dest3, ys, top_w, x1, gate_f, g_post, seq)


def _layer(x, c8, p):
    b, s, d = x.shape
    n_tok = b * s
    hw = p["hy_skip"].shape[0]
    gw = GROUP_WIDTH
    assert hw == COL_BLOCK and gw == COL_BLOCK and d % COL_BLOCK == 0

    mod = _adaln(c8, p["w_ada"], p["b_ada"][None])[:b]
    shift_m, scale_m, gate_m, shift_f, scale_f, gate_f = [m[:, None, :] for m in jnp.split(mod, 6, axis=-1)]

    h, h4, h16 = _prenorm_mix(x, p["g_pre_mix"][None], scale_m, shift_m)

    w_in = p["w_in"].astype(BF16)
    nd = d // COL_BLOCK
    nat_blocks = [0, 1, 2, 3, 6, 9] + list(range(12, 12 + 2 * nd))
    pn = _matmul(h.reshape(n_tok, d), w_in, nat_blocks, BF16, "in_proj_natural")
    qkv1 = _matmul(h4.reshape(n_tok, d), w_in, [4, 7, 10], BF16, "in_proj_dil4")
    qkv2 = _matmul(h16.reshape(n_tok, d), w_in, [5, 8, 11], BF16, "in_proj_dil16")

    r, pitch = _fft_radix(s)
    hfb = _hyena_filters(s, r, pitch, p["hy_f_w1"], p["hy_f_b1"], p["hy_f_w2"], p["hy_f_b2"], p["hy_f_w3"],
                         p["hy_f_b3"], p["hy_f_freq"], p["hy_f_wout"])
    zin, x1c = _shortconv(pn.reshape(b, s, -1), p["hy_conv_w"], p["hy_conv_b"], hw, r, pitch)
    yconv = _long_conv(zin, hfb, r, pitch)

    slopes = _alibi_slopes(N_GROUPS * HEADS_PER_GROUP).reshape(N_GROUPS, HEADS_PER_GROUP)
    o0, l0 = _dilated_attention(pn.reshape(b, s, -1), s, 1, slopes[0], col0=3)
    o1, l1 = _dilated_attention(qkv1.reshape(b * 4, s // 4, 3 * gw), s // 4, 4, slopes[1])
    o2, l2 = _dilated_attention(qkv2.reshape(b * 16, s // 16, 3 * gw), s // 16, 16, slopes[2])

    merged = _merge(yconv.reshape(-1, hw), zin.reshape(-1, hw), x1c.reshape(n_tok, hw), p["hy_skip"][None],
                    o0.reshape(n_tok, gw), l0.reshape(n_tok, gw),
                    o1.reshape(b, 4, s // 4, gw), l1.reshape(b, 4, s // 4, gw),
                    o2.reshape(b, 16, s // 16, gw), l2.reshape(b, 16, s // 16, gw),
                    pn, 6, 6 + nd, p["w_proj_hyena"].astype(BF16), p["w_proj_attn"].astype(BF16), b, s, r, pitch)
    mo = _matmul(merged, p["w_out"].astype(BF16), list(range(nd)), F32, "out_proj")

    x1, h2s, top_idx, top_w = _post_mix_and_route(mo, x, gate_m, p["g_post_mix"][None], p["g_pre_ffn"][None],
                                                  scale_f, shift_f, p["w_router"], p["b_router"])
    out = _moe(h2s, top_idx, top_w, p["w_gate"].astype(BF16), p["b_gate"],
               p["w_up"].astype(BF16), p["b_up"], p["w_down"], p["b_down"], x1, gate_f,
               p["g_post_ffn"][None], s)
    return out.reshape(b, s, d)


def kernel(x, c, w_ada, b_ada, g_pre_mix, g_post_mix, g_pre_ffn, g_post_ffn, w_in, hy_conv_w, hy_conv_b, hy_skip, hy_f_w1, hy_f_b1, hy_f_w2, hy_f_b2, hy_f_w3, hy_f_b3, hy_f_freq, hy_f_wout, w_proj_hyena, w_proj_attn, w_out, w_router, b_router, w_gate, b_gate, w_up, b_up, w_down, b_down):
    names = ("w_ada", "b_ada", "g_pre_mix", "g_post_mix", "g_pre_ffn", "g_post_ffn", "w_in", "hy_conv_w",
             "hy_conv_b", "hy_skip", "hy_f_w1", "hy_f_b1", "hy_f_w2", "hy_f_b2", "hy_f_w3", "hy_f_b3",
             "hy_f_freq", "hy_f_wout", "w_proj_hyena", "w_proj_attn", "w_out", "w_router", "b_router",
             "w_gate", "b_gate", "w_up", "b_up", "w_down", "b_down")
    stacked = (w_ada, b_ada, g_pre_mix, g_post_mix, g_pre_ffn, g_post_ffn, w_in, hy_conv_w, hy_conv_b, hy_skip,
               hy_f_w1, hy_f_b1, hy_f_w2, hy_f_b2, hy_f_w3, hy_f_b3, hy_f_freq, hy_f_wout, w_proj_hyena,
               w_proj_attn, w_out, w_router, b_router, w_gate, b_gate, w_up, b_up, w_down, b_down)
    depth = w_ada.shape[0]
    b = x.shape[0]
    c8 = jnp.zeros((8, c.shape[1]), F32).at[:b].set(c)
    for l in range(depth):
        x = _layer(x, c8, {k: v[l] for k, v in zip(names, stacked)})
    return x
```

```python
import functools
import math

import jax
import jax.numpy as jnp
import numpy as np
from jax import lax
from jax.experimental import pallas as pl
from jax.experimental.pallas import tpu as pltpu

F32 = jnp.float32
BF16 = jnp.bfloat16
HIGHEST = lax.Precision.HIGHEST

LANES = 128
HEAD_DIM = 128
HEADS_PER_GROUP = 8
DILATED_GROUPS = ((128, 1), (512, 4), (2048, 16))
N_GROUPS = len(DILATED_GROUPS)
GROUP_WIDTH = HEADS_PER_GROUP * HEAD_DIM
ATTN_SIDE = 64
TOP_K = 4
SWIGLU_LIMIT = 7.0
SWIGLU_ALPHA = 1.702
RMS_EPS = 1e-6
NEG_INF = -1e30
HYENA_N_BANDS = 16
HYENA_DECAY_TARGET = 1e-2
HYENA_FAST_DECAY_PCT = 0.3
HYENA_SLOW_DECAY_PCT = 1.5
COL_BLOCK = 1024
MOE_ROWS = 256
SLAB_PAD = 8
VMEM_LIMIT = 56 * 1024 * 1024


def _pack_bf16_pairs(x):
    bits = lax.bitcast_convert_type(x.astype(BF16).astype(F32), jnp.uint32)
    groups = [bits[:, j:j + LANES] | (bits[:, j + LANES:j + 2 * LANES] >> 16)
              for j in range(0, x.shape[1], 2 * LANES)]
    return groups[0] if len(groups) == 1 else jnp.concatenate(groups, axis=1)


def _unpack_bf16_pairs(w):
    hi = lax.bitcast_convert_type(w & jnp.uint32(0xFFFF0000), F32)
    lo = lax.bitcast_convert_type(w << 16, F32)
    return hi, lo


def _params(sem, vmem=VMEM_LIMIT):
    return pltpu.CompilerParams(dimension_semantics=sem, vmem_limit_bytes=vmem)


def _sds(shape, dtype):
    return jax.ShapeDtypeStruct(shape, dtype)


def _ada_kernel(c_ref, w_ref, b_ref, o_ref):
    c = c_ref[...]
    sc = (c * jax.nn.sigmoid(c)).astype(BF16)
    o_ref[...] = jnp.dot(sc, w_ref[...].astype(BF16), preferred_element_type=F32) + b_ref[...]


def _adaln(c8, w_ada, b_ada):
    d, cols = w_ada.shape
    tn = 512
    return pl.pallas_call(
        _ada_kernel,
        out_shape=_sds((8, cols), F32),
        grid=(cols // tn,),
        in_specs=[pl.BlockSpec((8, d), lambda j: (0, 0)),
                  pl.BlockSpec((d, tn), lambda j: (0, j)),
                  pl.BlockSpec((1, tn), lambda j: (0, j))],
        out_specs=pl.BlockSpec((8, tn), lambda j: (0, j)),
        compiler_params=_params(("arbitrary",)),
        name="adaln",
    )(c8, w_ada, b_ada)


def _prenorm_kernel(x_ref, g_ref, sc_ref, sh_ref, o_ref, o4_ref, o16_ref, scr_ref, *, ts, d_model):
    x = x_ref[...]
    ms = jnp.mean(x * x, axis=-1, keepdims=True)
    h = x * lax.rsqrt(ms + RMS_EPS) * g_ref[...] * (1.0 + sc_ref[...]) + sh_ref[...]
    o_ref[...] = h.astype(BF16)
    nct = d_model // LANES
    for c in range(nct):
        scr_ref[c] = h[:, c * LANES:(c + 1) * LANES]
    for dil, oref in ((4, o4_ref), (16, o16_ref)):
        for r in range(dil):
            for c in range(nct):
                oref[r, :, c * LANES:(c + 1) * LANES] = scr_ref[c, pl.ds(r, ts // dil, stride=dil), :].astype(BF16)


def _prenorm_mix(x, g, scale, shift):
    b, s, d = x.shape
    ts = 256
    kern = functools.partial(_prenorm_kernel, ts=ts, d_model=d)
    return pl.pallas_call(
        kern,
        out_shape=(_sds((b, s, d), BF16), _sds((b, 4, s // 4, d), BF16), _sds((b, 16, s // 16, d), BF16)),
        grid=(b, s // ts),
        in_specs=[pl.BlockSpec((None, ts, d), lambda bi, i: (bi, i, 0)),
                  pl.BlockSpec((1, d), lambda bi, i: (0, 0)),
                  pl.BlockSpec((None, 1, d), lambda bi, i: (bi, 0, 0)),
                  pl.BlockSpec((None, 1, d), lambda bi, i: (bi, 0, 0))],
        out_specs=(pl.BlockSpec((None, ts, d), lambda bi, i: (bi, i, 0)),
                   pl.BlockSpec((None, 4, ts // 4, d), lambda bi, i: (bi, 0, i, 0)),
                   pl.BlockSpec((None, 16, ts // 16, d), lambda bi, i: (bi, 0, i, 0))),
        scratch_shapes=[pltpu.VMEM((d // LANES, ts, LANES), F32)],
        compiler_params=_params(("arbitrary", "arbitrary")),
        name="prenorm_mix",
    )(x, g, scale, shift)


def _mm_kernel(tbl_ref, a_ref, w_ref, o_ref):
    del tbl_ref
    o_ref[...] = jnp.dot(a_ref[...], w_ref[...], preferred_element_type=F32).astype(o_ref.dtype)


def _matmul(a, w, col_blocks, out_dtype, name, tm=1024, tn=COL_BLOCK):
    m, k = a.shape
    tm = min(tm, m)
    nb = len(col_blocks)
    tbl = jnp.asarray(col_blocks, jnp.int32)
    grid_spec = pltpu.PrefetchScalarGridSpec(
        num_scalar_prefetch=1,
        grid=(m // tm, nb),
        in_specs=[pl.BlockSpec((tm, k), lambda i, j, t: (i, 0)),
                  pl.BlockSpec((k, tn), lambda i, j, t: (0, t[j]))],
        out_specs=pl.BlockSpec((tm, tn), lambda i, j, t: (i, j)),
    )
    return pl.pallas_call(
        _mm_kernel,
        out_shape=_sds((m, nb * tn), out_dtype),
        grid_spec=grid_spec,
        compiler_params=_params(("arbitrary", "arbitrary")),
        name=name,
    )(tbl, a, w)


def _filter_kernel(z_ref, w1_ref, b1_ref, w2_ref, b2_ref, w3_ref, b3_ref, fr_ref, wo_ref, dl_ref,
                   h_ref, *, hw, r, pitch):
    def dot(a, b):
        return jnp.dot(a, b, precision=HIGHEST, preferred_element_type=F32)

    z = z_ref[...]
    fr = fr_ref[...]
    h = jnp.sin(fr * (dot(z, w1_ref[...]) + b1_ref[...]))
    h = jnp.sin(fr * (dot(h, w2_ref[...]) + b2_ref[...]))
    h = jnp.sin(fr * (dot(h, w3_ref[...]) + b3_ref[...]))
    filt = dot(h, wo_ref[...])
    decay = jnp.exp(-z[:, 0:1] * dl_ref[...])
    hf = filt[:, :hw] * decay
    hb = filt[:, hw:] * decay
    pad = jnp.zeros((pitch - r, hw), F32)
    for g in range(z.shape[0] // r):
        h_ref[0, g * pitch:g * pitch + r, :] = hf[g * r:(g + 1) * r]
        h_ref[1, g * pitch:g * pitch + r, :] = hb[g * r:(g + 1) * r]
        h_ref[0, g * pitch + r:(g + 1) * pitch, :] = pad
        h_ref[1, g * pitch + r:(g + 1) * pitch, :] = pad


def _hyena_filters(length, r, pitch, w1, b1, w2, b2, w3, b3, freq, wout):
    emb, fw = w1.shape
    hw = wout.shape[1] // 2
    t = np.linspace(0.0, 1.0, length)[:, None]
    bands = np.linspace(1e-4, HYENA_N_BANDS - 1, HYENA_N_BANDS)[None, :]
    ang = (2.0 * math.pi / length) * np.arange(length)[:, None] * bands
    z = np.concatenate([t, np.cos(ang), -np.sin(ang)], axis=-1)
    zpad = np.zeros((length, LANES), np.float32)
    zpad[:, :emb] = z
    w1p = jnp.zeros((LANES, fw), F32).at[:emb].set(w1)
    min_decay = math.log(HYENA_DECAY_TARGET) / HYENA_FAST_DECAY_PCT
    max_decay = math.log(HYENA_DECAY_TARGET) / HYENA_SLOW_DECAY_PCT
    deltas = np.abs(np.linspace(min_decay, max_decay, hw))[None, :].astype(np.float32)
    tl = min(1024, length)
    full = lambda shape: pl.BlockSpec(shape, lambda i: (0,) * len(shape))
    return pl.pallas_call(
        functools.partial(_filter_kernel, hw=hw, r=r, pitch=pitch),
        out_shape=_sds((2, length // r * pitch, hw), F32),
        grid=(length // tl,),
        in_specs=[pl.BlockSpec((tl, LANES), lambda i: (i, 0)),
                  full((LANES, fw)), full((1, fw)), full((fw, fw)), full((1, fw)),
                  full((fw, fw)), full((1, fw)), full((1, fw)), full((fw, 2 * hw)), full((1, hw))],
        out_specs=pl.BlockSpec((2, tl // r * pitch, hw), lambda i: (0, i, 0)),
        compiler_params=_params(("arbitrary",)),
        name="hyena_filters",
    )(jnp.asarray(zpad), w1p, b1[None], w2, b2[None], w3, b3[None], freq[None], wout, jnp.asarray(deltas))


def _shortconv_kernel(u_ref, up_ref, un_ref, w_ref, b_ref, z_ref, x1_ref, *, ts, hw, r, pitch):
    i = pl.program_id(1)
    last = pl.num_programs(1) - 1
    u = u_ref[...].astype(F32)
    prev_blk = up_ref[...].astype(F32)
    next_blk = un_ref[...].astype(F32)
    prev_row = jnp.where(i > 0, prev_blk[15:16, :], 0.0)
    next_row = jnp.where(i < last, next_blk[0:1, :], 0.0)
    row = lax.broadcasted_iota(jnp.int32, u.shape, 0)
    um = jnp.where(row == 0, prev_row, pltpu.roll(u, 1, 0))
    up = jnp.where(row == ts - 1, next_row, pltpu.roll(u, ts - 1, 0))
    w = w_ref[...]
    uc = w[0:1] * um + w[1:2] * u + w[2:3] * up + b_ref[...]
    x1_ref[...] = uc[:, :hw]
    z = uc[:, 2 * hw:] * uc[:, hw:2 * hw]
    pad = jnp.zeros((pitch - r, hw), F32)
    for g in range(ts // r):
        z_ref[g * pitch:g * pitch + r, :] = z[g * r:(g + 1) * r]
        z_ref[g * pitch + r:(g + 1) * pitch, :] = pad


def _shortconv(pn3, conv_w, conv_b, hw, r, pitch):
    b, s, _ = pn3.shape
    ts = 512
    w3 = 3 * hw
    nh = s // 16
    return pl.pallas_call(
        functools.partial(_shortconv_kernel, ts=ts, hw=hw, r=r, pitch=pitch),
        out_shape=(_sds((b, s // r * pitch, hw), F32), _sds((b, s, hw), F32)),
        grid=(b, s // ts),
        in_specs=[pl.BlockSpec((None, ts, w3), lambda bi, i: (bi, i, 0)),
                  pl.BlockSpec((None, 16, w3), lambda bi, i: (bi, jnp.maximum(i * (ts // 16) - 1, 0), 0)),
                  pl.BlockSpec((None, 16, w3), lambda bi, i: (bi, jnp.minimum((i + 1) * (ts // 16), nh - 1), 0)),
                  pl.BlockSpec((3, w3), lambda bi, i: (0, 0)),
                  pl.BlockSpec((1, w3), lambda bi, i: (0, 0))],
        out_specs=(pl.BlockSpec((None, ts // r * pitch, hw), lambda bi, i: (bi, i, 0)),
                   pl.BlockSpec((None, ts, hw), lambda bi, i: (bi, i, 0))),
        compiler_params=_params(("arbitrary", "arbitrary")),
        name="hyena_shortconv",
    )(pn3, pn3, pn3, conv_w, conv_b[None])


FFT_GROUP = 4


def _split_bf16(a):
    hi = a.astype(BF16)
    return hi, (a - hi.astype(F32)).astype(BF16)


def _dot3(fh, fl, x):
    xh, xl = _split_bf16(x)
    d = lambda a, b: jnp.dot(a, b, preferred_element_type=F32)
    return d(fh, xh) + (d(fh, xl) + d(fl, xh))


def _fft_kept(r):
    return (r // 2 + 1 + 7) // 8 * 8


def _dft_tables(r):
    idx = np.arange(r)
    ang = 2.0 * np.pi * np.outer(idx, idx) / r
    cos, sin = np.cos(ang), np.sin(ang)
    kp = _fft_kept(r)
    fa_half = np.concatenate([cos[:kp, :r // 2], -sin[:kp, :r // 2]], axis=0)
    fbig = np.block([[cos, sin], [-sin, cos]])
    fconj = np.block([[cos, -sin], [sin, cos]])
    wgt = np.zeros(kp)
    wgt[0] = wgt[r // 2] = 1.0
    wgt[1:r // 2] = 2.0
    gfin = np.concatenate([cos[:r // 2, :kp] * wgt, -sin[:r // 2, :kp] * wgt], axis=1)
    tang = 2.0 * np.pi * np.outer(idx, idx) / (r * r)
    tw = np.stack([np.cos(tang), -np.sin(tang)], axis=-1)
    split = lambda a: _split_bf16(jnp.asarray(a.astype(np.float32)))
    return split(fa_half), split(fbig), split(fconj), split(gfin), jnp.asarray(tw.astype(np.float32))


def _fft_a_kernel(x_ref, fh_ref, fl_ref, o_ref, *, r, k1, kp, pitch):
    fh, fl = fh_ref[...], fl_ref[...]
    pad = jnp.zeros((pitch - r, LANES), F32)
    for g in range(kp):
        o_ref[0, g * pitch + r:(g + 1) * pitch, :] = pad
        o_ref[1, g * pitch + r:(g + 1) * pitch, :] = pad

    def body(g, carry):
        n2 = g * FFT_GROUP
        xs = jnp.concatenate([x_ref[pl.ds(n2 + k, k1, stride=pitch), :] for k in range(FFT_GROUP)], axis=1)
        a = _dot3(fh, fl, xs)
        for k in range(FFT_GROUP):
            o_ref[0, pl.ds(n2 + k, kp, stride=pitch), :] = a[:kp, k * LANES:(k + 1) * LANES]
            o_ref[1, pl.ds(n2 + k, kp, stride=pitch), :] = a[kp:, k * LANES:(k + 1) * LANES]
        return carry

    lax.fori_loop(0, r // FFT_GROUP, body, 0)


def _fft_stage_a(x, fmat, r, pitch):
    bx, rows, c = x.shape
    k1 = rows // pitch
    kp = _fft_kept(r)
    return pl.pallas_call(
        functools.partial(_fft_a_kernel, r=r, k1=k1, kp=kp, pitch=pitch),
        out_shape=_sds((bx, 2, kp * pitch, c), F32),
        grid=(bx, c // LANES),
        in_specs=[pl.BlockSpec((None, rows, LANES), lambda b, ci: (b, 0, ci)),
                  pl.BlockSpec((2 * kp, k1), lambda b, ci: (0, 0)),
                  pl.BlockSpec((2 * kp, k1), lambda b, ci: (0, 0))],
        out_specs=pl.BlockSpec((None, 2, kp * pitch, LANES), lambda b, ci: (b, 0, 0, ci)),
        compiler_params=_params(("arbitrary", "arbitrary")),
        name="fft_stage_a",
    )(x, *fmat)


def _twiddled(a_ref, b, tr, ti, r):
    are, aim = a_ref[b, 0, :r, :], a_ref[b, 1, :r, :]
    return jnp.concatenate([are * tr - aim * ti, are * ti + aim * tr], axis=0)


def _fft_mk_kernel(a_ref, tw_ref, h0_ref, fbh_ref, fbl_ref, o_ref, *, r, scale):
    tr, ti = tw_ref[:, 0:1], tw_ref[:, 1:2]
    fbh, fbl = fbh_ref[...], fbl_ref[...]
    xf = _dot3(fbh, fbl, _twiddled(a_ref, 0, tr, ti, r))
    xb = _dot3(fbh, fbl, _twiddled(a_ref, 1, tr, ti, r))
    o_ref[0] = (xf[:r] + xb[:r] - h0_ref[...]) * scale
    o_ref[1] = (xf[r:] - xb[r:]) * scale


def _fft_filter_spectrum(a, tw, h0, fbig, r, pitch):
    c = a.shape[-1]
    kp = _fft_kept(r)
    return pl.pallas_call(
        functools.partial(_fft_mk_kernel, r=r, scale=1.0 / (r * r)),
        out_shape=_sds((2, kp * r, c), F32),
        grid=(kp,),
        in_specs=[pl.BlockSpec((2, 2, pitch, c), lambda k: (0, 0, k, 0)),
                  pl.BlockSpec((None, r, 2), lambda k: (k, 0, 0)),
                  pl.BlockSpec((1, c), lambda k: (0, 0)),
                  pl.BlockSpec((2 * r, 2 * r), lambda k: (0, 0)),
                  pl.BlockSpec((2 * r, 2 * r), lambda k: (0, 0))],
        out_specs=pl.BlockSpec((2, r, c), lambda k: (0, k, 0)),
        compiler_params=_params(("arbitrary",)),
        name="fft_filter_spectrum",
    )(a, tw, h0, *fbig)


def _fft_m_kernel(a_ref, ks_ref, tw_ref, fbh_ref, fbl_ref, fch_ref, fcl_ref, o_ref, *, r):
    tr, ti = tw_ref[:, 0:1], tw_ref[:, 1:2]
    x = _dot3(fbh_ref[...], fbl_ref[...], _twiddled(a_ref, 0, tr, ti, r))
    xre, xim = x[:r], x[r:]
    kre, kim = ks_ref[0], ks_ref[1]
    c = _dot3(fch_ref[...], fcl_ref[...],
              jnp.concatenate([xre * kre - xim * kim, xre * kim + xim * kre], axis=0))
    cre, cim = c[:r], c[r:]
    o_ref[0, :r, :] = cre * tr + cim * ti
    o_ref[1, :r, :] = cim * tr - cre * ti
    o_ref[:, r:, :] = jnp.zeros((2,) + (o_ref.shape[1] - r, o_ref.shape[2]), F32)


def _fft_stage_m(a, ks, tw, fbig, fconj, r, pitch):
    b, _, rows, c = a.shape
    a5 = a.reshape(b, 1, 2, rows, c)
    mat = pl.BlockSpec((2 * r, 2 * r), lambda k, bi: (0, 0))
    return pl.pallas_call(
        functools.partial(_fft_m_kernel, r=r),
        out_shape=_sds(a.shape, F32),
        grid=(rows // pitch, b),
        in_specs=[pl.BlockSpec((None, 1, 2, pitch, c), lambda k, bi: (bi, 0, 0, k, 0)),
                  pl.BlockSpec((2, r, c), lambda k, bi: (0, k, 0)),
                  pl.BlockSpec((None, r, 2), lambda k, bi: (k, 0, 0)),
                  mat, mat, mat, mat],
        out_specs=pl.BlockSpec((None, 2, pitch, c), lambda k, bi: (bi, 0, k, 0)),
        compiler_params=_params(("arbitrary", "arbitrary")),
        name="fft_stage_m",
    )(a5, ks, tw, *fbig, *fconj)


def _fft_f_kernel(d_ref, gh_ref, gl_ref, o_ref, *, r, kp, pitch):
    gh, gl = gh_ref[...], gl_ref[...]
    pad = jnp.zeros((pitch - r, LANES), F32)
    for g in range(r // 2):
        o_ref[g * pitch + r:(g + 1) * pitch, :] = pad

    def body(g, carry):
        n2 = g * FFT_GROUP
        dcat = jnp.concatenate(
            [jnp.concatenate([d_ref[0, pl.ds(n2 + k, kp, stride=pitch), :],
                              d_ref[1, pl.ds(n2 + k, kp, stride=pitch), :]], axis=0) for k in range(FFT_GROUP)],
            axis=1)
        y = _dot3(gh, gl, dcat)
        for k in range(FFT_GROUP):
            o_ref[pl.ds(n2 + k, r // 2, stride=pitch), :] = y[:, k * LANES:(k + 1) * LANES]
        return carry

    lax.fori_loop(0, r // FFT_GROUP, body, 0)


def _fft_stage_f(dmat, gfin, r, pitch):
    b, _, rows, c = dmat.shape
    kp = rows // pitch
    out_rows = r // 2 * pitch
    return pl.pallas_call(
        functools.partial(_fft_f_kernel, r=r, kp=kp, pitch=pitch),
        out_shape=_sds((b, out_rows, c), F32),
        grid=(b, c // LANES),
        in_specs=[pl.BlockSpec((None, 2, rows, LANES), lambda bi, ci: (bi, 0, 0, ci)),
                  pl.BlockSpec((r // 2, 2 * kp), lambda bi, ci: (0, 0)),
                  pl.BlockSpec((r // 2, 2 * kp), lambda bi, ci: (0, 0))],
        out_specs=pl.BlockSpec((None, out_rows, LANES), lambda bi, ci: (bi, 0, ci)),
        compiler_params=_params(("arbitrary", "arbitrary")),
        name="fft_stage_f",
    )(dmat, *gfin)


def _fft_radix(length):
    r = int(round(math.sqrt(2 * length)))
    assert r * r == 2 * length, "sequence length must make 2L a perfect square"
    return r, r + 8


def _long_conv(zin, hfb, r, pitch):
    fa_half, fbig, fconj, gfin, tw = _dft_tables(r)
    ks = _fft_filter_spectrum(_fft_stage_a(hfb, fa_half, r, pitch), tw, hfb[1, 0:1, :], fbig, r, pitch)
    a = _fft_stage_a(zin, fa_half, r, pitch)
    dmat = _fft_stage_m(a, ks, tw, fbig, fconj, r, pitch)
    return _fft_stage_f(dmat, gfin, r, pitch)


def _alibi_slopes(n_heads):
    def pow2_slopes(m):
        start = 2.0 ** (-8.0 / m)
        return [start ** (i + 1) for i in range(m)]
    base = 2 ** int(math.floor(math.log2(n_heads)))
    slopes = pow2_slopes(base)
    if base < n_heads:
        slopes = slopes + pow2_slopes(2 * base)[0::2][: n_heads - base]
    return np.array(sorted(slopes, reverse=True), dtype=np.float32)


def _attn_kernel(q_ref, k_ref, kp_ref, kn_ref, v_ref, vp_ref, vn_ref, o_ref, l_ref, *, tq, n, dil, slopes):
    i = pl.program_id(1)
    side = ATTN_SIDE
    nk = tq + 2 * side
    row = lax.broadcasted_iota(jnp.int32, (tq, nk), 0)
    col = lax.broadcasted_iota(jnp.int32, (tq, nk), 1)
    rel = jnp.abs(col - side - row)
    kglob = i * tq + col - side
    valid = (rel <= side) & (kglob >= 0) & (kglob < n)
    dist = (rel * dil).astype(F32)
    scale = HEAD_DIM ** -0.5
    for h in range(HEADS_PER_GROUP):
        hs = slice(h * HEAD_DIM, (h + 1) * HEAD_DIM)
        q = q_ref[:, hs]
        kc = jnp.concatenate([kp_ref[:, hs], k_ref[:, hs], kn_ref[:, hs]], axis=0)
        vc = jnp.concatenate([vp_ref[:, hs], v_ref[:, hs], vn_ref[:, hs]], axis=0)
        s = lax.dot_general(q, kc, (((1,), (1,)), ((), ())), preferred_element_type=F32) * scale
        s = jnp.where(valid, s - float(slopes[h]) * dist, NEG_INF)
        m = jnp.max(s, axis=-1, keepdims=True)
        p = jnp.exp(s - m)
        den = jnp.sum(p, axis=-1, keepdims=True)
        o = jnp.dot(p.astype(BF16), vc, preferred_element_type=F32) / den
        o_ref[:, hs] = o.astype(o_ref.dtype)
        l_ref[:, hs] = jnp.broadcast_to(m + jnp.log(den), (tq, HEAD_DIM))


def _dilated_attention(qkv, n, dil, slopes, col0=0):
    streams = qkv.shape[0]
    tq = min(256, n)
    side = ATTN_SIDE
    nh = n // side
    gw = GROUP_WIDTH
    main = lambda cb: pl.BlockSpec((None, tq, gw), lambda s, i: (s, i, col0 + cb))
    prev = lambda cb: pl.BlockSpec((None, side, gw),
                                   lambda s, i: (s, jnp.maximum(i * (tq // side) - 1, 0), col0 + cb))
    nxt = lambda cb: pl.BlockSpec((None, side, gw),
                                  lambda s, i: (s, jnp.minimum((i + 1) * (tq // side), nh - 1), col0 + cb))
    return pl.pallas_call(
        functools.partial(_attn_kernel, tq=tq, n=n, dil=dil, slopes=tuple(float(v) for v in slopes)),
        out_shape=(_sds((streams, n, gw), BF16), _sds((streams, n, gw), F32)),
        grid=(streams, n // tq),
        in_specs=[main(0), main(1), prev(1), nxt(1), main(2), prev(2), nxt(2)],
        out_specs=(pl.BlockSpec((None, tq, gw), lambda s, i: (s, i, 0)),
                   pl.BlockSpec((None, tq, gw), lambda s, i: (s, i, 0))),
        compiler_params=_params(("arbitrary", "arbitrary")),
        name=f"dilated_attention_d{dil}",
    )(qkv, qkv, qkv, qkv, qkv, qkv, qkv)


def _merge_kernel(y_ref, z_ref, x1_ref, skip_ref, o0_ref, l0_ref, o1_ref, l1_ref, o2_ref, l2_ref,
                  ghy_ref, gat_ref, wh_ref, wa_ref, out_ref, hy_s, at_s, so1, sl1, so2, sl2, *, tm, tn, r, pitch):
    nct = GROUP_WIDTH // LANES
    for g in range(tm // r):
        rows = slice(g * pitch, g * pitch + r)
        hy_s[g * r:(g + 1) * r, :] = ((y_ref[rows, :] + z_ref[rows, :] * skip_ref[...])
                                      * x1_ref[g * r:(g + 1) * r, :]).astype(BF16)
    for dil, oref, lref, so, sl in ((4, o1_ref, l1_ref, so1, sl1), (16, o2_ref, l2_ref, so2, sl2)):
        for res in range(dil):
            for c in range(nct):
                cs = slice(c * LANES, (c + 1) * LANES)
                so[c, pl.ds(res, tm // dil, stride=dil), :] = oref[res, :, cs].astype(F32)
                sl[c, pl.ds(res, tm // dil, stride=dil), :] = lref[res, :, cs]
    for c in range(nct):
        cs = slice(c * LANES, (c + 1) * LANES)
        a0, a1, a2 = l0_ref[:, cs], sl1[c], sl2[c]
        m = jnp.maximum(jnp.maximum(a0, a1), a2)
        e0, e1, e2 = jnp.exp(a0 - m), jnp.exp(a1 - m), jnp.exp(a2 - m)
        at = (e0 * o0_ref[:, cs].astype(F32) + e1 * so1[c] + e2 * so2[c]) / (e0 + e1 + e2)
        at_s[:, cs] = at.astype(BF16)

    hy, at = hy_s[...], at_s[...]
    for j in range(out_ref.shape[1] // tn):
        cols = slice(j * tn, (j + 1) * tn)
        acc_h = jnp.dot(hy, wh_ref[:, cols], preferred_element_type=F32)
        acc_a = jnp.dot(at, wa_ref[:, cols], preferred_element_type=F32)
        out = (jax.nn.sigmoid(ghy_ref[:, cols].astype(F32)) * acc_h
               + jax.nn.sigmoid(gat_ref[:, cols].astype(F32)) * acc_a)
        out_ref[:, cols] = out.astype(out_ref.dtype)


def _merge(yconv, zin, x1c, skip, o0, l0, o1, l1, o2, l2, gates, wh, wa, seq, r, pitch):
    n_tok, hw = x1c.shape
    d_model = wh.shape[1]
    gw = GROUP_WIDTH
    tm = 256
    spb = seq // tm
    row = lambda width: pl.BlockSpec((tm, width), lambda i: (i, 0))
    prow = pl.BlockSpec((tm // r * pitch, hw), lambda i: (i, 0))
    res = lambda dil: pl.BlockSpec((None, dil, tm // dil, gw), lambda i: (i // spb, 0, i % spb, 0))
    resident = lambda rows: pl.BlockSpec((rows, d_model), lambda i: (0, 0), pipeline_mode=pl.Buffered(1))
    return pl.pallas_call(
        functools.partial(_merge_kernel, tm=tm, tn=COL_BLOCK, r=r, pitch=pitch),
        out_shape=_sds((n_tok, d_model), BF16),
        grid=(n_tok // tm,),
        in_specs=[prow, prow, row(hw), pl.BlockSpec((1, hw), lambda i: (0, 0)),
                  row(gw), row(gw), res(4), res(4), res(16), res(16),
                  pl.BlockSpec((tm, d_model), lambda i: (i, 0)),
                  pl.BlockSpec((tm, d_model), lambda i: (i, 1)),
                  resident(hw), resident(gw)],
        out_specs=pl.BlockSpec((tm, d_model), lambda i: (i, 0)),
        scratch_shapes=[pltpu.VMEM((tm, hw), BF16), pltpu.VMEM((tm, gw), BF16)]
                       + [pltpu.VMEM((gw // LANES, tm, LANES), F32)] * 4,
        compiler_params=_params(("arbitrary",)),
        name="gated_merge",
    )(yconv, zin, x1c, skip, o0, l0, o1, l1, o2, l2, gates, gates, wh, wa)


def _router_kernel(mo_ref, x_ref, gm_ref, gpost_ref, gpre_ref, sc_ref, sh_ref, wr_ref, br_ref,
                   x1_ref, h2_ref, idx_ref, tw_ref, *, ts, d_model, n_experts):
    mo = mo_ref[...]
    y = mo * lax.rsqrt(jnp.mean(mo * mo, axis=-1, keepdims=True) + RMS_EPS) * gpost_ref[...]
    x1 = x_ref[...] + gm_ref[...] * y
    x1_ref[...] = x1
    h2 = (x1 * lax.rsqrt(jnp.mean(x1 * x1, axis=-1, keepdims=True) + RMS_EPS) * gpre_ref[...]
          * (1.0 + sc_ref[...]) + sh_ref[...])
    nct = d_model // 2 // LANES
    sp = nct + SLAB_PAD
    words = _pack_bf16_pairs(h2)
    for c in range(nct):
        h2_ref[pl.ds(c, ts, stride=sp), :] = words[:, c * LANES:(c + 1) * LANES]
    for c in range(nct, sp):
        h2_ref[pl.ds(c, ts, stride=sp), :] = jnp.zeros((ts, LANES), jnp.uint32)
    logits = jnp.dot(h2, wr_ref[...], precision=HIGHEST, preferred_element_type=F32) + br_ref[...]
    lane = lax.broadcasted_iota(jnp.int32, logits.shape, 1)
    lane_f = lane.astype(F32)
    logits = jnp.where(lane < n_experts, logits, -jnp.inf)
    idx_out = jnp.zeros(logits.shape, jnp.int32)
    val_out = jnp.zeros(logits.shape, F32)
    top0 = None
    den = None
    for k in range(TOP_K):
        m = jnp.max(logits, axis=-1, keepdims=True)
        idx = jnp.min(jnp.where(logits == m, lane_f, float(LANES)), axis=-1, keepdims=True).astype(jnp.int32)
        if k == 0:
            top0 = m
        e = jnp.exp(m - top0)
        den = e if den is None else den + e
        idx_out = jnp.where(lane == k, idx, idx_out)
        val_out = jnp.where(lane == k, e, val_out)
        logits = jnp.where(lane == idx, -jnp.inf, logits)
    idx_ref[...] = idx_out
    tw_ref[...] = val_out / den


def _post_mix_and_route(mo, x, gate_m, g_post, g_pre, scale_f, shift_f, w_router, b_router):
    b, s, d = x.shape
    e = w_router.shape[1]
    ts = 256
    wr = jnp.zeros((d, LANES), F32).at[:, :e].set(w_router)
    br = jnp.zeros((1, LANES), F32).at[0, :e].set(b_router)
    sp = d // 2 // LANES + SLAB_PAD
    spb = s // ts
    rowblk = lambda width: pl.BlockSpec((ts, width), lambda i: (i, 0))
    per_batch = pl.BlockSpec((None, 1, d), lambda i: (i // spb, 0, 0))
    vec = pl.BlockSpec((1, d), lambda i: (0, 0))
    n_tok = b * s
    return pl.pallas_call(
        functools.partial(_router_kernel, ts=ts, d_model=d, n_experts=e),
        out_shape=(_sds((n_tok, d), F32), _sds((n_tok * sp, LANES), jnp.uint32),
                   _sds((n_tok, LANES), jnp.int32), _sds((n_tok, LANES), F32)),
        grid=(n_tok // ts,),
        in_specs=[rowblk(d), rowblk(d), per_batch, vec, vec, per_batch, per_batch,
                  pl.BlockSpec((d, LANES), lambda i: (0, 0)), pl.BlockSpec((1, LANES), lambda i: (0, 0))],
        out_specs=(rowblk(d), pl.BlockSpec((ts * sp, LANES), lambda i: (i, 0)), rowblk(LANES), rowblk(LANES)),
        compiler_params=_params(("arbitrary",)),
        name="post_mix_route",
    )(mo, x.reshape(n_tok, d), gate_m, g_post, g_pre, scale_f, shift_f, wr, br)


W_CHUNKS = 8


def _expert_up_kernel(be_ref, nu_ref, sch_ref, tok0_ref, tok1_ref, h2_hbm, wg_hbm, bg_ref, wu_hbm, bu_ref, act_ref,
                      xbuf, xb, wbuf, stage, sem, wsem, *, tb, nct, rc):
    i = pl.program_id(0)
    n_used = nu_ref[0]
    sp = nct + SLAB_PAD

    def row_copy(t, j, slot):
        return pltpu.make_async_copy(h2_hbm.at[pl.ds(pl.multiple_of(t * sp, 8), nct), :],
                                     xbuf.at[slot, pl.ds(pl.multiple_of(j * sp, 8), nct), :], sem.at[slot])

    def gather(tok_ref, slot):
        def body(j, carry):
            row_copy(tok_ref[0, j], j, slot).start()
            return carry
        lax.fori_loop(0, tb, body, 0, unroll=8)

    def wait(slot):
        pltpu.make_async_copy(h2_hbm.at[pl.ds(0, tb * nct), :], xbuf.at[slot, pl.ds(0, tb * nct), :],
                              sem.at[slot]).wait()

    def chunk_copies(e, c, st):
        rows = pl.ds(pl.multiple_of(c * rc, rc), rc)
        return (pltpu.make_async_copy(wg_hbm.at[e, rows, :], stage.at[st, 0], wsem.at[st]),
                pltpu.make_async_copy(wu_hbm.at[e, rows, :], stage.at[st, 1], wsem.at[st]))

    def chunk_start(e, c, st):
        for cp in chunk_copies(e, c, st):
            cp.start()

    def finish_chunks(e, lo, hi, half):
        def body(c, carry):
            st = c % 2
            for cp in chunk_copies(e, c, st):
                cp.wait()
            rows = pl.ds(pl.multiple_of(c * rc, rc), rc)
            wbuf[2 * half, rows, :] = stage[st, 0].astype(BF16)
            wbuf[2 * half + 1, rows, :] = stage[st, 1].astype(BF16)

            @pl.when(c + 1 < W_CHUNKS)
            def _():
                chunk_start(e, c + 1, 1 - st)
            return carry
        lax.fori_loop(lo, hi, body, 0)

    half = sch_ref[5 * i]

    @pl.when((i == 0) & (n_used > 0))
    def _():
        gather(tok0_ref, 0)
        chunk_start(be_ref[0], 0, 0)
        finish_chunks(be_ref[0], 0, W_CHUNKS, half)

    @pl.when(i < n_used)
    def _():
        nxt = sch_ref[5 * i + 1]

        @pl.when(sch_ref[5 * i + 2] == 1)
        def _():
            chunk_start(nxt, 0, 0)

        finish_chunks(nxt, sch_ref[5 * i + 3], sch_ref[5 * i + 4], 1 - half)

    @pl.when(i < n_used)
    def _():
        slot = i % 2
        wait(slot)
        for c in range(nct):
            hi, lo = _unpack_bf16_pairs(xbuf[slot, pl.ds(c, tb, stride=sp), :])
            xb[:, 2 * c * LANES:(2 * c + 1) * LANES] = hi.astype(BF16)
            xb[:, (2 * c + 1) * LANES:(2 * c + 2) * LANES] = lo.astype(BF16)
        for j in range(tb):
            row_copy(tok1_ref[0, j], j, 1 - slot).start()
        x = xb[...]
        g = jnp.dot(x, wbuf[2 * half], preferred_element_type=F32) + bg_ref[...]
        u = jnp.dot(x, wbuf[2 * half + 1], preferred_element_type=F32) + bu_ref[...]
        g = jnp.minimum(g, SWIGLU_LIMIT)
        u = jnp.clip(u, -SWIGLU_LIMIT, SWIGLU_LIMIT)
        act_ref[...] = (g * jax.nn.sigmoid(SWIGLU_ALPHA * g) * (u + 1.0)).astype(act_ref.dtype)

    @pl.when((i == n_used) & (n_used > 0))
    def _():
        wait(i % 2)

    @pl.when(i >= n_used)
    def _():
        act_ref[...] = jnp.zeros(act_ref.shape, act_ref.dtype)


def _expert_up(block_e, n_used, sched, row_tok3, h2s, wg, bg, wu, bu, nct):
    n_blocks, _, tb = row_tok3.shape
    _, d, f = wg.shape
    rc = d // W_CHUNKS
    grid_spec = pltpu.PrefetchScalarGridSpec(
        num_scalar_prefetch=3,
        grid=(n_blocks,),
        in_specs=[pl.BlockSpec((None, 1, tb), lambda i, be, nu, sc: (i, 0, 0), memory_space=pltpu.SMEM),
                  pl.BlockSpec((None, 1, tb), lambda i, be, nu, sc: (jnp.minimum(i + 1, n_blocks - 1), 0, 0),
                               memory_space=pltpu.SMEM),
                  pl.BlockSpec(memory_space=pl.ANY),
                  pl.BlockSpec(memory_space=pl.ANY),
                  pl.BlockSpec((None, 1, f), lambda i, be, nu, sc: (be[i], 0, 0)),
                  pl.BlockSpec(memory_space=pl.ANY),
                  pl.BlockSpec((None, 1, f), lambda i, be, nu, sc: (be[i], 0, 0))],
        out_specs=pl.BlockSpec((tb, f), lambda i, be, nu, sc: (i, 0)),
        scratch_shapes=[pltpu.VMEM((2, tb * (nct + SLAB_PAD), LANES), jnp.uint32), pltpu.VMEM((tb, d), BF16),
                        pltpu.VMEM((4, d, f), BF16), pltpu.VMEM((2, 2, rc, f), F32),
                        pltpu.SemaphoreType.DMA((2,)), pltpu.SemaphoreType.DMA((2,))],
    )
    return pl.pallas_call(
        functools.partial(_expert_up_kernel, tb=tb, nct=nct, rc=rc),
        out_shape=_sds((n_blocks * tb, f), BF16),
        grid_spec=grid_spec,
        compiler_params=_params(("arbitrary",)),
        name="expert_up",
    )(block_e, n_used, sched, row_tok3, row_tok3, h2s, wg, bg, wu, bu)


def _expert_down_kernel(be_ref, nu_ref, act_ref, wd_ref, bd_ref, ys_ref, wb, *, tb, nct):
    i = pl.program_id(0)

    @pl.when((i < nu_ref[0]) & ((i == 0) | (be_ref[i] != be_ref[jnp.maximum(i - 1, 0)])))
    def _():
        wb[...] = wd_ref[...].astype(BF16)

    @pl.when(i < nu_ref[0])
    def _():
        act = act_ref[...]
        sp = nct + SLAB_PAD
        for c in range(nct):
            cols = slice(2 * c * LANES, (2 * c + 2) * LANES)
            y = jnp.dot(act, wb[:, cols], preferred_element_type=F32) + bd_ref[:, cols]
            ys_ref[pl.ds(c, tb, stride=sp), :] = _pack_bf16_pairs(y)
        for c in range(nct, sp):
            ys_ref[pl.ds(c, tb, stride=sp), :] = jnp.zeros((tb, LANES), jnp.uint32)

    @pl.when(i >= nu_ref[0])
    def _():
        ys_ref[...] = jnp.zeros(ys_ref.shape, ys_ref.dtype)


def _expert_down(block_e, n_used, act, wd, bd, tb):
    _, f, d = wd.shape
    n_blocks = act.shape[0] // tb
    nct = d // 2 // LANES
    grid_spec = pltpu.PrefetchScalarGridSpec(
        num_scalar_prefetch=2,
        grid=(n_blocks,),
        in_specs=[pl.BlockSpec((tb, f), lambda i, be, nu: (i, 0)),
                  pl.BlockSpec((None, f, d), lambda i, be, nu: (be[i], 0, 0)),
                  pl.BlockSpec((None, 1, d), lambda i, be, nu: (be[i], 0, 0))],
        out_specs=pl.BlockSpec((tb * (nct + SLAB_PAD), LANES), lambda i, be, nu: (i, 0)),
        scratch_shapes=[pltpu.VMEM((f, d), BF16)],
    )
    return pl.pallas_call(
        functools.partial(_expert_down_kernel, tb=tb, nct=nct),
        out_shape=_sds((n_blocks * tb * (nct + SLAB_PAD), LANES), jnp.uint32),
        grid_spec=grid_spec,
        compiler_params=_params(("arbitrary",)),
        name="expert_down",
    )(block_e, n_used, act, wd, bd)


def _combine_kernel(d0_ref, d1_ref, ys_hbm, tw_ref, x1_ref, gf_ref, gpost_ref, o_ref, buf, ff, sem, *, tc, nct):
    i = pl.program_id(0)
    last = pl.num_programs(0) - 1
    n_rows = TOP_K * tc
    sp = nct + SLAB_PAD

    def row_copy(r, j, slot):
        return pltpu.make_async_copy(ys_hbm.at[pl.ds(pl.multiple_of(r * sp, 8), nct), :],
                                     buf.at[slot, pl.ds(pl.multiple_of(j * sp, 8), nct), :], sem.at[slot])

    def gather(dref, slot):
        def body(j, carry):
            row_copy(dref[0, j], j, slot).start()
            return carry
        lax.fori_loop(0, n_rows, body, 0, unroll=8)

    def wait(slot):
        pltpu.make_async_copy(ys_hbm.at[pl.ds(0, n_rows * nct), :], buf.at[slot, pl.ds(0, n_rows * nct), :],
                              sem.at[slot]).wait()

    @pl.when(i == 0)
    def _():
        gather(d0_ref, 0)

    slot = i % 2
    wait(slot)
    for j in range(n_rows):
        row_copy(d1_ref[0, j], j, 1 - slot).start()
    wk =[tw_ref[:, k:k + 1] for k in range(TOP_K)]
    for c in range(nct):
        acc_hi = acc_lo = None
        for k in range(TOP_K):
            hi, lo = _unpack_bf16_pairs(buf[slot, pl.ds(k * tc * sp + c, tc, stride=sp), :])
            acc_hi = wk[k] * hi if acc_hi is None else acc_hi + wk[k] * hi
            acc_lo = wk[k] * lo if acc_lo is None else acc_lo + wk[k] * lo
        ff[:, 2 * c * LANES:(2 * c + 1) * LANES] = acc_hi
        ff[:, (2 * c + 1) * LANES:(2 * c + 2) * LANES] = acc_lo
    f = ff[...]
    y = f * lax.rsqrt(jnp.mean(f * f, axis=-1, keepdims=True) + RMS_EPS) * gpost_ref[...]
    o_ref[...] = x1_ref[...] + gf_ref[...] * y

    @pl.when(i == last)
    def _():
        wait(1 - slot)


def _combine(dest3, ys, top_w, x1, gate_f, g_post, seq):
    n_tok, d = x1.shape
    n_steps, _, n_rows = dest3.shape
    tc = n_rows // TOP_K
    nct = d // 2 // LANES
    spb = seq // tc
    return pl.pallas_call(
        functools.partial(_combine_kernel, tc=tc, nct=nct),
        out_shape=_sds((n_tok, d), F32),
        grid=(n_steps,),
        in_specs=[pl.BlockSpec((None, 1, n_rows), lambda i: (i, 0, 0), memory_space=pltpu.SMEM),
                  pl.BlockSpec((None, 1, n_rows), lambda i: (jnp.minimum(i + 1, n_steps - 1), 0, 0),
                               memory_space=pltpu.SMEM),
                  pl.BlockSpec(memory_space=pl.ANY),
                  pl.BlockSpec((tc, LANES), lambda i: (i, 0)),
                  pl.BlockSpec((tc, d), lambda i: (i, 0)),
                  pl.BlockSpec((None, 1, d), lambda i: (i // spb, 0, 0)),
                  pl.BlockSpec((1, d), lambda i: (0, 0))],
        out_specs=pl.BlockSpec((tc, d), lambda i: (i, 0)),
        scratch_shapes=[pltpu.VMEM((2, n_rows * (nct + SLAB_PAD), LANES), jnp.uint32), pltpu.VMEM((tc, d), F32),
                        pltpu.SemaphoreType.DMA((2,))],
        compiler_params=_params(("arbitrary",)),
        name="expert_combine",
    )(dest3, dest3, ys, top_w, x1, gate_f, g_post)


def _rank_kernel(idx_ref, tri_ref, rank_ref, cnt_ref, carry, *, tr):
    @pl.when(pl.program_id(0) == 0)
    def _():
        carry[...] = jnp.zeros(carry.shape, F32)

    idx = idx_ref[...]
    lane = lax.broadcasted_iota(jnp.int32, idx.shape, 1)
    base = carry[0:1, :]
    out = jnp.zeros(idx.shape, jnp.int32)
    for k in range(TOP_K):
        onehot = jnp.where(lane == idx[:, k:k + 1], 1.0, 0.0)
        csum = jnp.dot(tri_ref[...], onehot.astype(BF16), preferred_element_type=F32)
        rank = jnp.sum(onehot * (csum + base), axis=-1, keepdims=True) - 1.0
        out = jnp.where(lane == k, rank.astype(jnp.int32), out)
        base = base + csum[tr - 1:tr, :]
    rank_ref[...] = out
    carry[...] = jnp.broadcast_to(base, carry.shape)
    cnt_ref[...] = jnp.broadcast_to(base, cnt_ref.shape).astype(jnp.int32)


def _expert_ranks(top_idx_padded):
    n_tok = top_idx_padded.shape[0]
    tr = 512
    tri = jnp.asarray(np.tril(np.ones((tr, tr), np.float32)), BF16)
    return pl.pallas_call(
        functools.partial(_rank_kernel, tr=tr),
        out_shape=(_sds((n_tok, LANES), jnp.int32), _sds((8, LANES), jnp.int32)),
        grid=(n_tok // tr,),
        in_specs=[pl.BlockSpec((tr, LANES), lambda i: (i, 0)), pl.BlockSpec((tr, tr), lambda i: (0, 0))],
        out_specs=(pl.BlockSpec((tr, LANES), lambda i: (i, 0)), pl.BlockSpec((8, LANES), lambda i: (0, 0))),
        scratch_shapes=[pltpu.VMEM((8, LANES), F32)],
        compiler_params=_params(("arbitrary",)),
        name="expert_ranks",
    )(top_idx_padded, tri)


def _routing_tables(top_idx_padded, n_experts, tb):
    n_tok = top_idx_padded.shape[0]
    n_assign = n_tok * TOP_K
    ranks, counts = _expert_ranks(top_idx_padded)
    sizes = counts[0, :n_experts]
    padded = (sizes + tb - 1) // tb * tb
    pad_end = jnp.cumsum(padded)
    pad_start = pad_end - padded
    top_idx = top_idx_padded[:, :TOP_K]
    experts = jnp.arange(n_experts, dtype=jnp.int32)
    start_of = jnp.sum(jnp.where(top_idx[:, :, None] == experts, pad_start, 0), axis=-1)
    dest = (start_of + ranks[:, :TOP_K]).astype(jnp.int32).reshape(-1)
    n_rows = -(-n_assign // tb) * tb + n_experts * tb
    n_blocks = n_rows // tb
    tok = (jnp.arange(n_assign, dtype=jnp.int32) // TOP_K)
    row_tok = jnp.zeros((n_rows,), jnp.int32).at[dest].set(tok)
    block_start = jnp.arange(n_blocks, dtype=jnp.int32) * tb
    block_e = jnp.minimum(jnp.sum((pad_end[None, :] <= block_start[:, None]).astype(jnp.int32), axis=1),
                          n_experts - 1).astype(jnp.int32)
    n_used = (pad_end[-1] // tb).astype(jnp.int32).reshape(1)
    nb = padded // tb
    occupied = nb > 0
    ordinal = jnp.cumsum(occupied.astype(jnp.int32)) - 1
    later = lax.cummin(jnp.where(occupied, experts, n_experts), reverse=True)
    next_e = jnp.concatenate([later[1:], jnp.full((1,), n_experts, jnp.int32)])
    blk = jnp.arange(n_blocks, dtype=jnp.int32)
    k = blk - (pad_start // tb)[block_e]
    nbe = jnp.maximum(nb[block_e], 1)
    active = (blk < n_used[0]) & (next_e[block_e] < n_experts)
    zero = jnp.zeros_like(blk)
    sched = jnp.stack([ordinal[block_e] % 2,
                       jnp.where(active, next_e[block_e], zero),
                       jnp.where(active & (k == 0), 1, zero),
                       jnp.where(active, k * W_CHUNKS // nbe, zero),
                       jnp.where(active, (k + 1) * W_CHUNKS // nbe, zero)], axis=1).reshape(-1).astype(jnp.int32)
    return dest, row_tok, block_e, n_used, sched, n_rows


def _moe(h2s, top_idx, top_w, wg, bg, wu, bu, wd, bd, x1, gate_f, g_post, seq):
    n_tok, d = x1.shape
    n_experts = wg.shape[0]
    tb = MOE_ROWS
    nct = d // 2 // LANES
    dest, row_tok, block_e, n_used, sched, n_rows = _routing_tables(top_idx, n_experts, tb)
    n_blocks = n_rows // tb
    act = _expert_up(block_e, n_used, sched, row_tok.reshape(n_blocks, 1, tb), h2s, wg, bg[:, None, :], wu,
                     bu[:, None, :], nct)
    ys = _expert_down(block_e, n_used, act, wd, bd[:, None, :], tb)
    tc = 128
    dest3 = dest.reshape(n_tok // tc, tc, TOP_K).transpose(0, 2, 1).reshape(n_tok // tc, 1, TOP_K * tc)
    return _combine(dest3, ys, top_w, x1, gate_f, g_post, seq)


def _layer(x, c8, p):
    b, s, d = x.shape
    n_tok = b * s
    hw = p["hy_skip"].shape[0]
    gw = GROUP_WIDTH
    assert hw == COL_BLOCK and gw == COL_BLOCK and d % COL_BLOCK == 0

    mod = _adaln(c8, p["w_ada"], p["b_ada"][None])[:b]
    shift_m, scale_m, gate_m, shift_f, scale_f, gate_f = [m[:, None, :] for m in jnp.split(mod, 6, axis=-1)]

    h, h4, h16 = _prenorm_mix(x, p["g_pre_mix"][None], scale_m, shift_m)

    w_in = p["w_in"].astype(BF16)
    nd = d // COL_BLOCK
    pn = _matmul(h.reshape(n_tok, d), w_in, [0, 1, 2, 3, 6, 9], BF16, "in_proj_natural")
    gates = _matmul(h.reshape(n_tok, d), w_in, list(range(12, 12 + 2 * nd)), BF16, "in_proj_gates")
    qkv1 = _matmul(h4.reshape(n_tok, d), w_in, [4, 7, 10], BF16, "in_proj_dil4")
    qkv2 = _matmul(h16.reshape(n_tok, d), w_in, [5, 8, 11], BF16, "in_proj_dil16")

    r, pitch = _fft_radix(s)
    hfb = _hyena_filters(s, r, pitch, p["hy_f_w1"], p["hy_f_b1"], p["hy_f_w2"], p["hy_f_b2"], p["hy_f_w3"],
                         p["hy_f_b3"], p["hy_f_freq"], p["hy_f_wout"])
    zin, x1c = _shortconv(pn.reshape(b, s, -1), p["hy_conv_w"], p["hy_conv_b"], hw, r, pitch)
    yconv = _long_conv(zin, hfb, r, pitch)

    slopes = _alibi_slopes(N_GROUPS * HEADS_PER_GROUP).reshape(N_GROUPS, HEADS_PER_GROUP)
    o0, l0 = _dilated_attention(pn.reshape(b, s, -1), s, 1, slopes[0], col0=3)
    o1, l1 = _dilated_attention(qkv1.reshape(b * 4, s // 4, 3 * gw), s // 4, 4, slopes[1])
    o2, l2 = _dilated_attention(qkv2.reshape(b * 16, s // 16, 3 * gw), s // 16, 16, slopes[2])

    merged = _merge(yconv.reshape(-1, hw), zin.reshape(-1, hw), x1c.reshape(n_tok, hw), p["hy_skip"][None],
                    o0.reshape(n_tok, gw), l0.reshape(n_tok, gw),
                    o1.reshape(b, 4, s // 4, gw), l1.reshape(b, 4, s // 4, gw),
                    o2.reshape(b, 16, s // 16, gw), l2.reshape(b, 16, s // 16, gw),
                    gates, p["w_proj_hyena"].astype(BF16), p["w_proj_attn"].astype(BF16), s, r, pitch)
    mo = _matmul(merged, p["w_out"].astype(BF16), list(range(nd)), F32, "out_proj")

    x1, h2s, top_idx, top_w = _post_mix_and_route(mo, x, gate_m, p["g_post_mix"][None], p["g_pre_ffn"][None],
                                                  scale_f, shift_f, p["w_router"], p["b_router"])
    out = _moe(h2s, top_idx, top_w, p["w_gate"], p["b_gate"], p["w_up"], p["b_up"], p["w_down"], p["b_down"], x1, gate_f,
               p["g_post_ffn"][None], s)
    return out.reshape(b, s, d)


def kernel(x, c, w_ada, b_ada, g_pre_mix, g_post_mix, g_pre_ffn, g_post_ffn, w_in, hy_conv_w, hy_conv_b, hy_skip, hy_f_w1, hy_f_b1, hy_f_w2, hy_f_b2, hy_f_w3, hy_f_b3, hy_f_freq, hy_f_wout, w_proj_hyena, w_proj_attn, w_out, w_router, b_router, w_gate, b_gate, w_up, b_up, w_down, b_down):
    names = ("w_ada", "b_ada", "g_pre_mix", "g_post_mix", "g_pre_ffn", "g_post_ffn", "w_in", "hy_conv_w",
             "hy_conv_b", "hy_skip", "hy_f_w1", "hy_f_b1", "hy_f_w2", "hy_f_b2", "hy_f_w3", "hy_f_b3",
             "hy_f_freq", "hy_f_wout", "w_proj_hyena", "w_proj_attn", "w_out", "w_router", "b_router",
             "w_gate", "b_gate", "w_up", "b_up", "w_down", "b_down")
    stacked = (w_ada, b_ada, g_pre_mix, g_post_mix, g_pre_ffn, g_post_ffn, w_in, hy_conv_w, hy_conv_b, hy_skip,
               hy_f_w1, hy_f_b1, hy_f_w2, hy_f_b2, hy_f_w3, hy_f_b3, hy_f_freq, hy_f_wout, w_proj_hyena,
               w_proj_attn, w_out, w_router, b_router, w_gate, b_gate, w_up, b_up, w_down, b_down)
    depth = w_ada.shape[0]
    b = x.shape[0]
    c8 = jnp.zeros((8, c.shape[1]), F32).at[:b].set(c)
    for l in range(depth):
        x = _layer(x, c8, {k: v[l] for k, v in zip(names, stacked)})
    return x
```

```python
import functools
import math

import jax
import jax.numpy as jnp
import numpy as np
from jax import lax
from jax.experimental import pallas as pl
from jax.experimental.pallas import tpu as pltpu

F32 = jnp.float32
BF16 = jnp.bfloat16
HIGHEST = lax.Precision.HIGHEST

LANES = 128
HEAD_DIM = 128
HEADS_PER_GROUP = 8
DILATED_GROUPS = ((128, 1), (512, 4), (2048, 16))
N_GROUPS = len(DILATED_GROUPS)
GROUP_WIDTH = HEADS_PER_GROUP * HEAD_DIM
ATTN_SIDE = 64
TOP_K = 4
SWIGLU_LIMIT = 7.0
SWIGLU_ALPHA = 1.702
RMS_EPS = 1e-6
NEG_INF = -1e30
HYENA_N_BANDS = 16
HYENA_DECAY_TARGET = 1e-2
HYENA_FAST_DECAY_PCT = 0.3
HYENA_SLOW_DECAY_PCT = 1.5
COL_BLOCK = 1024
MOE_ROWS = 256
SLAB_PAD = 8
VMEM_LIMIT = 56 * 1024 * 1024


def _pack_bf16_pairs(x):
    bits = lax.bitcast_convert_type(x.astype(BF16).astype(F32), jnp.uint32)
    groups = [bits[:, j:j + LANES] | (bits[:, j + LANES:j + 2 * LANES] >> 16)
              for j in range(0, x.shape[1], 2 * LANES)]
    return groups[0] if len(groups) == 1 else jnp.concatenate(groups, axis=1)


def _unpack_bf16_pairs(w):
    hi = lax.bitcast_convert_type(w & jnp.uint32(0xFFFF0000), F32)
    lo = lax.bitcast_convert_type(w << 16, F32)
    return hi, lo


def _params(sem, vmem=VMEM_LIMIT):
    return pltpu.CompilerParams(dimension_semantics=sem, vmem_limit_bytes=vmem)


def _sds(shape, dtype):
    return jax.ShapeDtypeStruct(shape, dtype)


def _ada_kernel(c_ref, w_ref, b_ref, o_ref):
    c = c_ref[...]
    sc = (c * jax.nn.sigmoid(c)).astype(BF16)
    o_ref[...] = jnp.dot(sc, w_ref[...].astype(BF16), preferred_element_type=F32) + b_ref[...]


def _adaln(c8, w_ada, b_ada):
    d, cols = w_ada.shape
    tn = 512
    return pl.pallas_call(
        _ada_kernel,
        out_shape=_sds((8, cols), F32),
        grid=(cols // tn,),
        in_specs=[pl.BlockSpec((8, d), lambda j: (0, 0)),
                  pl.BlockSpec((d, tn), lambda j: (0, j)),
                  pl.BlockSpec((1, tn), lambda j: (0, j))],
        out_specs=pl.BlockSpec((8, tn), lambda j: (0, j)),
        compiler_params=_params(("arbitrary",)),
        name="adaln",
    )(c8, w_ada, b_ada)


def _prenorm_kernel(x_ref, g_ref, sc_ref, sh_ref, o_ref, o4_ref, o16_ref, scr_ref, *, ts, d_model):
    x = x_ref[...]
    ms = jnp.mean(x * x, axis=-1, keepdims=True)
    h = x * lax.rsqrt(ms + RMS_EPS) * g_ref[...] * (1.0 + sc_ref[...]) + sh_ref[...]
    o_ref[...] = h.astype(BF16)
    nct = d_model // LANES
    for c in range(nct):
        scr_ref[c] = h[:, c * LANES:(c + 1) * LANES]
    for dil, oref in ((4, o4_ref), (16, o16_ref)):
        for r in range(dil):
            for c in range(nct):
                oref[r, :, c * LANES:(c + 1) * LANES] = scr_ref[c, pl.ds(r, ts // dil, stride=dil), :].astype(BF16)


def _prenorm_mix(x, g, scale, shift):
    b, s, d = x.shape
    ts = 256
    kern = functools.partial(_prenorm_kernel, ts=ts, d_model=d)
    return pl.pallas_call(
        kern,
        out_shape=(_sds((b, s, d), BF16), _sds((b, 4, s // 4, d), BF16), _sds((b, 16, s // 16, d), BF16)),
        grid=(b, s // ts),
        in_specs=[pl.BlockSpec((None, ts, d), lambda bi, i: (bi, i, 0)),
                  pl.BlockSpec((1, d), lambda bi, i: (0, 0)),
                  pl.BlockSpec((None, 1, d), lambda bi, i: (bi, 0, 0)),
                  pl.BlockSpec((None, 1, d), lambda bi, i: (bi, 0, 0))],
        out_specs=(pl.BlockSpec((None, ts, d), lambda bi, i: (bi, i, 0)),
                   pl.BlockSpec((None, 4, ts // 4, d), lambda bi, i: (bi, 0, i, 0)),
                   pl.BlockSpec((None, 16, ts // 16, d), lambda bi, i: (bi, 0, i, 0))),
        scratch_shapes=[pltpu.VMEM((d // LANES, ts, LANES), F32)],
        compiler_params=_params(("arbitrary", "arbitrary")),
        name="prenorm_mix",
    )(x, g, scale, shift)


def _mm_kernel(tbl_ref, a_ref, w_ref, o_ref):
    del tbl_ref
    o_ref[...] = jnp.dot(a_ref[...], w_ref[...], preferred_element_type=F32).astype(o_ref.dtype)


def _matmul(a, w, col_blocks, out_dtype, name, tm=1024, tn=COL_BLOCK):
    m, k = a.shape
    tm = min(tm, m)
    nb = len(col_blocks)
    tbl = jnp.asarray(col_blocks, jnp.int32)
    grid_spec = pltpu.PrefetchScalarGridSpec(
        num_scalar_prefetch=1,
        grid=(m // tm, nb),
        in_specs=[pl.BlockSpec((tm, k), lambda i, j, t: (i, 0)),
                  pl.BlockSpec((k, tn), lambda i, j, t: (0, t[j]))],
        out_specs=pl.BlockSpec((tm, tn), lambda i, j, t: (i, j)),
    )
    return pl.pallas_call(
        _mm_kernel,
        out_shape=_sds((m, nb * tn), out_dtype),
        grid_spec=grid_spec,
        compiler_params=_params(("arbitrary", "arbitrary")),
        name=name,
    )(tbl, a, w)


def _filter_kernel(z_ref, w1_ref, b1_ref, w2_ref, b2_ref, w3_ref, b3_ref, fr_ref, wo_ref, dl_ref,
                   h_ref, *, hw, r, pitch):
    def dot(a, b):
        return jnp.dot(a, b, precision=HIGHEST, preferred_element_type=F32)

    z = z_ref[...]
    fr = fr_ref[...]
    h = jnp.sin(fr * (dot(z, w1_ref[...]) + b1_ref[...]))
    h = jnp.sin(fr * (dot(h, w2_ref[...]) + b2_ref[...]))
    h = jnp.sin(fr * (dot(h, w3_ref[...]) + b3_ref[...]))
    filt = dot(h, wo_ref[...])
    decay = jnp.exp(-z[:, 0:1] * dl_ref[...])
    hf = filt[:, :hw] * decay
    hb = filt[:, hw:] * decay
    pad = jnp.zeros((pitch - r, hw), F32)
    for g in range(z.shape[0] // r):
        h_ref[0, g * pitch:g * pitch + r, :] = hf[g * r:(g + 1) * r]
        h_ref[1, g * pitch:g * pitch + r, :] = hb[g * r:(g + 1) * r]
        h_ref[0, g * pitch + r:(g + 1) * pitch, :] = pad
        h_ref[1, g * pitch + r:(g + 1) * pitch, :] = pad


def _hyena_filters(length, r, pitch, w1, b1, w2, b2, w3, b3, freq, wout):
    emb, fw = w1.shape
    hw = wout.shape[1] // 2
    t = np.linspace(0.0, 1.0, length)[:, None]
    bands = np.linspace(1e-4, HYENA_N_BANDS - 1, HYENA_N_BANDS)[None, :]
    ang = (2.0 * math.pi / length) * np.arange(length)[:, None] * bands
    z = np.concatenate([t, np.cos(ang), -np.sin(ang)], axis=-1)
    zpad = np.zeros((length, LANES), np.float32)
    zpad[:, :emb] = z
    w1p = jnp.zeros((LANES, fw), F32).at[:emb].set(w1)
    min_decay = math.log(HYENA_DECAY_TARGET) / HYENA_FAST_DECAY_PCT
    max_decay = math.log(HYENA_DECAY_TARGET) / HYENA_SLOW_DECAY_PCT
    deltas = np.abs(np.linspace(min_decay, max_decay, hw))[None, :].astype(np.float32)
    tl = min(1024, length)
    full = lambda shape: pl.BlockSpec(shape, lambda i: (0,) * len(shape))
    return pl.pallas_call(
        functools.partial(_filter_kernel, hw=hw, r=r, pitch=pitch),
        out_shape=_sds((2, length // r * pitch, hw), F32),
        grid=(length // tl,),
        in_specs=[pl.BlockSpec((tl, LANES), lambda i: (i, 0)),
                  full((LANES, fw)), full((1, fw)), full((fw, fw)), full((1, fw)),
                  full((fw, fw)), full((1, fw)), full((1, fw)), full((fw, 2 * hw)), full((1, hw))],
        out_specs=pl.BlockSpec((2, tl // r * pitch, hw), lambda i: (0, i, 0)),
        compiler_params=_params(("arbitrary",)),
        name="hyena_filters",
    )(jnp.asarray(zpad), w1p, b1[None], w2, b2[None], w3, b3[None], freq[None], wout, jnp.asarray(deltas))


def _shortconv_kernel(u_ref, up_ref, un_ref, w_ref, b_ref, z_ref, x1_ref, *, ts, hw, r, pitch):
    i = pl.program_id(1)
    last = pl.num_programs(1) - 1
    u = u_ref[...].astype(F32)
    prev_blk = up_ref[...].astype(F32)
    next_blk = un_ref[...].astype(F32)
    prev_row = jnp.where(i > 0, prev_blk[15:16, :], 0.0)
    next_row = jnp.where(i < last, next_blk[0:1, :], 0.0)
    row = lax.broadcasted_iota(jnp.int32, u.shape, 0)
    um = jnp.where(row == 0, prev_row, pltpu.roll(u, 1, 0))
    up = jnp.where(row == ts - 1, next_row, pltpu.roll(u, ts - 1, 0))
    w = w_ref[...]
    uc = w[0:1] * um + w[1:2] * u + w[2:3] * up + b_ref[...]
    x1_ref[...] = uc[:, :hw]
    z = uc[:, 2 * hw:] * uc[:, hw:2 * hw]
    pad = jnp.zeros((pitch - r, hw), F32)
    for g in range(ts // r):
        z_ref[g * pitch:g * pitch + r, :] = z[g * r:(g + 1) * r]
        z_ref[g * pitch + r:(g + 1) * pitch, :] = pad


def _shortconv(pn3, conv_w, conv_b, hw, r, pitch):
    b, s, _ = pn3.shape
    ts = 512
    w3 = 3 * hw
    nh = s // 16
    return pl.pallas_call(
        functools.partial(_shortconv_kernel, ts=ts, hw=hw, r=r, pitch=pitch),
        out_shape=(_sds((b, s // r * pitch, hw), F32), _sds((b, s, hw), F32)),
        grid=(b, s // ts),
        in_specs=[pl.BlockSpec((None, ts, w3), lambda bi, i: (bi, i, 0)),
                  pl.BlockSpec((None, 16, w3), lambda bi, i: (bi, jnp.maximum(i * (ts // 16) - 1, 0), 0)),
                  pl.BlockSpec((None, 16, w3), lambda bi, i: (bi, jnp.minimum((i + 1) * (ts // 16), nh - 1), 0)),
                  pl.BlockSpec((3, w3), lambda bi, i: (0, 0)),
                  pl.BlockSpec((1, w3), lambda bi, i: (0, 0))],
        out_specs=(pl.BlockSpec((None, ts // r * pitch, hw), lambda bi, i: (bi, i, 0)),
                   pl.BlockSpec((None, ts, hw), lambda bi, i: (bi, i, 0))),
        compiler_params=_params(("arbitrary", "arbitrary")),
        name="hyena_shortconv",
    )(pn3, pn3, pn3, conv_w, conv_b[None])


FFT_GROUP = 4


def _split_bf16(a):
    hi = a.astype(BF16)
    return hi, (a - hi.astype(F32)).astype(BF16)


def _dot3(fh, fl, x):
    xh, xl = _split_bf16(x)
    d = lambda a, b: jnp.dot(a, b, preferred_element_type=F32)
    return d(fh, xh) + (d(fh, xl) + d(fl, xh))


def _fft_kept(r):
    return (r // 2 + 1 + 7) // 8 * 8


def _dft_tables(r):
    idx = np.arange(r)
    ang = 2.0 * np.pi * np.outer(idx, idx) / r
    cos, sin = np.cos(ang), np.sin(ang)
    kp = _fft_kept(r)
    fa_half = np.concatenate([cos[:kp, :r // 2], -sin[:kp, :r // 2]], axis=0)
    fbig = np.block([[cos, sin], [-sin, cos]])
    fconj = np.block([[cos, -sin], [sin, cos]])
    wgt = np.zeros(kp)
    wgt[0] = wgt[r // 2] = 1.0
    wgt[1:r // 2] = 2.0
    gfin = np.concatenate([cos[:r // 2, :kp] * wgt, -sin[:r // 2, :kp] * wgt], axis=1)
    tang = 2.0 * np.pi * np.outer(idx, idx) / (r * r)
    tw = np.stack([np.cos(tang), -np.sin(tang)], axis=-1)
    split = lambda a: _split_bf16(jnp.asarray(a.astype(np.float32)))
    return split(fa_half), split(fbig), split(fconj), split(gfin), jnp.asarray(tw.astype(np.float32))


def _fft_a_kernel(x_ref, fh_ref, fl_ref, o_ref, *, r, k1, kp, pitch):
    fh, fl = fh_ref[...], fl_ref[...]
    pad = jnp.zeros((pitch - r, LANES), F32)
    for g in range(kp):
        o_ref[0, g * pitch + r:(g + 1) * pitch, :] = pad
        o_ref[1, g * pitch + r:(g + 1) * pitch, :] = pad

    def body(g, carry):
        n2 = g * FFT_GROUP
        xs = jnp.concatenate([x_ref[pl.ds(n2 + k, k1, stride=pitch), :] for k in range(FFT_GROUP)], axis=1)
        a = _dot3(fh, fl, xs)
        for k in range(FFT_GROUP):
            o_ref[0, pl.ds(n2 + k, kp, stride=pitch), :] = a[:kp, k * LANES:(k + 1) * LANES]
            o_ref[1, pl.ds(n2 + k, kp, stride=pitch), :] = a[kp:, k * LANES:(k + 1) * LANES]
        return carry

    lax.fori_loop(0, r // FFT_GROUP, body, 0)


def _fft_stage_a(x, fmat, r, pitch):
    bx, rows, c = x.shape
    k1 = rows // pitch
    kp = _fft_kept(r)
    return pl.pallas_call(
        functools.partial(_fft_a_kernel, r=r, k1=k1, kp=kp, pitch=pitch),
        out_shape=_sds((bx, 2, kp * pitch, c), F32),
        grid=(bx, c // LANES),
        in_specs=[pl.BlockSpec((None, rows, LANES), lambda b, ci: (b, 0, ci)),
                  pl.BlockSpec((2 * kp, k1), lambda b, ci: (0, 0)),
                  pl.BlockSpec((2 * kp, k1), lambda b, ci: (0, 0))],
        out_specs=pl.BlockSpec((None, 2, kp * pitch, LANES), lambda b, ci: (b, 0, 0, ci)),
        compiler_params=_params(("arbitrary", "arbitrary")),
        name="fft_stage_a",
    )(x, *fmat)


def _twiddled(a_ref, b, tr, ti, r):
    are, aim = a_ref[b, 0, :r, :], a_ref[b, 1, :r, :]
    return jnp.concatenate([are * tr - aim * ti, are * ti + aim * tr], axis=0)


def _fft_mk_kernel(a_ref, tw_ref, h0_ref, fbh_ref, fbl_ref, o_ref, *, r, scale):
    tr, ti = tw_ref[:, 0:1], tw_ref[:, 1:2]
    fbh, fbl = fbh_ref[...], fbl_ref[...]
    xf = _dot3(fbh, fbl, _twiddled(a_ref, 0, tr, ti, r))
    xb = _dot3(fbh, fbl, _twiddled(a_ref, 1, tr, ti, r))
    o_ref[0] = (xf[:r] + xb[:r] - h0_ref[...]) * scale
    o_ref[1] = (xf[r:] - xb[r:]) * scale


def _fft_filter_spectrum(a, tw, h0, fbig, r, pitch):
    c = a.shape[-1]
    kp = _fft_kept(r)
    return pl.pallas_call(
        functools.partial(_fft_mk_kernel, r=r, scale=1.0 / (r * r)),
        out_shape=_sds((2, kp * r, c), F32),
        grid=(kp,),
        in_specs=[pl.BlockSpec((2, 2, pitch, c), lambda k: (0, 0, k, 0)),
                  pl.BlockSpec((None, r, 2), lambda k: (k, 0, 0)),
                  pl.BlockSpec((1, c), lambda k: (0, 0)),
                  pl.BlockSpec((2 * r, 2 * r), lambda k: (0, 0)),
                  pl.BlockSpec((2 * r, 2 * r), lambda k: (0, 0))],
        out_specs=pl.BlockSpec((2, r, c), lambda k: (0, k, 0)),
        compiler_params=_params(("arbitrary",)),
        name="fft_filter_spectrum",
    )(a, tw, h0, *fbig)


def _fft_m_kernel(a_ref, ks_ref, tw_ref, fbh_ref, fbl_ref, fch_ref, fcl_ref, o_ref, *, r):
    tr, ti = tw_ref[:, 0:1], tw_ref[:, 1:2]
    x = _dot3(fbh_ref[...], fbl_ref[...], _twiddled(a_ref, 0, tr, ti, r))
    xre, xim = x[:r], x[r:]
    kre, kim = ks_ref[0], ks_ref[1]
    c = _dot3(fch_ref[...], fcl_ref[...],
              jnp.concatenate([xre * kre - xim * kim, xre * kim + xim * kre], axis=0))
    cre, cim = c[:r], c[r:]
    o_ref[0, :r, :] = cre * tr + cim * ti
    o_ref[1, :r, :] = cim * tr - cre * ti
    o_ref[:, r:, :] = jnp.zeros((2,) + (o_ref.shape[1] - r, o_ref.shape[2]), F32)


def _fft_stage_m(a, ks, tw, fbig, fconj, r, pitch):
    b, _, rows, c = a.shape
    a5 = a.reshape(b, 1, 2, rows, c)
    mat = pl.BlockSpec((2 * r, 2 * r), lambda k, bi: (0, 0))
    return pl.pallas_call(
        functools.partial(_fft_m_kernel, r=r),
        out_shape=_sds(a.shape, F32),
        grid=(rows // pitch, b),
        in_specs=[pl.BlockSpec((None, 1, 2, pitch, c), lambda k, bi: (bi, 0, 0, k, 0)),
                  pl.BlockSpec((2, r, c), lambda k, bi: (0, k, 0)),
                  pl.BlockSpec((None, r, 2), lambda k, bi: (k, 0, 0)),
                  mat, mat, mat, mat],
        out_specs=pl.BlockSpec((None, 2, pitch, c), lambda k, bi: (bi, 0, k, 0)),
        compiler_params=_params(("arbitrary", "arbitrary")),
        name="fft_stage_m",
    )(a5, ks, tw, *fbig, *fconj)


def _fft_f_kernel(d_ref, gh_ref, gl_ref, o_ref, *, r, kp, pitch):
    gh, gl = gh_ref[...], gl_ref[...]
    pad = jnp.zeros((pitch - r, LANES), F32)
    for g in range(r // 2):
        o_ref[g * pitch + r:(g + 1) * pitch, :] = pad

    def body(g, carry):
        n2 = g * FFT_GROUP
        dcat = jnp.concatenate(
            [jnp.concatenate([d_ref[0, pl.ds(n2 + k, kp, stride=pitch), :],
                              d_ref[1, pl.ds(n2 + k, kp, stride=pitch), :]], axis=0) for k in range(FFT_GROUP)],
            axis=1)
        y = _dot3(gh, gl, dcat)
        for k in range(FFT_GROUP):
            o_ref[pl.ds(n2 + k, r // 2, stride=pitch), :] = y[:, k * LANES:(k + 1) * LANES]
        return carry

    lax.fori_loop(0, r // FFT_GROUP, body, 0)


def _fft_stage_f(dmat, gfin, r, pitch):
    b, _, rows, c = dmat.shape
    kp = rows // pitch
    out_rows = r // 2 * pitch
    return pl.pallas_call(
        functools.partial(_fft_f_kernel, r=r, kp=kp, pitch=pitch),
        out_shape=_sds((b, out_rows, c), F32),
        grid=(b, c // LANES),
        in_specs=[pl.BlockSpec((None, 2, rows, LANES), lambda bi, ci: (bi, 0, 0, ci)),
                  pl.BlockSpec((r // 2, 2 * kp), lambda bi, ci: (0, 0)),
                  pl.BlockSpec((r // 2, 2 * kp), lambda bi, ci: (0, 0))],
        out_specs=pl.BlockSpec((None, out_rows, LANES), lambda bi, ci: (bi, 0, ci)),
        compiler_params=_params(("arbitrary", "arbitrary")),
        name="fft_stage_f",
    )(dmat, *gfin)


def _fft_radix(length):
    r = int(round(math.sqrt(2 * length)))
    assert r * r == 2 * length, "sequence length must make 2L a perfect square"
    return r, r + 8


def _long_conv(zin, hfb, r, pitch):
    fa_half, fbig, fconj, gfin, tw = _dft_tables(r)
    ks = _fft_filter_spectrum(_fft_stage_a(hfb, fa_half, r, pitch), tw, hfb[1, 0:1, :], fbig, r, pitch)
    a = _fft_stage_a(zin, fa_half, r, pitch)
    dmat = _fft_stage_m(a, ks, tw, fbig, fconj, r, pitch)
    return _fft_stage_f(dmat, gfin, r, pitch)


def _alibi_slopes(n_heads):
    def pow2_slopes(m):
        start = 2.0 ** (-8.0 / m)
        return [start ** (i + 1) for i in range(m)]
    base = 2 ** int(math.floor(math.log2(n_heads)))
    slopes = pow2_slopes(base)
    if base < n_heads:
        slopes = slopes + pow2_slopes(2 * base)[0::2][: n_heads - base]
    return np.array(sorted(slopes, reverse=True), dtype=np.float32)


def _attn_kernel(q_ref, k_ref, kp_ref, kn_ref, v_ref, vp_ref, vn_ref, o_ref, l_ref, *, tq, n, dil, slopes):
    i = pl.program_id(1)
    side = ATTN_SIDE
    nk = tq + 2 * side
    row = lax.broadcasted_iota(jnp.int32, (tq, nk), 0)
    col = lax.broadcasted_iota(jnp.int32, (tq, nk), 1)
    rel = jnp.abs(col - side - row)
    kglob = i * tq + col - side
    valid = (rel <= side) & (kglob >= 0) & (kglob < n)
    dist = (rel * dil).astype(F32)
    scale = HEAD_DIM ** -0.5
    for h in range(HEADS_PER_GROUP):
        hs = slice(h * HEAD_DIM, (h + 1) * HEAD_DIM)
        q = q_ref[:, hs]
        kc = jnp.concatenate([kp_ref[:, hs], k_ref[:, hs], kn_ref[:, hs]], axis=0)
        vc = jnp.concatenate([vp_ref[:, hs], v_ref[:, hs], vn_ref[:, hs]], axis=0)
        s = lax.dot_general(q, kc, (((1,), (1,)), ((), ())), preferred_element_type=F32) * scale
        s = jnp.where(valid, s - float(slopes[h]) * dist, NEG_INF)
        m = jnp.max(s, axis=-1, keepdims=True)
        p = jnp.exp(s - m)
        den = jnp.sum(p, axis=-1, keepdims=True)
        o = jnp.dot(p.astype(BF16), vc, preferred_element_type=F32) / den
        o_ref[:, hs] = o.astype(o_ref.dtype)
        l_ref[:, hs] = jnp.broadcast_to(m + jnp.log(den), (tq, HEAD_DIM))


def _dilated_attention(qkv, n, dil, slopes, col0=0):
    streams = qkv.shape[0]
    tq = min(256, n)
    side = ATTN_SIDE
    nh = n // side
    gw = GROUP_WIDTH
    main = lambda cb: pl.BlockSpec((None, tq, gw), lambda s, i: (s, i, col0 + cb))
    prev = lambda cb: pl.BlockSpec((None, side, gw),
                                   lambda s, i: (s, jnp.maximum(i * (tq // side) - 1, 0), col0 + cb))
    nxt = lambda cb: pl.BlockSpec((None, side, gw),
                                  lambda s, i: (s, jnp.minimum((i + 1) * (tq // side), nh - 1), col0 + cb))
    return pl.pallas_call(
        functools.partial(_attn_kernel, tq=tq, n=n, dil=dil, slopes=tuple(float(v) for v in slopes)),
        out_shape=(_sds((streams, n, gw), BF16), _sds((streams, n, gw), F32)),
        grid=(streams, n // tq),
        in_specs=[main(0), main(1), prev(1), nxt(1), main(2), prev(2), nxt(2)],
        out_specs=(pl.BlockSpec((None, tq, gw), lambda s, i: (s, i, 0)),
                   pl.BlockSpec((None, tq, gw), lambda s, i: (s, i, 0))),
        compiler_params=_params(("arbitrary", "arbitrary")),
        name=f"dilated_attention_d{dil}",
    )(qkv, qkv, qkv, qkv, qkv, qkv, qkv)


def _merge_kernel(y_ref, z_ref, x1_ref, skip_ref, o0_ref, l0_ref, o1_ref, l1_ref, o2_ref, l2_ref,
                  ghy_ref, gat_ref, wh_ref, wa_ref, out_ref, hy_s, at_s, so1, sl1, so2, sl2, *, tm, tn, r, pitch):
    nct = GROUP_WIDTH // LANES
    for g in range(tm // r):
        rows = slice(g * pitch, g * pitch + r)
        hy_s[g * r:(g + 1) * r, :] = ((y_ref[rows, :] + z_ref[rows, :] * skip_ref[...])
                                      * x1_ref[g * r:(g + 1) * r, :]).astype(BF16)
    for dil, oref, lref, so, sl in ((4, o1_ref, l1_ref, so1, sl1), (16, o2_ref, l2_ref, so2, sl2)):
        for res in range(dil):
            for c in range(nct):
                cs = slice(c * LANES, (c + 1) * LANES)
                so[c, pl.ds(res, tm // dil, stride=dil), :] = oref[res, :, cs].astype(F32)
                sl[c, pl.ds(res, tm // dil, stride=dil), :] = lref[res, :, cs]
    for c in range(nct):
        cs = slice(c * LANES, (c + 1) * LANES)
        a0, a1, a2 = l0_ref[:, cs], sl1[c], sl2[c]
        m = jnp.maximum(jnp.maximum(a0, a1), a2)
        e0, e1, e2 = jnp.exp(a0 - m), jnp.exp(a1 - m), jnp.exp(a2 - m)
        at = (e0 * o0_ref[:, cs].astype(F32) + e1 * so1[c] + e2 * so2[c]) / (e0 + e1 + e2)
        at_s[:, cs] = at.astype(BF16)

    hy, at = hy_s[...], at_s[...]
    for j in range(out_ref.shape[1] // tn):
        cols = slice(j * tn, (j + 1) * tn)
        acc_h = jnp.dot(hy, wh_ref[:, cols], preferred_element_type=F32)
        acc_a = jnp.dot(at, wa_ref[:, cols], preferred_element_type=F32)
        out = (jax.nn.sigmoid(ghy_ref[:, cols].astype(F32)) * acc_h
               + jax.nn.sigmoid(gat_ref[:, cols].astype(F32)) * acc_a)
        out_ref[:, cols] = out.astype(out_ref.dtype)


def _merge(yconv, zin, x1c, skip, o0, l0, o1, l1, o2, l2, gates, wh, wa, seq, r, pitch):
    n_tok, hw = x1c.shape
    d_model = wh.shape[1]
    gw = GROUP_WIDTH
    tm = 256
    spb = seq // tm
    row = lambda width: pl.BlockSpec((tm, width), lambda i: (i, 0))
    prow = pl.BlockSpec((tm // r * pitch, hw), lambda i: (i, 0))
    res = lambda dil: pl.BlockSpec((None, dil, tm // dil, gw), lambda i: (i // spb, 0, i % spb, 0))
    resident = lambda rows: pl.BlockSpec((rows, d_model), lambda i: (0, 0), pipeline_mode=pl.Buffered(1))
    return pl.pallas_call(
        functools.partial(_merge_kernel, tm=tm, tn=COL_BLOCK, r=r, pitch=pitch),
        out_shape=_sds((n_tok, d_model), BF16),
        grid=(n_tok // tm,),
        in_specs=[prow, prow, row(hw), pl.BlockSpec((1, hw), lambda i: (0, 0)),
                  row(gw), row(gw), res(4), res(4), res(16), res(16),
                  pl.BlockSpec((tm, d_model), lambda i: (i, 0)),
                  pl.BlockSpec((tm, d_model), lambda i: (i, 1)),
                  resident(hw), resident(gw)],
        out_specs=pl.BlockSpec((tm, d_model), lambda i: (i, 0)),
        scratch_shapes=[pltpu.VMEM((tm, hw), BF16), pltpu.VMEM((tm, gw), BF16)]
                       + [pltpu.VMEM((gw // LANES, tm, LANES), F32)] * 4,
        compiler_params=_params(("arbitrary",)),
        name="gated_merge",
    )(yconv, zin, x1c, skip, o0, l0, o1, l1, o2, l2, gates, gates, wh, wa)


def _router_kernel(mo_ref, x_ref, gm_ref, gpost_ref, gpre_ref, sc_ref, sh_ref, wr_ref, br_ref,
                   x1_ref, h2_ref, idx_ref, tw_ref, *, ts, d_model, n_experts):
    mo = mo_ref[...]
    y = mo * lax.rsqrt(jnp.mean(mo * mo, axis=-1, keepdims=True) + RMS_EPS) * gpost_ref[...]
    x1 = x_ref[...] + gm_ref[...] * y
    x1_ref[...] = x1
    h2 = (x1 * lax.rsqrt(jnp.mean(x1 * x1, axis=-1, keepdims=True) + RMS_EPS) * gpre_ref[...]
          * (1.0 + sc_ref[...]) + sh_ref[...])
    nct = d_model // 2 // LANES
    sp = nct + SLAB_PAD
    words = _pack_bf16_pairs(h2)
    for c in range(nct):
        h2_ref[pl.ds(c, ts, stride=sp), :] = words[:, c * LANES:(c + 1) * LANES]
    for c in range(nct, sp):
        h2_ref[pl.ds(c, ts, stride=sp), :] = jnp.zeros((ts, LANES), jnp.uint32)
    logits = jnp.dot(h2, wr_ref[...], precision=HIGHEST, preferred_element_type=F32) + br_ref[...]
    lane = lax.broadcasted_iota(jnp.int32, logits.shape, 1)
    lane_f = lane.astype(F32)
    logits = jnp.where(lane < n_experts, logits, -jnp.inf)
    idx_out = jnp.zeros(logits.shape, jnp.int32)
    val_out = jnp.zeros(logits.shape, F32)
    top0 = None
    den = None
    for k in range(TOP_K):
        m = jnp.max(logits, axis=-1, keepdims=True)
        idx = jnp.min(jnp.where(logits == m, lane_f, float(LANES)), axis=-1, keepdims=True).astype(jnp.int32)
        if k == 0:
            top0 = m
        e = jnp.exp(m - top0)
        den = e if den is None else den + e
        idx_out = jnp.where(lane == k, idx, idx_out)
        val_out = jnp.where(lane == k, e, val_out)
        logits = jnp.where(lane == idx, -jnp.inf, logits)
    idx_ref[...] = idx_out
    tw_ref[...] = val_out / den


def _post_mix_and_route(mo, x, gate_m, g_post, g_pre, scale_f, shift_f, w_router, b_router):
    b, s, d = x.shape
    e = w_router.shape[1]
    ts = 256
    wr = jnp.zeros((d, LANES), F32).at[:, :e].set(w_router)
    br = jnp.zeros((1, LANES), F32).at[0, :e].set(b_router)
    sp = d // 2 // LANES + SLAB_PAD
    spb = s // ts
    rowblk = lambda width: pl.BlockSpec((ts, width), lambda i: (i, 0))
    per_batch = pl.BlockSpec((None, 1, d), lambda i: (i // spb, 0, 0))
    vec = pl.BlockSpec((1, d), lambda i: (0, 0))
    n_tok = b * s
    return pl.pallas_call(
        functools.partial(_router_kernel, ts=ts, d_model=d, n_experts=e),
        out_shape=(_sds((n_tok, d), F32), _sds((n_tok * sp, LANES), jnp.uint32),
                   _sds((n_tok, LANES), jnp.int32), _sds((n_tok, LANES), F32)),
        grid=(n_tok // ts,),
        in_specs=[rowblk(d), rowblk(d), per_batch, vec, vec, per_batch, per_batch,
                  pl.BlockSpec((d, LANES), lambda i: (0, 0)), pl.BlockSpec((1, LANES), lambda i: (0, 0))],
        out_specs=(rowblk(d), pl.BlockSpec((ts * sp, LANES), lambda i: (i, 0)), rowblk(LANES), rowblk(LANES)),
        compiler_params=_params(("arbitrary",)),
        name="post_mix_route",
    )(mo, x.reshape(n_tok, d), gate_m, g_post, g_pre, scale_f, shift_f, wr, br)


W_CHUNKS = 8


def _expert_up_kernel(be_ref, nu_ref, sch_ref, tok0_ref, tok1_ref, h2_hbm, wg_hbm, bg_ref, wu_hbm, bu_ref, act_ref,
                      xbuf, xb, wbuf, stage, sem, wsem, *, tb, nct, rc):
    i = pl.program_id(0)
    n_used = nu_ref[0]
    sp = nct + SLAB_PAD

    def row_copy(t, j, slot):
        return pltpu.make_async_copy(h2_hbm.at[pl.ds(pl.multiple_of(t * sp, 8), nct), :],
                                     xbuf.at[slot, pl.ds(pl.multiple_of(j * sp, 8), nct), :], sem.at[slot])

    def gather(tok_ref, slot):
        def body(j, carry):
            row_copy(tok_ref[0, j], j, slot).start()
            return carry
        lax.fori_loop(0, tb, body, 0, unroll=8)

    def wait(slot):
        pltpu.make_async_copy(h2_hbm.at[pl.ds(0, tb * nct), :], xbuf.at[slot, pl.ds(0, tb * nct), :],
                              sem.at[slot]).wait()

    def chunk_copies(e, c, st):
        rows = pl.ds(pl.multiple_of(c * rc, rc), rc)
        return (pltpu.make_async_copy(wg_hbm.at[e, rows, :], stage.at[st, 0], wsem.at[st]),
                pltpu.make_async_copy(wu_hbm.at[e, rows, :], stage.at[st, 1], wsem.at[st]))

    def chunk_start(e, c, st):
        for cp in chunk_copies(e, c, st):
            cp.start()

    def finish_chunks(e, lo, hi, half):
        def body(c, carry):
            st = c % 2
            for cp in chunk_copies(e, c, st):
                cp.wait()
            rows = pl.ds(pl.multiple_of(c * rc, rc), rc)
            wbuf[2 * half, rows, :] = stage[st, 0].astype(BF16)
            wbuf[2 * half + 1, rows, :] = stage[st, 1].astype(BF16)

            @pl.when(c + 1 < W_CHUNKS)
            def _():
                chunk_start(e, c + 1, 1 - st)
            return carry
        lax.fori_loop(lo, hi, body, 0)

    half = sch_ref[5 * i]

    @pl.when((i == 0) & (n_used > 0))
    def _():
        gather(tok0_ref, 0)
        chunk_start(be_ref[0], 0, 0)
        finish_chunks(be_ref[0], 0, W_CHUNKS, half)

    @pl.when(i < n_used)
    def _():
        nxt = sch_ref[5 * i + 1]

        @pl.when(sch_ref[5 * i + 2] == 1)
        def _():
            chunk_start(nxt, 0, 0)

        finish_chunks(nxt, sch_ref[5 * i + 3], sch_ref[5 * i + 4], 1 - half)

    @pl.when(i < n_used)
    def _():
        slot = i % 2
        wait(slot)
        for j in range(tb):
            row_copy(tok1_ref[0, j], j, 1 - slot).start()
        for c in range(nct):
            hi, lo = _unpack_bf16_pairs(xbuf[slot, pl.ds(c, tb, stride=sp), :])
            xb[:, 2 * c * LANES:(2 * c + 1) * LANES] = hi.astype(BF16)
            xb[:, (2 * c + 1) * LANES:(2 * c + 2) * LANES] = lo.astype(BF16)
        x = xb[...]
        g = jnp.dot(x, wbuf[2 * half], preferred_element_type=F32) + bg_ref[...]
        u = jnp.dot(x, wbuf[2 * half + 1], preferred_element_type=F32) + bu_ref[...]
        g = jnp.minimum(g, SWIGLU_LIMIT)
        u = jnp.clip(u, -SWIGLU_LIMIT, SWIGLU_LIMIT)
        act_ref[...] = (g * jax.nn.sigmoid(SWIGLU_ALPHA * g) * (u + 1.0)).astype(act_ref.dtype)

    @pl.when((i == n_used) & (n_used > 0))
    def _():
        wait(i % 2)

    @pl.when(i >= n_used)
    def _():
        act_ref[...] = jnp.zeros(act_ref.shape, act_ref.dtype)


def _expert_up(block_e, n_used, sched, row_tok3, h2s, wg, bg, wu, bu, nct):
    n_blocks, _, tb = row_tok3.shape
    _, d, f = wg.shape
    rc = d // W_CHUNKS
    grid_spec = pltpu.PrefetchScalarGridSpec(
        num_scalar_prefetch=3,
        grid=(n_blocks,),
        in_specs=[pl.BlockSpec((None, 1, tb), lambda i, be, nu, sc: (i, 0, 0), memory_space=pltpu.SMEM),
                  pl.BlockSpec((None, 1, tb), lambda i, be, nu, sc: (jnp.minimum(i + 1, n_blocks - 1), 0, 0),
                               memory_space=pltpu.SMEM),
                  pl.BlockSpec(memory_space=pl.ANY),
                  pl.BlockSpec(memory_space=pl.ANY),
                  pl.BlockSpec((None, 1, f), lambda i, be, nu, sc: (be[i], 0, 0)),
                  pl.BlockSpec(memory_space=pl.ANY),
                  pl.BlockSpec((None, 1, f), lambda i, be, nu, sc: (be[i], 0, 0))],
        out_specs=pl.BlockSpec((tb, f), lambda i, be, nu, sc: (i, 0)),
        scratch_shapes=[pltpu.VMEM((2, tb * (nct + SLAB_PAD), LANES), jnp.uint32), pltpu.VMEM((tb, d), BF16),
                        pltpu.VMEM((4, d, f), BF16), pltpu.VMEM((2, 2, rc, f), F32),
                        pltpu.SemaphoreType.DMA((2,)), pltpu.SemaphoreType.DMA((2,))],
    )
    return pl.pallas_call(
        functools.partial(_expert_up_kernel, tb=tb, nct=nct, rc=rc),
        out_shape=_sds((n_blocks * tb, f), BF16),
        grid_spec=grid_spec,
        compiler_params=_params(("arbitrary",)),
        name="expert_up",
    )(block_e, n_used, sched, row_tok3, row_tok3, h2s, wg, bg, wu, bu)


def _expert_down_kernel(be_ref, nu_ref, act_ref, wd_ref, bd_ref, ys_ref, wb, *, tb, nct):
    i = pl.program_id(0)

    @pl.when((i < nu_ref[0]) & ((i == 0) | (be_ref[i] != be_ref[jnp.maximum(i - 1, 0)])))
    def _():
        wb[...] = wd_ref[...].astype(BF16)

    @pl.when(i < nu_ref[0])
    def _():
        act = act_ref[...]
        sp = nct + SLAB_PAD
        for c in range(nct):
            cols = slice(2 * c * LANES, (2 * c + 2) * LANES)
            y = jnp.dot(act, wb[:, cols], preferred_element_type=F32) + bd_ref[:, cols]
            ys_ref[pl.ds(c, tb, stride=sp), :] = _pack_bf16_pairs(y)
        for c in range(nct, sp):
            ys_ref[pl.ds(c, tb, stride=sp), :] = jnp.zeros((tb, LANES), jnp.uint32)

    @pl.when(i >= nu_ref[0])
    def _():
        ys_ref[...] = jnp.zeros(ys_ref.shape, ys_ref.dtype)


def _expert_down(block_e, n_used, act, wd, bd, tb):
    _, f, d = wd.shape
    n_blocks = act.shape[0] // tb
    nct = d // 2 // LANES
    grid_spec = pltpu.PrefetchScalarGridSpec(
        num_scalar_prefetch=2,
        grid=(n_blocks,),
        in_specs=[pl.BlockSpec((tb, f), lambda i, be, nu: (i, 0)),
                  pl.BlockSpec((None, f, d), lambda i, be, nu: (be[i], 0, 0)),
                  pl.BlockSpec((None, 1, d), lambda i, be, nu: (be[i], 0, 0))],
        out_specs=pl.BlockSpec((tb * (nct + SLAB_PAD), LANES), lambda i, be, nu: (i, 0)),
        scratch_shapes=[pltpu.VMEM((f, d), BF16)],
    )
    return pl.pallas_call(
        functools.partial(_expert_down_kernel, tb=tb, nct=nct),
        out_shape=_sds((n_blocks * tb * (nct + SLAB_PAD), LANES), jnp.uint32),
        grid_spec=grid_spec,
        compiler_params=_params(("arbitrary",)),
        name="expert_down",
    )(block_e, n_used, act, wd, bd)


def _combine_kernel(d0_ref, d1_ref, ys_hbm, tw_ref, x1_ref, gf_ref, gpost_ref, o_ref, buf, ff, sem, *, tc, nct):
    i = pl.program_id(0)
    last = pl.num_programs(0) - 1
    n_rows = TOP_K * tc
    sp = nct + SLAB_PAD

    def row_copy(r, j, slot):
        return pltpu.make_async_copy(ys_hbm.at[pl.ds(pl.multiple_of(r * sp, 8), nct), :],
                                     buf.at[slot, pl.ds(pl.multiple_of(j * sp, 8), nct), :], sem.at[slot])

    def gather(dref, slot):
        def body(j, carry):
            row_copy(dref[0, j], j, slot).start()
            return carry
        lax.fori_loop(0, n_rows, body, 0, unroll=8)

    def wait(slot):
        pltpu.make_async_copy(ys_hbm.at[pl.ds(0, n_rows * nct), :], buf.at[slot, pl.ds(0, n_rows * nct), :],
                              sem.at[slot]).wait()

    @pl.when(i == 0)
    def _():
        gather(d0_ref, 0)

    slot = i % 2
    wait(slot)
    for j in range(n_rows):
        row_copy(d1_ref[0, j], j, 1 - slot).start()
    wk =[tw_ref[:, k:k + 1] for k in range(TOP_K)]
    for c in range(nct):
        acc_hi = acc_lo = None
        for k in range(TOP_K):
            hi, lo = _unpack_bf16_pairs(buf[slot, pl.ds(k * tc * sp + c, tc, stride=sp), :])
            acc_hi = wk[k] * hi if acc_hi is None else acc_hi + wk[k] * hi
            acc_lo = wk[k] * lo if acc_lo is None else acc_lo + wk[k] * lo
        ff[:, 2 * c * LANES:(2 * c + 1) * LANES] = acc_hi
        ff[:, (2 * c + 1) * LANES:(2 * c + 2) * LANES] = acc_lo
    f = ff[...]
    y = f * lax.rsqrt(jnp.mean(f * f, axis=-1, keepdims=True) + RMS_EPS) * gpost_ref[...]
    o_ref[...] = x1_ref[...] + gf_ref[...] * y

    @pl.when(i == last)
    def _():
        wait(1 - slot)


def _combine(dest3, ys, top_w, x1, gate_f, g_post, seq):
    n_tok, d = x1.shape
    n_steps, _, n_rows = dest3.shape
    tc = n_rows // TOP_K
    nct = d // 2 // LANES
    spb = seq // tc
    return pl.pallas_call(
        functools.partial(_combine_kernel, tc=tc, nct=nct),
        out_shape=_sds((n_tok, d), F32),
        grid=(n_steps,),
        in_specs=[pl.BlockSpec((None, 1, n_rows), lambda i: (i, 0, 0), memory_space=pltpu.SMEM),
                  pl.BlockSpec((None, 1, n_rows), lambda i: (jnp.minimum(i + 1, n_steps - 1), 0, 0),
                               memory_space=pltpu.SMEM),
                  pl.BlockSpec(memory_space=pl.ANY),
                  pl.BlockSpec((tc, LANES), lambda i: (i, 0)),
                  pl.BlockSpec((tc, d), lambda i: (i, 0)),
                  pl.BlockSpec((None, 1, d), lambda i: (i // spb, 0, 0)),
                  pl.BlockSpec((1, d), lambda i: (0, 0))],
        out_specs=pl.BlockSpec((tc, d), lambda i: (i, 0)),
        scratch_shapes=[pltpu.VMEM((2, n_rows * (nct + SLAB_PAD), LANES), jnp.uint32), pltpu.VMEM((tc, d), F32),
                        pltpu.SemaphoreType.DMA((2,))],
        compiler_params=_params(("arbitrary",)),
        name="expert_combine",
    )(dest3, dest3, ys, top_w, x1, gate_f, g_post)


def _rank_kernel(idx_ref, tri_ref, rank_ref, cnt_ref, carry, *, tr):
    @pl.when(pl.program_id(0) == 0)
    def _():
        carry[...] = jnp.zeros(carry.shape, F32)

    idx = idx_ref[...]
    lane = lax.broadcasted_iota(jnp.int32, idx.shape, 1)
    base = carry[0:1, :]
    out = jnp.zeros(idx.shape, jnp.int32)
    for k in range(TOP_K):
        onehot = jnp.where(lane == idx[:, k:k + 1], 1.0, 0.0)
        csum = jnp.dot(tri_ref[...], onehot.astype(BF16), preferred_element_type=F32)
        rank = jnp.sum(onehot * (csum + base), axis=-1, keepdims=True) - 1.0
        out = jnp.where(lane == k, rank.astype(jnp.int32), out)
        base = base + csum[tr - 1:tr, :]
    rank_ref[...] = out
    carry[...] = jnp.broadcast_to(base, carry.shape)
    cnt_ref[...] = jnp.broadcast_to(base, cnt_ref.shape).astype(jnp.int32)


def _expert_ranks(top_idx_padded):
    n_tok = top_idx_padded.shape[0]
    tr = 512
    tri = jnp.asarray(np.tril(np.ones((tr, tr), np.float32)), BF16)
    return pl.pallas_call(
        functools.partial(_rank_kernel, tr=tr),
        out_shape=(_sds((n_tok, LANES), jnp.int32), _sds((8, LANES), jnp.int32)),
        grid=(n_tok // tr,),
        in_specs=[pl.BlockSpec((tr, LANES), lambda i: (i, 0)), pl.BlockSpec((tr, tr), lambda i: (0, 0))],
        out_specs=(pl.BlockSpec((tr, LANES), lambda i: (i, 0)), pl.BlockSpec((8, LANES), lambda i: (0, 0))),
        scratch_shapes=[pltpu.VMEM((8, LANES), F32)],
        compiler_params=_params(("arbitrary",)),
        name="expert_ranks",
    )(top_idx_padded, tri)


def _routing_tables(top_idx_padded, n_experts, tb):
    n_tok = top_idx_padded.shape[0]
    n_assign = n_tok * TOP_K
    ranks, counts = _expert_ranks(top_idx_padded)
    sizes = counts[0, :n_experts]
    padded = (sizes + tb - 1) // tb * tb
    pad_end = jnp.cumsum(padded)
    pad_start = pad_end - padded
    top_idx = top_idx_padded[:, :TOP_K]
    experts = jnp.arange(n_experts, dtype=jnp.int32)
    start_of = jnp.sum(jnp.where(top_idx[:, :, None] == experts, pad_start, 0), axis=-1)
    dest = (start_of + ranks[:, :TOP_K]).astype(jnp.int32).reshape(-1)
    n_rows = -(-n_assign // tb) * tb + n_experts * tb
    n_blocks = n_rows // tb
    tok = (jnp.arange(n_assign, dtype=jnp.int32) // TOP_K)
    row_tok = jnp.zeros((n_rows,), jnp.int32).at[dest].set(tok)
    block_start = jnp.arange(n_blocks, dtype=jnp.int32) * tb
    block_e = jnp.minimum(jnp.sum((pad_end[None, :] <= block_start[:, None]).astype(jnp.int32), axis=1),
                          n_experts - 1).astype(jnp.int32)
    n_used = (pad_end[-1] // tb).astype(jnp.int32).reshape(1)
    nb = padded // tb
    occupied = nb > 0
    ordinal = jnp.cumsum(occupied.astype(jnp.int32)) - 1
    later = lax.cummin(jnp.where(occupied, experts, n_experts), reverse=True)
    next_e = jnp.concatenate([later[1:], jnp.full((1,), n_experts, jnp.int32)])
    blk = jnp.arange(n_blocks, dtype=jnp.int32)
    k = blk - (pad_start // tb)[block_e]
    nbe = nb[block_e]
    active = (blk < n_used[0]) & (next_e[block_e] < n_experts)
    zero = jnp.zeros_like(blk)
    spread = jnp.maximum(nbe - 1, 1)
    lo = jnp.where(nbe > 1, jnp.maximum(k - 1, 0) * W_CHUNKS // spread, 0)
    hi = jnp.where(nbe > 1, k * W_CHUNKS // spread, W_CHUNKS)
    sched = jnp.stack([ordinal[block_e] % 2,
                       jnp.where(active, next_e[block_e], zero),
                       jnp.where(active & (k == 0), 1, zero),
                       jnp.where(active, lo, zero),
                       jnp.where(active, hi, zero)], axis=1).reshape(-1).astype(jnp.int32)
    return dest, row_tok, block_e, n_used, sched, n_rows


def _moe(h2s, top_idx, top_w, wg, bg, wu, bu, wd, bd, x1, gate_f, g_post, seq):
    n_tok, d = x1.shape
    n_experts = wg.shape[0]
    tb = MOE_ROWS
    nct = d // 2 // LANES
    dest, row_tok, block_e, n_used, sched, n_rows = _routing_tables(top_idx, n_experts, tb)
    n_blocks = n_rows // tb
    act = _expert_up(block_e, n_used, sched, row_tok.reshape(n_blocks, 1, tb), h2s, wg, bg[:, None, :], wu,
                     bu[:, None, :], nct)
    ys = _expert_down(block_e, n_used, act, wd, bd[:, None, :], tb)
    tc = 128
    dest3 = dest.reshape(n_tok // tc, tc, TOP_K).transpose(0, 2, 1).reshape(n_tok // tc, 1, TOP_K * tc)
    return _combine(dest3, ys, top_w, x1, gate_f, g_post, seq)


def _layer(x, c8, p):
    b, s, d = x.shape
    n_tok = b * s
    hw = p["hy_skip"].shape[0]
    gw = GROUP_WIDTH
    assert hw == COL_BLOCK and gw == COL_BLOCK and d % COL_BLOCK == 0

    mod = _adaln(c8, p["w_ada"], p["b_ada"][None])[:b]
    shift_m, scale_m, gate_m, shift_f, scale_f, gate_f = [m[:, None, :] for m in jnp.split(mod, 6, axis=-1)]

    h, h4, h16 = _prenorm_mix(x, p["g_pre_mix"][None], scale_m, shift_m)

    w_in = p["w_in"].astype(BF16)
    nd = d // COL_BLOCK
    pn = _matmul(h.reshape(n_tok, d), w_in, [0, 1, 2, 3, 6, 9], BF16, "in_proj_natural")
    gates = _matmul(h.reshape(n_tok, d), w_in, list(range(12, 12 + 2 * nd)), BF16, "in_proj_gates")
    qkv1 = _matmul(h4.reshape(n_tok, d), w_in, [4, 7, 10], BF16, "in_proj_dil4")
    qkv2 = _matmul(h16.reshape(n_tok, d), w_in, [5, 8, 11], BF16, "in_proj_dil16")

    r, pitch = _fft_radix(s)
    hfb = _hyena_filters(s, r, pitch, p["hy_f_w1"], p["hy_f_b1"], p["hy_f_w2"], p["hy_f_b2"], p["hy_f_w3"],
                         p["hy_f_b3"], p["hy_f_freq"], p["hy_f_wout"])
    zin, x1c = _shortconv(pn.reshape(b, s, -1), p["hy_conv_w"], p["hy_conv_b"], hw, r, pitch)
    yconv = _long_conv(zin, hfb, r, pitch)

    slopes = _alibi_slopes(N_GROUPS * HEADS_PER_GROUP).reshape(N_GROUPS, HEADS_PER_GROUP)
    o0, l0 = _dilated_attention(pn.reshape(b, s, -1), s, 1, slopes[0], col0=3)
    o1, l1 = _dilated_attention(qkv1.reshape(b * 4, s // 4, 3 * gw), s // 4, 4, slopes[1])
    o2, l2 = _dilated_attention(qkv2.reshape(b * 16, s // 16, 3 * gw), s // 16, 16, slopes[2])

    merged = _merge(yconv.reshape(-1, hw), zin.reshape(-1, hw), x1c.reshape(n_tok, hw), p["hy_skip"][None],
                    o0.reshape(n_tok, gw), l0.reshape(n_tok, gw),
                    o1.reshape(b, 4, s // 4, gw), l1.reshape(b, 4, s // 4, gw),
                    o2.reshape(b, 16, s // 16, gw), l2.reshape(b, 16, s // 16, gw),
                    gates, p["w_proj_hyena"].astype(BF16), p["w_proj_attn"].astype(BF16), s, r, pitch)
    mo = _matmul(merged, p["w_out"].astype(BF16), list(range(nd)), F32, "out_proj")

    x1, h2s, top_idx, top_w = _post_mix_and_route(mo, x, gate_m, p["g_post_mix"][None], p["g_pre_ffn"][None],
                                                  scale_f, shift_f, p["w_router"], p["b_router"])
    out = _moe(h2s, top_idx, top_w, p["w_gate"], p["b_gate"], p["w_up"], p["b_up"], p["w_down"], p["b_down"], x1, gate_f,
               p["g_post_ffn"][None], s)
    return out.reshape(b, s, d)


def kernel(x, c, w_ada, b_ada, g_pre_mix, g_post_mix, g_pre_ffn, g_post_ffn, w_in, hy_conv_w, hy_conv_b, hy_skip, hy_f_w1, hy_f_b1, hy_f_w2, hy_f_b2, hy_f_w3, hy_f_b3, hy_f_freq, hy_f_wout, w_proj_hyena, w_proj_attn, w_out, w_router, b_router, w_gate, b_gate, w_up, b_up, w_down, b_down):
    names = ("w_ada", "b_ada", "g_pre_mix", "g_post_mix", "g_pre_ffn", "g_post_ffn", "w_in", "hy_conv_w",
             "hy_conv_b", "hy_skip", "hy_f_w1", "hy_f_b1", "hy_f_w2", "hy_f_b2", "hy_f_w3", "hy_f_b3",
             "hy_f_freq", "hy_f_wout", "w_proj_hyena", "w_proj_attn", "w_out", "w_router", "b_router",
             "w_gate", "b_gate", "w_up", "b_up", "w_down", "b_down")
    stacked = (w_ada, b_ada, g_pre_mix, g_post_mix, g_pre_ffn, g_post_ffn, w_in, hy_conv_w, hy_conv_b, hy_skip,
               hy_f_w1, hy_f_b1, hy_f_w2, hy_f_b2, hy_f_w3, hy_f_b3, hy_f_freq, hy_f_wout, w_proj_hyena,
               w_proj_attn, w_out, w_router, b_router, w_gate, b_gate, w_up, b_up, w_down, b_down)
    depth = w_ada.shape[0]
    b = x.shape[0]
    c8 = jnp.zeros((8, c.shape[1]), F32).at[:b].set(c)
    for l in range(depth):
        x = _layer(x, c8, {k: v[l] for k, v in zip(names, stacked)})
    return x
```

```python
import functools
import math

import jax
import jax.numpy as jnp
import numpy as np
from jax import lax
from jax.experimental import pallas as pl
from jax.experimental.pallas import tpu as pltpu

F32 = jnp.float32
BF16 = jnp.bfloat16
HIGHEST = lax.Precision.HIGHEST

LANES = 128
HEAD_DIM = 128
HEADS_PER_GROUP = 8
DILATED_GROUPS = ((128, 1), (512, 4), (2048, 16))
N_GROUPS = len(DILATED_GROUPS)
GROUP_WIDTH = HEADS_PER_GROUP * HEAD_DIM
ATTN_SIDE = 64
TOP_K = 4
SWIGLU_LIMIT = 7.0
SWIGLU_ALPHA = 1.702
RMS_EPS = 1e-6
NEG_INF = -1e30
HYENA_N_BANDS = 16
HYENA_DECAY_TARGET = 1e-2
HYENA_FAST_DECAY_PCT = 0.3
HYENA_SLOW_DECAY_PCT = 1.5
COL_BLOCK = 1024
MOE_ROWS = 256
SLAB_PAD = 8
VMEM_LIMIT = 56 * 1024 * 1024


def _pack_bf16_pairs(x):
    bits = lax.bitcast_convert_type(x.astype(BF16).astype(F32), jnp.uint32)
    groups = [bits[:, j:j + LANES] | (bits[:, j + LANES:j + 2 * LANES] >> 16)
              for j in range(0, x.shape[1], 2 * LANES)]
    return groups[0] if len(groups) == 1 else jnp.concatenate(groups, axis=1)


def _unpack_bf16_pairs(w):
    hi = lax.bitcast_convert_type(w & jnp.uint32(0xFFFF0000), F32)
    lo = lax.bitcast_convert_type(w << 16, F32)
    return hi, lo


def _params(sem, vmem=VMEM_LIMIT):
    return pltpu.CompilerParams(dimension_semantics=sem, vmem_limit_bytes=vmem)


def _sds(shape, dtype):
    return jax.ShapeDtypeStruct(shape, dtype)


def _ada_kernel(c_ref, w_ref, b_ref, o_ref):
    c = c_ref[...]
    sc = (c * jax.nn.sigmoid(c)).astype(BF16)
    o_ref[...] = jnp.dot(sc, w_ref[...].astype(BF16), preferred_element_type=F32) + b_ref[...]


def _adaln(c8, w_ada, b_ada):
    d, cols = w_ada.shape
    tn = 512
    return pl.pallas_call(
        _ada_kernel,
        out_shape=_sds((8, cols), F32),
        grid=(cols // tn,),
        in_specs=[pl.BlockSpec((8, d), lambda j: (0, 0)),
                  pl.BlockSpec((d, tn), lambda j: (0, j)),
                  pl.BlockSpec((1, tn), lambda j: (0, j))],
        out_specs=pl.BlockSpec((8, tn), lambda j: (0, j)),
        compiler_params=_params(("arbitrary",)),
        name="adaln",
    )(c8, w_ada, b_ada)


def _prenorm_kernel(x_ref, g_ref, sc_ref, sh_ref, o_ref, o4_ref, o16_ref, scr_ref, *, ts, d_model):
    x = x_ref[...]
    ms = jnp.mean(x * x, axis=-1, keepdims=True)
    h = x * lax.rsqrt(ms + RMS_EPS) * g_ref[...] * (1.0 + sc_ref[...]) + sh_ref[...]
    o_ref[...] = h.astype(BF16)
    nct = d_model // LANES
    for c in range(nct):
        scr_ref[c] = h[:, c * LANES:(c + 1) * LANES]
    for dil, oref in ((4, o4_ref), (16, o16_ref)):
        for r in range(dil):
            for c in range(nct):
                oref[r, :, c * LANES:(c + 1) * LANES] = scr_ref[c, pl.ds(r, ts // dil, stride=dil), :].astype(BF16)


def _prenorm_mix(x, g, scale, shift):
    b, s, d = x.shape
    ts = 256
    kern = functools.partial(_prenorm_kernel, ts=ts, d_model=d)
    return pl.pallas_call(
        kern,
        out_shape=(_sds((b, s, d), BF16), _sds((b, 4, s // 4, d), BF16), _sds((b, 16, s // 16, d), BF16)),
        grid=(b, s // ts),
        in_specs=[pl.BlockSpec((None, ts, d), lambda bi, i: (bi, i, 0)),
                  pl.BlockSpec((1, d), lambda bi, i: (0, 0)),
                  pl.BlockSpec((None, 1, d), lambda bi, i: (bi, 0, 0)),
                  pl.BlockSpec((None, 1, d), lambda bi, i: (bi, 0, 0))],
        out_specs=(pl.BlockSpec((None, ts, d), lambda bi, i: (bi, i, 0)),
                   pl.BlockSpec((None, 4, ts // 4, d), lambda bi, i: (bi, 0, i, 0)),
                   pl.BlockSpec((None, 16, ts // 16, d), lambda bi, i: (bi, 0, i, 0))),
        scratch_shapes=[pltpu.VMEM((d // LANES, ts, LANES), F32)],
        compiler_params=_params(("arbitrary", "arbitrary")),
        name="prenorm_mix",
    )(x, g, scale, shift)


def _mm_kernel(tbl_ref, a_ref, w_ref, o_ref):
    del tbl_ref
    o_ref[...] = jnp.dot(a_ref[...], w_ref[...], preferred_element_type=F32).astype(o_ref.dtype)


def _matmul(a, w, col_blocks, out_dtype, name, tm=1024, tn=COL_BLOCK):
    m, k = a.shape
    tm = min(tm, m)
    nb = len(col_blocks)
    tbl = jnp.asarray(col_blocks, jnp.int32)
    grid_spec = pltpu.PrefetchScalarGridSpec(
        num_scalar_prefetch=1,
        grid=(m // tm, nb),
        in_specs=[pl.BlockSpec((tm, k), lambda i, j, t: (i, 0)),
                  pl.BlockSpec((k, tn), lambda i, j, t: (0, t[j]))],
        out_specs=pl.BlockSpec((tm, tn), lambda i, j, t: (i, j)),
    )
    return pl.pallas_call(
        _mm_kernel,
        out_shape=_sds((m, nb * tn), out_dtype),
        grid_spec=grid_spec,
        compiler_params=_params(("arbitrary", "arbitrary")),
        name=name,
    )(tbl, a, w)


def _filter_kernel(z_ref, w1_ref, b1_ref, w2_ref, b2_ref, w3_ref, b3_ref, fr_ref, wo_ref, dl_ref,
                   h_ref, *, hw, r, pitch):
    def dot(a, b):
        return jnp.dot(a, b, precision=HIGHEST, preferred_element_type=F32)

    z = z_ref[...]
    fr = fr_ref[...]
    h = jnp.sin(fr * (dot(z, w1_ref[...]) + b1_ref[...]))
    h = jnp.sin(fr * (dot(h, w2_ref[...]) + b2_ref[...]))
    h = jnp.sin(fr * (dot(h, w3_ref[...]) + b3_ref[...]))
    filt = dot(h, wo_ref[...])
    decay = jnp.exp(-z[:, 0:1] * dl_ref[...])
    hf = filt[:, :hw] * decay
    hb = filt[:, hw:] * decay
    pad = jnp.zeros((pitch - r, hw), F32)
    for g in range(z.shape[0] // r):
        h_ref[0, g * pitch:g * pitch + r, :] = hf[g * r:(g + 1) * r]
        h_ref[1, g * pitch:g * pitch + r, :] = hb[g * r:(g + 1) * r]
        h_ref[0, g * pitch + r:(g + 1) * pitch, :] = pad
        h_ref[1, g * pitch + r:(g + 1) * pitch, :] = pad


def _hyena_filters(length, r, pitch, w1, b1, w2, b2, w3, b3, freq, wout):
    emb, fw = w1.shape
    hw = wout.shape[1] // 2
    t = np.linspace(0.0, 1.0, length)[:, None]
    bands = np.linspace(1e-4, HYENA_N_BANDS - 1, HYENA_N_BANDS)[None, :]
    ang = (2.0 * math.pi / length) * np.arange(length)[:, None] * bands
    z = np.concatenate([t, np.cos(ang), -np.sin(ang)], axis=-1)
    zpad = np.zeros((length, LANES), np.float32)
    zpad[:, :emb] = z
    w1p = jnp.zeros((LANES, fw), F32).at[:emb].set(w1)
    min_decay = math.log(HYENA_DECAY_TARGET) / HYENA_FAST_DECAY_PCT
    max_decay = math.log(HYENA_DECAY_TARGET) / HYENA_SLOW_DECAY_PCT
    deltas = np.abs(np.linspace(min_decay, max_decay, hw))[None, :].astype(np.float32)
    tl = min(1024, length)
    full = lambda shape: pl.BlockSpec(shape, lambda i: (0,) * len(shape))
    return pl.pallas_call(
        functools.partial(_filter_kernel, hw=hw, r=r, pitch=pitch),
        out_shape=_sds((2, length // r * pitch, hw), F32),
        grid=(length // tl,),
        in_specs=[pl.BlockSpec((tl, LANES), lambda i: (i, 0)),
                  full((LANES, fw)), full((1, fw)), full((fw, fw)), full((1, fw)),
                  full((fw, fw)), full((1, fw)), full((1, fw)), full((fw, 2 * hw)), full((1, hw))],
        out_specs=pl.BlockSpec((2, tl // r * pitch, hw), lambda i: (0, i, 0)),
        compiler_params=_params(("arbitrary",)),
        name="hyena_filters",
    )(jnp.asarray(zpad), w1p, b1[None], w2, b2[None], w3, b3[None], freq[None], wout, jnp.asarray(deltas))


def _shortconv_kernel(u_ref, up_ref, un_ref, w_ref, b_ref, z_ref, x1_ref, *, ts, hw, r, pitch):
    i = pl.program_id(1)
    last = pl.num_programs(1) - 1
    u = u_ref[...].astype(F32)
    prev_blk = up_ref[...].astype(F32)
    next_blk = un_ref[...].astype(F32)
    prev_row = jnp.where(i > 0, prev_blk[15:16, :], 0.0)
    next_row = jnp.where(i < last, next_blk[0:1, :], 0.0)
    row = lax.broadcasted_iota(jnp.int32, u.shape, 0)
    um = jnp.where(row == 0, prev_row, pltpu.roll(u, 1, 0))
    up = jnp.where(row == ts - 1, next_row, pltpu.roll(u, ts - 1, 0))
    w = w_ref[...]
    uc = w[0:1] * um + w[1:2] * u + w[2:3] * up + b_ref[...]
    x1_ref[...] = uc[:, :hw]
    z = uc[:, 2 * hw:] * uc[:, hw:2 * hw]
    pad = jnp.zeros((pitch - r, hw), F32)
    for g in range(ts // r):
        z_ref[g * pitch:g * pitch + r, :] = z[g * r:(g + 1) * r]
        z_ref[g * pitch + r:(g + 1) * pitch, :] = pad


def _shortconv(pn3, conv_w, conv_b, hw, r, pitch):
    b, s, _ = pn3.shape
    ts = 512
    w3 = 3 * hw
    nh = s // 16
    return pl.pallas_call(
        functools.partial(_shortconv_kernel, ts=ts, hw=hw, r=r, pitch=pitch),
        out_shape=(_sds((b, s // r * pitch, hw), F32), _sds((b, s, hw), F32)),
        grid=(b, s // ts),
        in_specs=[pl.BlockSpec((None, ts, w3), lambda bi, i: (bi, i, 0)),
                  pl.BlockSpec((None, 16, w3), lambda bi, i: (bi, jnp.maximum(i * (ts // 16) - 1, 0), 0)),
                  pl.BlockSpec((None, 16, w3), lambda bi, i: (bi, jnp.minimum((i + 1) * (ts // 16), nh - 1), 0)),
                  pl.BlockSpec((3, w3), lambda bi, i: (0, 0)),
                  pl.BlockSpec((1, w3), lambda bi, i: (0, 0))],
        out_specs=(pl.BlockSpec((None, ts // r * pitch, hw), lambda bi, i: (bi, i, 0)),
                   pl.BlockSpec((None, ts, hw), lambda bi, i: (bi, i, 0))),
        compiler_params=_params(("arbitrary", "arbitrary")),
        name="hyena_shortconv",
    )(pn3, pn3, pn3, conv_w, conv_b[None])


FFT_GROUP = 4


def _split_bf16(a):
    hi = a.astype(BF16)
    return hi, (a - hi.astype(F32)).astype(BF16)


def _dot3(fh, fl, x):
    xh, xl = _split_bf16(x)
    d = lambda a, b: jnp.dot(a, b, preferred_element_type=F32)
    return d(fh, xh) + (d(fh, xl) + d(fl, xh))


def _fft_kept(r):
    return (r // 2 + 1 + 7) // 8 * 8


def _dft_tables(r):
    idx = np.arange(r)
    ang = 2.0 * np.pi * np.outer(idx, idx) / r
    cos, sin = np.cos(ang), np.sin(ang)
    kp = _fft_kept(r)
    fa_half = np.concatenate([cos[:kp, :r // 2], -sin[:kp, :r // 2]], axis=0)
    fbig = np.block([[cos, sin], [-sin, cos]])
    fconj = np.block([[cos, -sin], [sin, cos]])
    wgt = np.zeros(kp)
    wgt[0] = wgt[r // 2] = 1.0
    wgt[1:r // 2] = 2.0
    gfin = np.concatenate([cos[:r // 2, :kp] * wgt, -sin[:r // 2, :kp] * wgt], axis=1)
    tang = 2.0 * np.pi * np.outer(idx, idx) / (r * r)
    tw = np.stack([np.cos(tang), -np.sin(tang)], axis=-1)
    split = lambda a: _split_bf16(jnp.asarray(a.astype(np.float32)))
    return split(fa_half), split(fbig), split(fconj), split(gfin), jnp.asarray(tw.astype(np.float32))


def _fft_a_kernel(x_ref, fh_ref, fl_ref, o_ref, *, r, k1, kp, pitch):
    fh, fl = fh_ref[...], fl_ref[...]
    pad = jnp.zeros((pitch - r, LANES), F32)
    for g in range(kp):
        o_ref[0, g * pitch + r:(g + 1) * pitch, :] = pad
        o_ref[1, g * pitch + r:(g + 1) * pitch, :] = pad

    def body(g, carry):
        n2 = g * FFT_GROUP
        xs = jnp.concatenate([x_ref[pl.ds(n2 + k, k1, stride=pitch), :] for k in range(FFT_GROUP)], axis=1)
        a = _dot3(fh, fl, xs)
        for k in range(FFT_GROUP):
            o_ref[0, pl.ds(n2 + k, kp, stride=pitch), :] = a[:kp, k * LANES:(k + 1) * LANES]
            o_ref[1, pl.ds(n2 + k, kp, stride=pitch), :] = a[kp:, k * LANES:(k + 1) * LANES]
        return carry

    lax.fori_loop(0, r // FFT_GROUP, body, 0)


def _fft_stage_a(x, fmat, r, pitch):
    bx, rows, c = x.shape
    k1 = rows // pitch
    kp = _fft_kept(r)
    return pl.pallas_call(
        functools.partial(_fft_a_kernel, r=r, k1=k1, kp=kp, pitch=pitch),
        out_shape=_sds((bx, 2, kp * pitch, c), F32),
        grid=(bx, c // LANES),
        in_specs=[pl.BlockSpec((None, rows, LANES), lambda b, ci: (b, 0, ci)),
                  pl.BlockSpec((2 * kp, k1), lambda b, ci: (0, 0)),
                  pl.BlockSpec((2 * kp, k1), lambda b, ci: (0, 0))],
        out_specs=pl.BlockSpec((None, 2, kp * pitch, LANES), lambda b, ci: (b, 0, 0, ci)),
        compiler_params=_params(("arbitrary", "arbitrary")),
        name="fft_stage_a",
    )(x, *fmat)


def _twiddled(a_ref, b, tr, ti, r):
    are, aim = a_ref[b, 0, :r, :], a_ref[b, 1, :r, :]
    return jnp.concatenate([are * tr - aim * ti, are * ti + aim * tr], axis=0)


def _fft_mk_kernel(a_ref, tw_ref, h0_ref, fbh_ref, fbl_ref, o_ref, *, r, scale):
    tr, ti = tw_ref[:, 0:1], tw_ref[:, 1:2]
    fbh, fbl = fbh_ref[...], fbl_ref[...]
    xf = _dot3(fbh, fbl, _twiddled(a_ref, 0, tr, ti, r))
    xb = _dot3(fbh, fbl, _twiddled(a_ref, 1, tr, ti, r))
    o_ref[0] = (xf[:r] + xb[:r] - h0_ref[...]) * scale
    o_ref[1] = (xf[r:] - xb[r:]) * scale


def _fft_filter_spectrum(a, tw, h0, fbig, r, pitch):
    c = a.shape[-1]
    kp = _fft_kept(r)
    return pl.pallas_call(
        functools.partial(_fft_mk_kernel, r=r, scale=1.0 / (r * r)),
        out_shape=_sds((2, kp * r, c), F32),
        grid=(kp,),
        in_specs=[pl.BlockSpec((2, 2, pitch, c), lambda k: (0, 0, k, 0)),
                  pl.BlockSpec((None, r, 2), lambda k: (k, 0, 0)),
                  pl.BlockSpec((1, c), lambda k: (0, 0)),
                  pl.BlockSpec((2 * r, 2 * r), lambda k: (0, 0)),
                  pl.BlockSpec((2 * r, 2 * r), lambda k: (0, 0))],
        out_specs=pl.BlockSpec((2, r, c), lambda k: (0, k, 0)),
        compiler_params=_params(("arbitrary",)),
        name="fft_filter_spectrum",
    )(a, tw, h0, *fbig)


def _fft_m_kernel(a_ref, ks_ref, tw_ref, fbh_ref, fbl_ref, fch_ref, fcl_ref, o_ref, *, r):
    tr, ti = tw_ref[:, 0:1], tw_ref[:, 1:2]
    x = _dot3(fbh_ref[...], fbl_ref[...], _twiddled(a_ref, 0, tr, ti, r))
    xre, xim = x[:r], x[r:]
    kre, kim = ks_ref[0], ks_ref[1]
    c = _dot3(fch_ref[...], fcl_ref[...],
              jnp.concatenate([xre * kre - xim * kim, xre * kim + xim * kre], axis=0))
    cre, cim = c[:r], c[r:]
    o_ref[0, :r, :] = cre * tr + cim * ti
    o_ref[1, :r, :] = cim * tr - cre * ti
    o_ref[:, r:, :] = jnp.zeros((2,) + (o_ref.shape[1] - r, o_ref.shape[2]), F32)


def _fft_stage_m(a, ks, tw, fbig, fconj, r, pitch):
    b, _, rows, c = a.shape
    a5 = a.reshape(b, 1, 2, rows, c)
    mat = pl.BlockSpec((2 * r, 2 * r), lambda k, bi: (0, 0))
    return pl.pallas_call(
        functools.partial(_fft_m_kernel, r=r),
        out_shape=_sds(a.shape, F32),
        grid=(rows // pitch, b),
        in_specs=[pl.BlockSpec((None, 1, 2, pitch, c), lambda k, bi: (bi, 0, 0, k, 0)),
                  pl.BlockSpec((2, r, c), lambda k, bi: (0, k, 0)),
                  pl.BlockSpec((None, r, 2), lambda k, bi: (k, 0, 0)),
                  mat, mat, mat, mat],
        out_specs=pl.BlockSpec((None, 2, pitch, c), lambda k, bi: (bi, 0, k, 0)),
        compiler_params=_params(("arbitrary", "arbitrary")),
        name="fft_stage_m",
    )(a5, ks, tw, *fbig, *fconj)


def _fft_f_kernel(d_ref, gh_ref, gl_ref, o_ref, *, r, kp, pitch):
    gh, gl = gh_ref[...], gl_ref[...]
    pad = jnp.zeros((pitch - r, LANES), F32)
    for g in range(r // 2):
        o_ref[g * pitch + r:(g + 1) * pitch, :] = pad

    def body(g, carry):
        n2 = g * FFT_GROUP
        dcat = jnp.concatenate(
            [jnp.concatenate([d_ref[0, pl.ds(n2 + k, kp, stride=pitch), :],
                              d_ref[1, pl.ds(n2 + k, kp, stride=pitch), :]], axis=0) for k in range(FFT_GROUP)],
            axis=1)
        y = _dot3(gh, gl, dcat)
        for k in range(FFT_GROUP):
            o_ref[pl.ds(n2 + k, r // 2, stride=pitch), :] = y[:, k * LANES:(k + 1) * LANES]
        return carry

    lax.fori_loop(0, r // FFT_GROUP, body, 0)


def _fft_stage_f(dmat, gfin, r, pitch):
    b, _, rows, c = dmat.shape
    kp = rows // pitch
    out_rows = r // 2 * pitch
    return pl.pallas_call(
        functools.partial(_fft_f_kernel, r=r, kp=kp, pitch=pitch),
        out_shape=_sds((b, out_rows, c), F32),
        grid=(b, c // LANES),
        in_specs=[pl.BlockSpec((None, 2, rows, LANES), lambda bi, ci: (bi, 0, 0, ci)),
                  pl.BlockSpec((r // 2, 2 * kp), lambda bi, ci: (0, 0)),
                  pl.BlockSpec((r // 2, 2 * kp), lambda bi, ci: (0, 0))],
        out_specs=pl.BlockSpec((None, out_rows, LANES), lambda bi, ci: (bi, 0, ci)),
        compiler_params=_params(("arbitrary", "arbitrary")),
        name="fft_stage_f",
    )(dmat, *gfin)


def _fft_radix(length):
    r = int(round(math.sqrt(2 * length)))
    assert r * r == 2 * length, "sequence length must make 2L a perfect square"
    return r, r + 8


def _long_conv(zin, hfb, r, pitch):
    fa_half, fbig, fconj, gfin, tw = _dft_tables(r)
    ks = _fft_filter_spectrum(_fft_stage_a(hfb, fa_half, r, pitch), tw, hfb[1, 0:1, :], fbig, r, pitch)
    a = _fft_stage_a(zin, fa_half, r, pitch)
    dmat = _fft_stage_m(a, ks, tw, fbig, fconj, r, pitch)
    return _fft_stage_f(dmat, gfin, r, pitch)


def _alibi_slopes(n_heads):
    def pow2_slopes(m):
        start = 2.0 ** (-8.0 / m)
        return [start ** (i + 1) for i in range(m)]
    base = 2 ** int(math.floor(math.log2(n_heads)))
    slopes = pow2_slopes(base)
    if base < n_heads:
        slopes = slopes + pow2_slopes(2 * base)[0::2][: n_heads - base]
    return np.array(sorted(slopes, reverse=True), dtype=np.float32)


def _attn_kernel(q_ref, k_ref, kp_ref, kn_ref, v_ref, vp_ref, vn_ref, o_ref, l_ref, *, tq, n, dil, slopes):
    i = pl.program_id(1)
    side = ATTN_SIDE
    nk = tq + 2 * side
    row = lax.broadcasted_iota(jnp.int32, (tq, nk), 0)
    col = lax.broadcasted_iota(jnp.int32, (tq, nk), 1)
    rel = jnp.abs(col - side - row)
    kglob = i * tq + col - side
    valid = (rel <= side) & (kglob >= 0) & (kglob < n)
    dist = (rel * dil).astype(F32)
    scale = HEAD_DIM ** -0.5
    for h in range(HEADS_PER_GROUP):
        hs = slice(h * HEAD_DIM, (h + 1) * HEAD_DIM)
        q = q_ref[:, hs]
        kc = jnp.concatenate([kp_ref[:, hs], k_ref[:, hs], kn_ref[:, hs]], axis=0)
        vc = jnp.concatenate([vp_ref[:, hs], v_ref[:, hs], vn_ref[:, hs]], axis=0)
        s = lax.dot_general(q, kc, (((1,), (1,)), ((), ())), preferred_element_type=F32) * scale
        s = jnp.where(valid, s - float(slopes[h]) * dist, NEG_INF)
        m = jnp.max(s, axis=-1, keepdims=True)
        p = jnp.exp(s - m)
        den = jnp.sum(p, axis=-1, keepdims=True)
        o = jnp.dot(p.astype(BF16), vc, preferred_element_type=F32) / den
        o_ref[:, hs] = o.astype(o_ref.dtype)
        l_ref[:, hs] = jnp.broadcast_to(m + jnp.log(den), (tq, HEAD_DIM))


def _dilated_attention(qkv, n, dil, slopes, col0=0):
    streams = qkv.shape[0]
    tq = min(128, n)
    side = ATTN_SIDE
    nh = n // side
    gw = GROUP_WIDTH
    main = lambda cb: pl.BlockSpec((None, tq, gw), lambda s, i: (s, i, col0 + cb))
    prev = lambda cb: pl.BlockSpec((None, side, gw),
                                   lambda s, i: (s, jnp.maximum(i * (tq // side) - 1, 0), col0 + cb))
    nxt = lambda cb: pl.BlockSpec((None, side, gw),
                                  lambda s, i: (s, jnp.minimum((i + 1) * (tq // side), nh - 1), col0 + cb))
    return pl.pallas_call(
        functools.partial(_attn_kernel, tq=tq, n=n, dil=dil, slopes=tuple(float(v) for v in slopes)),
        out_shape=(_sds((streams, n, gw), BF16), _sds((streams, n, gw), F32)),
        grid=(streams, n // tq),
        in_specs=[main(0), main(1), prev(1), nxt(1), main(2), prev(2), nxt(2)],
        out_specs=(pl.BlockSpec((None, tq, gw), lambda s, i: (s, i, 0)),
                   pl.BlockSpec((None, tq, gw), lambda s, i: (s, i, 0))),
        compiler_params=_params(("arbitrary", "arbitrary")),
        name=f"dilated_attention_d{dil}",
    )(qkv, qkv, qkv, qkv, qkv, qkv, qkv)


def _merge_kernel(y_ref, z_ref, x1_ref, skip_ref, o0_ref, l0_ref, o1_ref, l1_ref, o2_ref, l2_ref,
                  ghy_ref, gat_ref, wh_ref, wa_ref, out_ref, hy_s, at_s, so1, sl1, so2, sl2, *, tm, tn, r, pitch):
    nct = GROUP_WIDTH // LANES
    for g in range(tm // r):
        rows = slice(g * pitch, g * pitch + r)
        hy_s[g * r:(g + 1) * r, :] = ((y_ref[rows, :] + z_ref[rows, :] * skip_ref[...])
                                      * x1_ref[g * r:(g + 1) * r, :]).astype(BF16)
    for dil, oref, lref, so, sl in ((4, o1_ref, l1_ref, so1, sl1), (16, o2_ref, l2_ref, so2, sl2)):
        for res in range(dil):
            for c in range(nct):
                cs = slice(c * LANES, (c + 1) * LANES)
                so[c, pl.ds(res, tm // dil, stride=dil), :] = oref[res, :, cs].astype(F32)
                sl[c, pl.ds(res, tm // dil, stride=dil), :] = lref[res, :, cs]
    for c in range(nct):
        cs = slice(c * LANES, (c + 1) * LANES)
        a0, a1, a2 = l0_ref[:, cs], sl1[c], sl2[c]
        m = jnp.maximum(jnp.maximum(a0, a1), a2)
        e0, e1, e2 = jnp.exp(a0 - m), jnp.exp(a1 - m), jnp.exp(a2 - m)
        at = (e0 * o0_ref[:, cs].astype(F32) + e1 * so1[c] + e2 * so2[c]) / (e0 + e1 + e2)
        at_s[:, cs] = at.astype(BF16)

    hy, at = hy_s[...], at_s[...]
    for j in range(out_ref.shape[1] // tn):
        cols = slice(j * tn, (j + 1) * tn)
        acc_h = jnp.dot(hy, wh_ref[:, cols], preferred_element_type=F32)
        acc_a = jnp.dot(at, wa_ref[:, cols], preferred_element_type=F32)
        out = (jax.nn.sigmoid(ghy_ref[:, cols].astype(F32)) * acc_h
               + jax.nn.sigmoid(gat_ref[:, cols].astype(F32)) * acc_a)
        out_ref[:, cols] = out.astype(out_ref.dtype)


def _merge(yconv, zin, x1c, skip, o0, l0, o1, l1, o2, l2, gates, wh, wa, seq, r, pitch):
    n_tok, hw = x1c.shape
    d_model = wh.shape[1]
    gw = GROUP_WIDTH
    tm = 256
    spb = seq // tm
    row = lambda width: pl.BlockSpec((tm, width), lambda i: (i, 0))
    prow = pl.BlockSpec((tm // r * pitch, hw), lambda i: (i, 0))
    res = lambda dil: pl.BlockSpec((None, dil, tm // dil, gw), lambda i: (i // spb, 0, i % spb, 0))
    resident = lambda rows: pl.BlockSpec((rows, d_model), lambda i: (0, 0), pipeline_mode=pl.Buffered(1))
    return pl.pallas_call(
        functools.partial(_merge_kernel, tm=tm, tn=COL_BLOCK, r=r, pitch=pitch),
        out_shape=_sds((n_tok, d_model), BF16),
        grid=(n_tok // tm,),
        in_specs=[prow, prow, row(hw), pl.BlockSpec((1, hw), lambda i: (0, 0)),
                  row(gw), row(gw), res(4), res(4), res(16), res(16),
                  pl.BlockSpec((tm, d_model), lambda i: (i, 0)),
                  pl.BlockSpec((tm, d_model), lambda i: (i, 1)),
                  resident(hw), resident(gw)],
        out_specs=pl.BlockSpec((tm, d_model), lambda i: (i, 0)),
        scratch_shapes=[pltpu.VMEM((tm, hw), BF16), pltpu.VMEM((tm, gw), BF16)]
                       + [pltpu.VMEM((gw // LANES, tm, LANES), F32)] * 4,
        compiler_params=_params(("arbitrary",)),
        name="gated_merge",
    )(yconv, zin, x1c, skip, o0, l0, o1, l1, o2, l2, gates, gates, wh, wa)


def _router_kernel(mo_ref, x_ref, gm_ref, gpost_ref, gpre_ref, sc_ref, sh_ref, wr_ref, br_ref,
                   x1_ref, h2_ref, idx_ref, tw_ref, *, ts, d_model, n_experts):
    mo = mo_ref[...]
    y = mo * lax.rsqrt(jnp.mean(mo * mo, axis=-1, keepdims=True) + RMS_EPS) * gpost_ref[...]
    x1 = x_ref[...] + gm_ref[...] * y
    x1_ref[...] = x1
    h2 = (x1 * lax.rsqrt(jnp.mean(x1 * x1, axis=-1, keepdims=True) + RMS_EPS) * gpre_ref[...]
          * (1.0 + sc_ref[...]) + sh_ref[...])
    nct = d_model // 2 // LANES
    sp = nct + SLAB_PAD
    words = _pack_bf16_pairs(h2)
    for c in range(nct):
        h2_ref[pl.ds(c, ts, stride=sp), :] = words[:, c * LANES:(c + 1) * LANES]
    for c in range(nct, sp):
        h2_ref[pl.ds(c, ts, stride=sp), :] = jnp.zeros((ts, LANES), jnp.uint32)
    logits = jnp.dot(h2, wr_ref[...], precision=HIGHEST, preferred_element_type=F32) + br_ref[...]
    lane = lax.broadcasted_iota(jnp.int32, logits.shape, 1)
    lane_f = lane.astype(F32)
    logits = jnp.where(lane < n_experts, logits, -jnp.inf)
    idx_out = jnp.zeros(logits.shape, jnp.int32)
    val_out = jnp.zeros(logits.shape, F32)
    top0 = None
    den = None
    for k in range(TOP_K):
        m = jnp.max(logits, axis=-1, keepdims=True)
        idx = jnp.min(jnp.where(logits == m, lane_f, float(LANES)), axis=-1, keepdims=True).astype(jnp.int32)
        if k == 0:
            top0 = m
        e = jnp.exp(m - top0)
        den = e if den is None else den + e
        idx_out = jnp.where(lane == k, idx, idx_out)
        val_out = jnp.where(lane == k, e, val_out)
        logits = jnp.where(lane == idx, -jnp.inf, logits)
    idx_ref[...] = idx_out
    tw_ref[...] = val_out / den


def _post_mix_and_route(mo, x, gate_m, g_post, g_pre, scale_f, shift_f, w_router, b_router):
    b, s, d = x.shape
    e = w_router.shape[1]
    ts = 256
    wr = jnp.zeros((d, LANES), F32).at[:, :e].set(w_router)
    br = jnp.zeros((1, LANES), F32).at[0, :e].set(b_router)
    sp = d // 2 // LANES + SLAB_PAD
    spb = s // ts
    rowblk = lambda width: pl.BlockSpec((ts, width), lambda i: (i, 0))
    per_batch = pl.BlockSpec((None, 1, d), lambda i: (i // spb, 0, 0))
    vec = pl.BlockSpec((1, d), lambda i: (0, 0))
    n_tok = b * s
    return pl.pallas_call(
        functools.partial(_router_kernel, ts=ts, d_model=d, n_experts=e),
        out_shape=(_sds((n_tok, d), F32), _sds((n_tok * sp, LANES), jnp.uint32),
                   _sds((n_tok, LANES), jnp.int32), _sds((n_tok, LANES), F32)),
        grid=(n_tok // ts,),
        in_specs=[rowblk(d), rowblk(d), per_batch, vec, vec, per_batch, per_batch,
                  pl.BlockSpec((d, LANES), lambda i: (0, 0)), pl.BlockSpec((1, LANES), lambda i: (0, 0))],
        out_specs=(rowblk(d), pl.BlockSpec((ts * sp, LANES), lambda i: (i, 0)), rowblk(LANES), rowblk(LANES)),
        compiler_params=_params(("arbitrary",)),
        name="post_mix_route",
    )(mo, x.reshape(n_tok, d), gate_m, g_post, g_pre, scale_f, shift_f, wr, br)


W_CHUNKS = 8


def _weight_stream(w_hbms, stage, wbuf, wsem, rc):
    n = len(w_hbms)

    def copies(e, c, st):
        rows = pl.ds(pl.multiple_of(c * rc, rc), rc)
        return [pltpu.make_async_copy(w.at[e, rows, :], stage.at[st, m], wsem.at[st]) for m, w in enumerate(w_hbms)]

    def start(e, c, st):
        for cp in copies(e, c, st):
            cp.start()

    def finish(e, lo, hi, half):
        def body(c, carry):
            st = c % 2
            for cp in copies(e, c, st):
                cp.wait()
            rows = pl.ds(pl.multiple_of(c * rc, rc), rc)
            for m in range(n):
                wbuf[n * half + m, rows, :] = stage[st, m].astype(BF16)

            @pl.when(c + 1 < W_CHUNKS)
            def _():
                start(e, c + 1, 1 - st)
            return carry
        lax.fori_loop(lo, hi, body, 0)

    return start, finish


def _stream_weights_step(i, n_used, be_ref, sch_ref, start, finish):
    half = sch_ref[5 * i]

    @pl.when((i == 0) & (n_used > 0))
    def _():
        start(be_ref[0], 0, 0)
        finish(be_ref[0], 0, W_CHUNKS, half)

    @pl.when(i < n_used)
    def _():
        nxt = sch_ref[5 * i + 1]

        @pl.when(sch_ref[5 * i + 2] == 1)
        def _():
            start(nxt, 0, 0)

        finish(nxt, sch_ref[5 * i + 3], sch_ref[5 * i + 4], 1 - half)

    return half


def _expert_up_kernel(be_ref, nu_ref, sch_ref, tok0_ref, tok1_ref, h2_hbm, wg_hbm, bg_ref, wu_hbm, bu_ref, act_ref,
                      xbuf, xb, wbuf, stage, sem, wsem, *, tb, nct, rc):
    i = pl.program_id(0)
    n_used = nu_ref[0]
    sp = nct + SLAB_PAD

    def row_copy(t, j, slot):
        return pltpu.make_async_copy(h2_hbm.at[pl.ds(pl.multiple_of(t * sp, 8), nct), :],
                                     xbuf.at[slot, pl.ds(pl.multiple_of(j * sp, 8), nct), :], sem.at[slot])

    def gather(tok_ref, slot):
        def body(j, carry):
            row_copy(tok_ref[0, j], j, slot).start()
            return carry
        lax.fori_loop(0, tb, body, 0, unroll=8)

    def wait(slot):
        pltpu.make_async_copy(h2_hbm.at[pl.ds(0, tb * nct), :], xbuf.at[slot, pl.ds(0, tb * nct), :],
                              sem.at[slot]).wait()

    @pl.when((i == 0) & (n_used > 0))
    def _():
        gather(tok0_ref, 0)

    start, finish = _weight_stream((wg_hbm, wu_hbm), stage, wbuf, wsem, rc)
    half = _stream_weights_step(i, n_used, be_ref, sch_ref, start, finish)

    @pl.when(i < n_used)
    def _():
        slot = i % 2
        wait(slot)
        for j in range(tb):
            row_copy(tok1_ref[0, j], j, 1 - slot).start()
        for c in range(nct):
            hi, lo = _unpack_bf16_pairs(xbuf[slot, pl.ds(c, tb, stride=sp), :])
            xb[:, 2 * c * LANES:(2 * c + 1) * LANES] = hi.astype(BF16)
            xb[:, (2 * c + 1) * LANES:(2 * c + 2) * LANES] = lo.astype(BF16)
        x = xb[...]
        g = jnp.dot(x, wbuf[2 * half], preferred_element_type=F32) + bg_ref[...]
        u = jnp.dot(x, wbuf[2 * half + 1], preferred_element_type=F32) + bu_ref[...]
        g = jnp.minimum(g, SWIGLU_LIMIT)
        u = jnp.clip(u, -SWIGLU_LIMIT, SWIGLU_LIMIT)
        act_ref[...] = (g * jax.nn.sigmoid(SWIGLU_ALPHA * g) * (u + 1.0)).astype(act_ref.dtype)

    @pl.when((i == n_used) & (n_used > 0))
    def _():
        wait(i % 2)

    @pl.when(i >= n_used)
    def _():
        act_ref[...] = jnp.zeros(act_ref.shape, act_ref.dtype)


def _expert_up(block_e, n_used, sched, row_tok3, h2s, wg, bg, wu, bu, nct):
    n_blocks, _, tb = row_tok3.shape
    _, d, f = wg.shape
    rc = d // W_CHUNKS
    grid_spec = pltpu.PrefetchScalarGridSpec(
        num_scalar_prefetch=3,
        grid=(n_blocks,),
        in_specs=[pl.BlockSpec((None, 1, tb), lambda i, be, nu, sc: (i, 0, 0), memory_space=pltpu.SMEM),
                  pl.BlockSpec((None, 1, tb), lambda i, be, nu, sc: (jnp.minimum(i + 1, n_blocks - 1), 0, 0),
                               memory_space=pltpu.SMEM),
                  pl.BlockSpec(memory_space=pl.ANY),
                  pl.BlockSpec(memory_space=pl.ANY),
                  pl.BlockSpec((None, 1, f), lambda i, be, nu, sc: (be[i], 0, 0)),
                  pl.BlockSpec(memory_space=pl.ANY),
                  pl.BlockSpec((None, 1, f), lambda i, be, nu, sc: (be[i], 0, 0))],
        out_specs=pl.BlockSpec((tb, f), lambda i, be, nu, sc: (i, 0)),
        scratch_shapes=[pltpu.VMEM((2, tb * (nct + SLAB_PAD), LANES), jnp.uint32), pltpu.VMEM((tb, d), BF16),
                        pltpu.VMEM((4, d, f), BF16), pltpu.VMEM((2, 2, rc, f), F32),
                        pltpu.SemaphoreType.DMA((2,)), pltpu.SemaphoreType.DMA((2,))],
    )
    return pl.pallas_call(
        functools.partial(_expert_up_kernel, tb=tb, nct=nct, rc=rc),
        out_shape=_sds((n_blocks * tb, f), BF16),
        grid_spec=grid_spec,
        compiler_params=_params(("arbitrary",)),
        name="expert_up",
    )(block_e, n_used, sched, row_tok3, row_tok3, h2s, wg, bg, wu, bu)


def _expert_down_kernel(be_ref, nu_ref, sch_ref, act_ref, wd_hbm, bd_ref, ys_ref, wbuf, stage, wsem, *, tb, nct, rc):
    i = pl.program_id(0)
    start, finish = _weight_stream((wd_hbm,), stage, wbuf, wsem, rc)
    half = _stream_weights_step(i, nu_ref[0], be_ref, sch_ref, start, finish)

    @pl.when(i < nu_ref[0])
    def _():
        act = act_ref[...]
        sp = nct + SLAB_PAD
        for c in range(nct):
            cols = slice(2 * c * LANES, (2 * c + 2) * LANES)
            y = jnp.dot(act, wbuf[half, :, cols], preferred_element_type=F32) + bd_ref[:, cols]
            ys_ref[pl.ds(c, tb, stride=sp), :] = _pack_bf16_pairs(y)
        for c in range(nct, sp):
            ys_ref[pl.ds(c, tb, stride=sp), :] = jnp.zeros((tb, LANES), jnp.uint32)

    @pl.when(i >= nu_ref[0])
    def _():
        ys_ref[...] = jnp.zeros(ys_ref.shape, ys_ref.dtype)


def _expert_down(block_e, n_used, sched, act, wd, bd, tb):
    _, f, d = wd.shape
    n_blocks = act.shape[0] // tb
    nct = d // 2 // LANES
    rc = f // W_CHUNKS
    grid_spec = pltpu.PrefetchScalarGridSpec(
        num_scalar_prefetch=3,
        grid=(n_blocks,),
        in_specs=[pl.BlockSpec((tb, f), lambda i, be, nu, sc: (i, 0)),
                  pl.BlockSpec(memory_space=pl.ANY),
                  pl.BlockSpec((None, 1, d), lambda i, be, nu, sc: (be[i], 0, 0))],
        out_specs=pl.BlockSpec((tb * (nct + SLAB_PAD), LANES), lambda i, be, nu, sc: (i, 0)),
        scratch_shapes=[pltpu.VMEM((2, f, d), BF16), pltpu.VMEM((2, 1, rc, d), F32),
                        pltpu.SemaphoreType.DMA((2,))],
    )
    return pl.pallas_call(
        functools.partial(_expert_down_kernel, tb=tb, nct=nct, rc=rc),
        out_shape=_sds((n_blocks * tb * (nct + SLAB_PAD), LANES), jnp.uint32),
        grid_spec=grid_spec,
        compiler_params=_params(("arbitrary",)),
        name="expert_down",
    )(block_e, n_used, sched, act, wd, bd)


def _combine_kernel(d0_ref, d1_ref, ys_hbm, tw_ref, x1_ref, gf_ref, gpost_ref, o_ref, buf, ff, sem, *, tc, nct):
    i = pl.program_id(0)
    last = pl.num_programs(0) - 1
    n_rows = TOP_K * tc
    sp = nct + SLAB_PAD

    def row_copy(r, j, slot):
        return pltpu.make_async_copy(ys_hbm.at[pl.ds(pl.multiple_of(r * sp, 8), nct), :],
                                     buf.at[slot, pl.ds(pl.multiple_of(j * sp, 8), nct), :], sem.at[slot])

    def gather(dref, slot):
        def body(j, carry):
            row_copy(dref[0, j], j, slot).start()
            return carry
        lax.fori_loop(0, n_rows, body, 0, unroll=8)

    def wait(slot):
        pltpu.make_async_copy(ys_hbm.at[pl.ds(0, n_rows * nct), :], buf.at[slot, pl.ds(0, n_rows * nct), :],
                              sem.at[slot]).wait()

    @pl.when(i == 0)
    def _():
        gather(d0_ref, 0)

    slot = i % 2
    wait(slot)
    for j in range(n_rows):
        row_copy(d1_ref[0, j], j, 1 - slot).start()
    wk =[tw_ref[:, k:k + 1] for k in range(TOP_K)]
    for c in range(nct):
        acc_hi = acc_lo = None
        for k in range(TOP_K):
            hi, lo = _unpack_bf16_pairs(buf[slot, pl.ds(k * tc * sp + c, tc, stride=sp), :])
            acc_hi = wk[k] * hi if acc_hi is None else acc_hi + wk[k] * hi
            acc_lo = wk[k] * lo if acc_lo is None else acc_lo + wk[k] * lo
        ff[:, 2 * c * LANES:(2 * c + 1) * LANES] = acc_hi
        ff[:, (2 * c + 1) * LANES:(2 * c + 2) * LANES] = acc_lo
    f = ff[...]
    y = f * lax.rsqrt(jnp.mean(f * f, axis=-1, keepdims=True) + RMS_EPS) * gpost_ref[...]
    o_ref[...] = x1_ref[...] + gf_ref[...] * y

    @pl.when(i == last)
    def _():
        wait(1 - slot)


def _combine(dest3, ys, top_w, x1, gate_f, g_post, seq):
    n_tok, d = x1.shape
    n_steps, _, n_rows = dest3.shape
    tc = n_rows // TOP_K
    nct = d // 2 // LANES
    spb = seq // tc
    return pl.pallas_call(
        functools.partial(_combine_kernel, tc=tc, nct=nct),
        out_shape=_sds((n_tok, d), F32),
        grid=(n_steps,),
        in_specs=[pl.BlockSpec((None, 1, n_rows), lambda i: (i, 0, 0), memory_space=pltpu.SMEM),
                  pl.BlockSpec((None, 1, n_rows), lambda i: (jnp.minimum(i + 1, n_steps - 1), 0, 0),
                               memory_space=pltpu.SMEM),
                  pl.BlockSpec(memory_space=pl.ANY),
                  pl.BlockSpec((tc, LANES), lambda i: (i, 0)),
                  pl.BlockSpec((tc, d), lambda i: (i, 0)),
                  pl.BlockSpec((None, 1, d), lambda i: (i // spb, 0, 0)),
                  pl.BlockSpec((1, d), lambda i: (0, 0))],
        out_specs=pl.BlockSpec((tc, d), lambda i: (i, 0)),
        scratch_shapes=[pltpu.VMEM((2, n_rows * (nct + SLAB_PAD), LANES), jnp.uint32), pltpu.VMEM((tc, d), F32),
                        pltpu.SemaphoreType.DMA((2,))],
        compiler_params=_params(("arbitrary",)),
        name="expert_combine",
    )(dest3, dest3, ys, top_w, x1, gate_f, g_post)


def _rank_kernel(idx_ref, tri_ref, rank_ref, cnt_ref, carry, *, tr):
    @pl.when(pl.program_id(0) == 0)
    def _():
        carry[...] = jnp.zeros(carry.shape, F32)

    idx = idx_ref[...]
    lane = lax.broadcasted_iota(jnp.int32, idx.shape, 1)
    base = carry[0:1, :]
    out = jnp.zeros(idx.shape, jnp.int32)
    for k in range(TOP_K):
        onehot = jnp.where(lane == idx[:, k:k + 1], 1.0, 0.0)
        csum = jnp.dot(tri_ref[...], onehot.astype(BF16), preferred_element_type=F32)
        rank = jnp.sum(onehot * (csum + base), axis=-1, keepdims=True) - 1.0
        out = jnp.where(lane == k, rank.astype(jnp.int32), out)
        base = base + csum[tr - 1:tr, :]
    rank_ref[...] = out
    carry[...] = jnp.broadcast_to(base, carry.shape)
    cnt_ref[...] = jnp.broadcast_to(base, cnt_ref.shape).astype(jnp.int32)


def _expert_ranks(top_idx_padded):
    n_tok = top_idx_padded.shape[0]
    tr = 512
    tri = jnp.asarray(np.tril(np.ones((tr, tr), np.float32)), BF16)
    return pl.pallas_call(
        functools.partial(_rank_kernel, tr=tr),
        out_shape=(_sds((n_tok, LANES), jnp.int32), _sds((8, LANES), jnp.int32)),
        grid=(n_tok // tr,),
        in_specs=[pl.BlockSpec((tr, LANES), lambda i: (i, 0)), pl.BlockSpec((tr, tr), lambda i: (0, 0))],
        out_specs=(pl.BlockSpec((tr, LANES), lambda i: (i, 0)), pl.BlockSpec((8, LANES), lambda i: (0, 0))),
        scratch_shapes=[pltpu.VMEM((8, LANES), F32)],
        compiler_params=_params(("arbitrary",)),
        name="expert_ranks",
    )(top_idx_padded, tri)


def _routing_tables(top_idx_padded, n_experts, tb):
    n_tok = top_idx_padded.shape[0]
    n_assign = n_tok * TOP_K
    ranks, counts = _expert_ranks(top_idx_padded)
    sizes = counts[0, :n_experts]
    padded = (sizes + tb - 1) // tb * tb
    pad_end = jnp.cumsum(padded)
    pad_start = pad_end - padded
    top_idx = top_idx_padded[:, :TOP_K]
    experts = jnp.arange(n_experts, dtype=jnp.int32)
    start_of = jnp.sum(jnp.where(top_idx[:, :, None] == experts, pad_start, 0), axis=-1)
    dest = (start_of + ranks[:, :TOP_K]).astype(jnp.int32).reshape(-1)
    n_rows = -(-n_assign // tb) * tb + n_experts * tb
    n_blocks = n_rows // tb
    tok = (jnp.arange(n_assign, dtype=jnp.int32) // TOP_K)
    row_tok = jnp.zeros((n_rows,), jnp.int32).at[dest].set(tok)
    block_start = jnp.arange(n_blocks, dtype=jnp.int32) * tb
    block_e = jnp.minimum(jnp.sum((pad_end[None, :] <= block_start[:, None]).astype(jnp.int32), axis=1),
                          n_experts - 1).astype(jnp.int32)
    n_used = (pad_end[-1] // tb).astype(jnp.int32).reshape(1)
    nb = padded // tb
    occupied = nb > 0
    ordinal = jnp.cumsum(occupied.astype(jnp.int32)) - 1
    later = lax.cummin(jnp.where(occupied, experts, n_experts), reverse=True)
    next_e = jnp.concatenate([later[1:], jnp.full((1,), n_experts, jnp.int32)])
    blk = jnp.arange(n_blocks, dtype=jnp.int32)
    k = blk - (pad_start // tb)[block_e]
    nbe = nb[block_e]
    active = (blk < n_used[0]) & (next_e[block_e] < n_experts)
    zero = jnp.zeros_like(blk)
    spread = jnp.maximum(nbe - 1, 1)
    lo = jnp.where(nbe > 1, jnp.maximum(k - 1, 0) * W_CHUNKS // spread, 0)
    hi = jnp.where(nbe > 1, k * W_CHUNKS // spread, W_CHUNKS)
    sched = jnp.stack([ordinal[block_e] % 2,
                       jnp.where(active, next_e[block_e], zero),
                       jnp.where(active & (k == 0), 1, zero),
                       jnp.where(active, lo, zero),
                       jnp.where(active, hi, zero)], axis=1).reshape(-1).astype(jnp.int32)
    return dest, row_tok, block_e, n_used, sched, n_rows


def _moe(h2s, top_idx, top_w, wg, bg, wu, bu, wd, bd, x1, gate_f, g_post, seq):
    n_tok, d = x1.shape
    n_experts = wg.shape[0]
    tb = MOE_ROWS
    nct = d // 2 // LANES
    dest, row_tok, block_e, n_used, sched, n_rows = _routing_tables(top_idx, n_experts, tb)
    n_blocks = n_rows // tb
    act = _expert_up(block_e, n_used, sched, row_tok.reshape(n_blocks, 1, tb), h2s, wg, bg[:, None, :], wu,
                     bu[:, None, :], nct)
    ys = _expert_down(block_e, n_used, sched, act, wd, bd[:, None, :], tb)
    tc = 128
    dest3 = dest.reshape(n_tok // tc, tc, TOP_K).transpose(0, 2, 1).reshape(n_tok // tc, 1, TOP_K * tc)
    return _combine(dest3, ys, top_w, x1, gate_f, g_post, seq)


def _layer(x, c8, p):
    b, s, d = x.shape
    n_tok = b * s
    hw = p["hy_skip"].shape[0]
    gw = GROUP_WIDTH
    assert hw == COL_BLOCK and gw == COL_BLOCK and d % COL_BLOCK == 0

    mod = _adaln(c8, p["w_ada"], p["b_ada"][None])[:b]
    shift_m, scale_m, gate_m, shift_f, scale_f, gate_f = [m[:, None, :] for m in jnp.split(mod, 6, axis=-1)]

    h, h4, h16 = _prenorm_mix(x, p["g_pre_mix"][None], scale_m, shift_m)

    w_in = p["w_in"].astype(BF16)
    nd = d // COL_BLOCK
    pn = _matmul(h.reshape(n_tok, d), w_in, [0, 1, 2, 3, 6, 9], BF16, "in_proj_natural")
    gates = _matmul(h.reshape(n_tok, d), w_in, list(range(12, 12 + 2 * nd)), BF16, "in_proj_gates")
    qkv1 = _matmul(h4.reshape(n_tok, d), w_in, [4, 7, 10], BF16, "in_proj_dil4")
    qkv2 = _matmul(h16.reshape(n_tok, d), w_in, [5, 8, 11], BF16, "in_proj_dil16")

    r, pitch = _fft_radix(s)
    hfb = _hyena_filters(s, r, pitch, p["hy_f_w1"], p["hy_f_b1"], p["hy_f_w2"], p["hy_f_b2"], p["hy_f_w3"],
                         p["hy_f_b3"], p["hy_f_freq"], p["hy_f_wout"])
    zin, x1c = _shortconv(pn.reshape(b, s, -1), p["hy_conv_w"], p["hy_conv_b"], hw, r, pitch)
    yconv = _long_conv(zin, hfb, r, pitch)

    slopes = _alibi_slopes(N_GROUPS * HEADS_PER_GROUP).reshape(N_GROUPS, HEADS_PER_GROUP)
    o0, l0 = _dilated_attention(pn.reshape(b, s, -1), s, 1, slopes[0], col0=3)
    o1, l1 = _dilated_attention(qkv1.reshape(b * 4, s // 4, 3 * gw), s // 4, 4, slopes[1])
    o2, l2 = _dilated_attention(qkv2.reshape(b * 16, s // 16, 3 * gw), s // 16, 16, slopes[2])

    merged = _merge(yconv.reshape(-1, hw), zin.reshape(-1, hw), x1c.reshape(n_tok, hw), p["hy_skip"][None],
                    o0.reshape(n_tok, gw), l0.reshape(n_tok, gw),
                    o1.reshape(b, 4, s // 4, gw), l1.reshape(b, 4, s // 4, gw),
                    o2.reshape(b, 16, s // 16, gw), l2.reshape(b, 16, s // 16, gw),
                    gates, p["w_proj_hyena"].astype(BF16), p["w_proj_attn"].astype(BF16), s, r, pitch)
    mo = _matmul(merged, p["w_out"].astype(BF16), list(range(nd)), F32, "out_proj")

    x1, h2s, top_idx, top_w = _post_mix_and_route(mo, x, gate_m, p["g_post_mix"][None], p["g_pre_ffn"][None],
                                                  scale_f, shift_f, p["w_router"], p["b_router"])
    out = _moe(h2s, top_idx, top_w, p["w_gate"], p["b_gate"], p["w_up"], p["b_up"], p["w_down"], p["b_down"], x1, gate_f,
               p["g_post_ffn"][None], s)
    return out.reshape(b, s, d)


def kernel(x, c, w_ada, b_ada, g_pre_mix, g_post_mix, g_pre_ffn, g_post_ffn, w_in, hy_conv_w, hy_conv_b, hy_skip, hy_f_w1, hy_f_b1, hy_f_w2, hy_f_b2, hy_f_w3, hy_f_b3, hy_f_freq, hy_f_wout, w_proj_hyena, w_proj_attn, w_out, w_router, b_router, w_gate, b_gate, w_up, b_up, w_down, b_down):
    names = ("w_ada", "b_ada", "g_pre_mix", "g_post_mix", "g_pre_ffn", "g_post_ffn", "w_in", "hy_conv_w",
             "hy_conv_b", "hy_skip", "hy_f_w1", "hy_f_b1", "hy_f_w2", "hy_f_b2", "hy_f_w3", "hy_f_b3",
             "hy_f_freq", "hy_f_wout", "w_proj_hyena", "w_proj_attn", "w_out", "w_router", "b_router",
             "w_gate", "b_gate", "w_up", "b_up", "w_down", "b_down")
    stacked = (w_ada, b_ada, g_pre_mix, g_post_mix, g_pre_ffn, g_post_ffn, w_in, hy_conv_w, hy_conv_b, hy_skip,
               hy_f_w1, hy_f_b1, hy_f_w2, hy_f_b2, hy_f_w3, hy_f_b3, hy_f_freq, hy_f_wout, w_proj_hyena,
               w_proj_attn, w_out, w_router, b_router, w_gate, b_gate, w_up, b_up, w_down, b_down)
    depth = w_ada.shape[0]
    b = x.shape[0]
    c8 = jnp.zeros((8, c.shape[1]), F32).at[:b].set(c)
    for l in range(depth):
        x = _layer(x, c8, {k: v[l] for k, v in zip(names, stacked)})
    return x
```

```python
import functools
import math

import jax
import jax.numpy as jnp
import numpy as np
from jax import lax
from jax.experimental import pallas as pl
from jax.experimental.pallas import tpu as pltpu

F32 = jnp.float32
BF16 = jnp.bfloat16
HIGHEST = lax.Precision.HIGHEST

LANES = 128
HEAD_DIM = 128
HEADS_PER_GROUP = 8
DILATED_GROUPS = ((128, 1), (512, 4), (2048, 16))
N_GROUPS = len(DILATED_GROUPS)
GROUP_WIDTH = HEADS_PER_GROUP * HEAD_DIM
ATTN_SIDE = 64
TOP_K = 4
SWIGLU_LIMIT = 7.0
SWIGLU_ALPHA = 1.702
RMS_EPS = 1e-6
NEG_INF = -1e30
HYENA_N_BANDS = 16
HYENA_DECAY_TARGET = 1e-2
HYENA_FAST_DECAY_PCT = 0.3
HYENA_SLOW_DECAY_PCT = 1.5
COL_BLOCK = 1024
MOE_ROWS = 256
SLAB_PAD = 8
VMEM_LIMIT = 56 * 1024 * 1024


def _pack_bf16_pairs(x):
    bits = lax.bitcast_convert_type(x.astype(BF16).astype(F32), jnp.uint32)
    groups = [bits[:, j:j + LANES] | (bits[:, j + LANES:j + 2 * LANES] >> 16)
              for j in range(0, x.shape[1], 2 * LANES)]
    return groups[0] if len(groups) == 1 else jnp.concatenate(groups, axis=1)


def _unpack_bf16_pairs(w):
    hi = lax.bitcast_convert_type(w & jnp.uint32(0xFFFF0000), F32)
    lo = lax.bitcast_convert_type(w << 16, F32)
    return hi, lo


def _params(sem, vmem=VMEM_LIMIT):
    return pltpu.CompilerParams(dimension_semantics=sem, vmem_limit_bytes=vmem)


def _sds(shape, dtype):
    return jax.ShapeDtypeStruct(shape, dtype)


def _ada_kernel(c_ref, w_ref, b_ref, o_ref):
    c = c_ref[...]
    sc = (c * jax.nn.sigmoid(c)).astype(BF16)
    o_ref[...] = jnp.dot(sc, w_ref[...].astype(BF16), preferred_element_type=F32) + b_ref[...]


def _adaln(c8, w_ada, b_ada):
    d, cols = w_ada.shape
    tn = 512
    return pl.pallas_call(
        _ada_kernel,
        out_shape=_sds((8, cols), F32),
        grid=(cols // tn,),
        in_specs=[pl.BlockSpec((8, d), lambda j: (0, 0)),
                  pl.BlockSpec((d, tn), lambda j: (0, j)),
                  pl.BlockSpec((1, tn), lambda j: (0, j))],
        out_specs=pl.BlockSpec((8, tn), lambda j: (0, j)),
        compiler_params=_params(("arbitrary",)),
        name="adaln",
    )(c8, w_ada, b_ada)


def _prenorm_kernel(x_ref, g_ref, sc_ref, sh_ref, o_ref, o4_ref, o16_ref, scr_ref, *, ts, d_model):
    x = x_ref[...]
    ms = jnp.mean(x * x, axis=-1, keepdims=True)
    h = x * lax.rsqrt(ms + RMS_EPS) * g_ref[...] * (1.0 + sc_ref[...]) + sh_ref[...]
    o_ref[...] = h.astype(BF16)
    nct = d_model // LANES
    for c in range(nct):
        scr_ref[c] = h[:, c * LANES:(c + 1) * LANES]
    for dil, oref in ((4, o4_ref), (16, o16_ref)):
        for r in range(dil):
            for c in range(nct):
                oref[r, :, c * LANES:(c + 1) * LANES] = scr_ref[c, pl.ds(r, ts // dil, stride=dil), :].astype(BF16)


def _prenorm_mix(x, g, scale, shift):
    b, s, d = x.shape
    ts = 256
    kern = functools.partial(_prenorm_kernel, ts=ts, d_model=d)
    return pl.pallas_call(
        kern,
        out_shape=(_sds((b, s, d), BF16), _sds((b, 4, s // 4, d), BF16), _sds((b, 16, s // 16, d), BF16)),
        grid=(b, s // ts),
        in_specs=[pl.BlockSpec((None, ts, d), lambda bi, i: (bi, i, 0)),
                  pl.BlockSpec((1, d), lambda bi, i: (0, 0)),
                  pl.BlockSpec((None, 1, d), lambda bi, i: (bi, 0, 0)),
                  pl.BlockSpec((None, 1, d), lambda bi, i: (bi, 0, 0))],
        out_specs=(pl.BlockSpec((None, ts, d), lambda bi, i: (bi, i, 0)),
                   pl.BlockSpec((None, 4, ts // 4, d), lambda bi, i: (bi, 0, i, 0)),
                   pl.BlockSpec((None, 16, ts // 16, d), lambda bi, i: (bi, 0, i, 0))),
        scratch_shapes=[pltpu.VMEM((d // LANES, ts, LANES), F32)],
        compiler_params=_params(("arbitrary", "arbitrary")),
        name="prenorm_mix",
    )(x, g, scale, shift)


def _mm_kernel(tbl_ref, a_ref, w_ref, o_ref):
    del tbl_ref
    o_ref[...] = jnp.dot(a_ref[...], w_ref[...], preferred_element_type=F32).astype(o_ref.dtype)


def _matmul(a, w, col_blocks, out_dtype, name, tm=1024, tn=COL_BLOCK):
    m, k = a.shape
    tm = min(tm, m)
    nb = len(col_blocks)
    tbl = jnp.asarray(col_blocks, jnp.int32)
    grid_spec = pltpu.PrefetchScalarGridSpec(
        num_scalar_prefetch=1,
        grid=(m // tm, nb),
        in_specs=[pl.BlockSpec((tm, k), lambda i, j, t: (i, 0)),
                  pl.BlockSpec((k, tn), lambda i, j, t: (0, t[j]))],
        out_specs=pl.BlockSpec((tm, tn), lambda i, j, t: (i, j)),
    )
    return pl.pallas_call(
        _mm_kernel,
        out_shape=_sds((m, nb * tn), out_dtype),
        grid_spec=grid_spec,
        compiler_params=_params(("arbitrary", "arbitrary")),
        name=name,
    )(tbl, a, w)


def _filter_kernel(z_ref, w1_ref, b1_ref, w2_ref, b2_ref, w3_ref, b3_ref, fr_ref, wo_ref, dl_ref,
                   h_ref, *, hw, r, pitch):
    def dot(a, b):
        return jnp.dot(a, b, precision=HIGHEST, preferred_element_type=F32)

    z = z_ref[...]
    fr = fr_ref[...]
    h = jnp.sin(fr * (dot(z, w1_ref[...]) + b1_ref[...]))
    h = jnp.sin(fr * (dot(h, w2_ref[...]) + b2_ref[...]))
    h = jnp.sin(fr * (dot(h, w3_ref[...]) + b3_ref[...]))
    filt = dot(h, wo_ref[...])
    decay = jnp.exp(-z[:, 0:1] * dl_ref[...])
    hf = filt[:, :hw] * decay
    hb = filt[:, hw:] * decay
    pad = jnp.zeros((pitch - r, hw), F32)
    for g in range(z.shape[0] // r):
        h_ref[0, g * pitch:g * pitch + r, :] = hf[g * r:(g + 1) * r]
        h_ref[1, g * pitch:g * pitch + r, :] = hb[g * r:(g + 1) * r]
        h_ref[0, g * pitch + r:(g + 1) * pitch, :] = pad
        h_ref[1, g * pitch + r:(g + 1) * pitch, :] = pad


def _hyena_filters(length, r, pitch, w1, b1, w2, b2, w3, b3, freq, wout):
    emb, fw = w1.shape
    hw = wout.shape[1] // 2
    t = np.linspace(0.0, 1.0, length)[:, None]
    bands = np.linspace(1e-4, HYENA_N_BANDS - 1, HYENA_N_BANDS)[None, :]
    ang = (2.0 * math.pi / length) * np.arange(length)[:, None] * bands
    z = np.concatenate([t, np.cos(ang), -np.sin(ang)], axis=-1)
    zpad = np.zeros((length, LANES), np.float32)
    zpad[:, :emb] = z
    w1p = jnp.zeros((LANES, fw), F32).at[:emb].set(w1)
    min_decay = math.log(HYENA_DECAY_TARGET) / HYENA_FAST_DECAY_PCT
    max_decay = math.log(HYENA_DECAY_TARGET) / HYENA_SLOW_DECAY_PCT
    deltas = np.abs(np.linspace(min_decay, max_decay, hw))[None, :].astype(np.float32)
    tl = min(1024, length)
    full = lambda shape: pl.BlockSpec(shape, lambda i: (0,) * len(shape))
    return pl.pallas_call(
        functools.partial(_filter_kernel, hw=hw, r=r, pitch=pitch),
        out_shape=_sds((2, length // r * pitch, hw), F32),
        grid=(length // tl,),
        in_specs=[pl.BlockSpec((tl, LANES), lambda i: (i, 0)),
                  full((LANES, fw)), full((1, fw)), full((fw, fw)), full((1, fw)),
                  full((fw, fw)), full((1, fw)), full((1, fw)), full((fw, 2 * hw)), full((1, hw))],
        out_specs=pl.BlockSpec((2, tl // r * pitch, hw), lambda i: (0, i, 0)),
        compiler_params=_params(("arbitrary",)),
        name="hyena_filters",
    )(jnp.asarray(zpad), w1p, b1[None], w2, b2[None], w3, b3[None], freq[None], wout, jnp.asarray(deltas))


def _shortconv_kernel(u_ref, up_ref, un_ref, w_ref, b_ref, z_ref, x1_ref, *, ts, hw, r, pitch):
    i = pl.program_id(1)
    last = pl.num_programs(1) - 1
    u = u_ref[...].astype(F32)
    prev_blk = up_ref[...].astype(F32)
    next_blk = un_ref[...].astype(F32)
    prev_row = jnp.where(i > 0, prev_blk[15:16, :], 0.0)
    next_row = jnp.where(i < last, next_blk[0:1, :], 0.0)
    row = lax.broadcasted_iota(jnp.int32, u.shape, 0)
    um = jnp.where(row == 0, prev_row, pltpu.roll(u, 1, 0))
    up = jnp.where(row == ts - 1, next_row, pltpu.roll(u, ts - 1, 0))
    w = w_ref[...]
    uc = w[0:1] * um + w[1:2] * u + w[2:3] * up + b_ref[...]
    x1_ref[...] = uc[:, :hw]
    z = uc[:, 2 * hw:] * uc[:, hw:2 * hw]
    pad = jnp.zeros((pitch - r, hw), F32)
    for g in range(ts // r):
        z_ref[g * pitch:g * pitch + r, :] = z[g * r:(g + 1) * r]
        z_ref[g * pitch + r:(g + 1) * pitch, :] = pad


def _shortconv(pn3, conv_w, conv_b, hw, r, pitch):
    b, s, _ = pn3.shape
    ts = 512
    w3 = 3 * hw
    nh = s // 16
    return pl.pallas_call(
        functools.partial(_shortconv_kernel, ts=ts, hw=hw, r=r, pitch=pitch),
        out_shape=(_sds((b, s // r * pitch, hw), F32), _sds((b, s, hw), F32)),
        grid=(b, s // ts),
        in_specs=[pl.BlockSpec((None, ts, w3), lambda bi, i: (bi, i, 0)),
                  pl.BlockSpec((None, 16, w3), lambda bi, i: (bi, jnp.maximum(i * (ts // 16) - 1, 0), 0)),
                  pl.BlockSpec((None, 16, w3), lambda bi, i: (bi, jnp.minimum((i + 1) * (ts // 16), nh - 1), 0)),
                  pl.BlockSpec((3, w3), lambda bi, i: (0, 0)),
                  pl.BlockSpec((1, w3), lambda bi, i: (0, 0))],
        out_specs=(pl.BlockSpec((None, ts // r * pitch, hw), lambda bi, i: (bi, i, 0)),
                   pl.BlockSpec((None, ts, hw), lambda bi, i: (bi, i, 0))),
        compiler_params=_params(("arbitrary", "arbitrary")),
        name="hyena_shortconv",
    )(pn3, pn3, pn3, conv_w, conv_b[None])


FFT_GROUP = 4


def _split_bf16(a):
    hi = a.astype(BF16)
    return hi, (a - hi.astype(F32)).astype(BF16)


def _dot3(fh, fl, x):
    xh, xl = _split_bf16(x)
    d = lambda a, b: jnp.dot(a, b, preferred_element_type=F32)
    return d(fh, xh) + (d(fh, xl) + d(fl, xh))


def _fft_kept(r):
    return (r // 2 + 1 + 7) // 8 * 8


def _dft_tables(r):
    idx = np.arange(r)
    ang = 2.0 * np.pi * np.outer(idx, idx) / r
    cos, sin = np.cos(ang), np.sin(ang)
    kp = _fft_kept(r)
    fa_half = np.concatenate([cos[:kp, :r // 2], -sin[:kp, :r // 2]], axis=0)
    fbig = np.block([[cos, sin], [-sin, cos]])
    fconj = np.block([[cos, -sin], [sin, cos]])
    wgt = np.zeros(kp)
    wgt[0] = wgt[r // 2] = 1.0
    wgt[1:r // 2] = 2.0
    gfin = np.concatenate([cos[:r // 2, :kp] * wgt, -sin[:r // 2, :kp] * wgt], axis=1)
    tang = 2.0 * np.pi * np.outer(idx, idx) / (r * r)
    tw = np.stack([np.cos(tang), -np.sin(tang)], axis=-1)
    split = lambda a: _split_bf16(jnp.asarray(a.astype(np.float32)))
    return split(fa_half), split(fbig), split(fconj), split(gfin), jnp.asarray(tw.astype(np.float32))


def _fft_a_kernel(x_ref, fh_ref, fl_ref, o_ref, *, r, k1, kp, pitch):
    fh, fl = fh_ref[...], fl_ref[...]
    pad = jnp.zeros((pitch - r, LANES), F32)
    for g in range(kp):
        o_ref[0, g * pitch + r:(g + 1) * pitch, :] = pad
        o_ref[1, g * pitch + r:(g + 1) * pitch, :] = pad

    def body(g, carry):
        n2 = g * FFT_GROUP
        xs = jnp.concatenate([x_ref[pl.ds(n2 + k, k1, stride=pitch), :] for k in range(FFT_GROUP)], axis=1)
        a = _dot3(fh, fl, xs)
        for k in range(FFT_GROUP):
            o_ref[0, pl.ds(n2 + k, kp, stride=pitch), :] = a[:kp, k * LANES:(k + 1) * LANES]
            o_ref[1, pl.ds(n2 + k, kp, stride=pitch), :] = a[kp:, k * LANES:(k + 1) * LANES]
        return carry

    lax.fori_loop(0, r // FFT_GROUP, body, 0)


def _fft_stage_a(x, fmat, r, pitch):
    bx, rows, c = x.shape
    k1 = rows // pitch
    kp = _fft_kept(r)
    return pl.pallas_call(
        functools.partial(_fft_a_kernel, r=r, k1=k1, kp=kp, pitch=pitch),
        out_shape=_sds((bx, 2, kp * pitch, c), F32),
        grid=(bx, c // LANES),
        in_specs=[pl.BlockSpec((None, rows, LANES), lambda b, ci: (b, 0, ci)),
                  pl.BlockSpec((2 * kp, k1), lambda b, ci: (0, 0)),
                  pl.BlockSpec((2 * kp, k1), lambda b, ci: (0, 0))],
        out_specs=pl.BlockSpec((None, 2, kp * pitch, LANES), lambda b, ci: (b, 0, 0, ci)),
        compiler_params=_params(("arbitrary", "arbitrary")),
        name="fft_stage_a",
    )(x, *fmat)


def _twiddled(a_ref, b, tr, ti, r):
    are, aim = a_ref[b, 0, :r, :], a_ref[b, 1, :r, :]
    return jnp.concatenate([are * tr - aim * ti, are * ti + aim * tr], axis=0)


def _fft_mk_kernel(a_ref, tw_ref, h0_ref, fbh_ref, fbl_ref, o_ref, *, r, scale):
    tr, ti = tw_ref[:, 0:1], tw_ref[:, 1:2]
    fbh, fbl = fbh_ref[...], fbl_ref[...]
    xf = _dot3(fbh, fbl, _twiddled(a_ref, 0, tr, ti, r))
    xb = _dot3(fbh, fbl, _twiddled(a_ref, 1, tr, ti, r))
    o_ref[0] = (xf[:r] + xb[:r] - h0_ref[...]) * scale
    o_ref[1] = (xf[r:] - xb[r:]) * scale


def _fft_filter_spectrum(a, tw, h0, fbig, r, pitch):
    c = a.shape[-1]
    kp = _fft_kept(r)
    return pl.pallas_call(
        functools.partial(_fft_mk_kernel, r=r, scale=1.0 / (r * r)),
        out_shape=_sds((2, kp * r, c), F32),
        grid=(kp,),
        in_specs=[pl.BlockSpec((2, 2, pitch, c), lambda k: (0, 0, k, 0)),
                  pl.BlockSpec((None, r, 2), lambda k: (k, 0, 0)),
                  pl.BlockSpec((1, c), lambda k: (0, 0)),
                  pl.BlockSpec((2 * r, 2 * r), lambda k: (0, 0)),
                  pl.BlockSpec((2 * r, 2 * r), lambda k: (0, 0))],
        out_specs=pl.BlockSpec((2, r, c), lambda k: (0, k, 0)),
        compiler_params=_params(("arbitrary",)),
        name="fft_filter_spectrum",
    )(a, tw, h0, *fbig)


def _fft_m_kernel(a_ref, ks_ref, tw_ref, fbh_ref, fbl_ref, fch_ref, fcl_ref, o_ref, *, r):
    tr, ti = tw_ref[:, 0:1], tw_ref[:, 1:2]
    x = _dot3(fbh_ref[...], fbl_ref[...], _twiddled(a_ref, 0, tr, ti, r))
    xre, xim = x[:r], x[r:]
    kre, kim = ks_ref[0], ks_ref[1]
    c = _dot3(fch_ref[...], fcl_ref[...],
              jnp.concatenate([xre * kre - xim * kim, xre * kim + xim * kre], axis=0))
    cre, cim = c[:r], c[r:]
    o_ref[0, :r, :] = cre * tr + cim * ti
    o_ref[1, :r, :] = cim * tr - cre * ti
    o_ref[:, r:, :] = jnp.zeros((2,) + (o_ref.shape[1] - r, o_ref.shape[2]), F32)


def _fft_stage_m(a, ks, tw, fbig, fconj, r, pitch):
    b, _, rows, c = a.shape
    a5 = a.reshape(b, 1, 2, rows, c)
    mat = pl.BlockSpec((2 * r, 2 * r), lambda k, bi: (0, 0))
    return pl.pallas_call(
        functools.partial(_fft_m_kernel, r=r),
        out_shape=_sds(a.shape, F32),
        grid=(rows // pitch, b),
        in_specs=[pl.BlockSpec((None, 1, 2, pitch, c), lambda k, bi: (bi, 0, 0, k, 0)),
                  pl.BlockSpec((2, r, c), lambda k, bi: (0, k, 0)),
                  pl.BlockSpec((None, r, 2), lambda k, bi: (k, 0, 0)),
                  mat, mat, mat, mat],
        out_specs=pl.BlockSpec((None, 2, pitch, c), lambda k, bi: (bi, 0, k, 0)),
        compiler_params=_params(("arbitrary", "arbitrary")),
        name="fft_stage_m",
    )(a5, ks, tw, *fbig, *fconj)


def _fft_f_kernel(d_ref, gh_ref, gl_ref, o_ref, *, r, kp, pitch):
    gh, gl = gh_ref[...], gl_ref[...]
    pad = jnp.zeros((pitch - r, LANES), F32)
    for g in range(r // 2):
        o_ref[g * pitch + r:(g + 1) * pitch, :] = pad

    def body(g, carry):
        n2 = g * FFT_GROUP
        dcat = jnp.concatenate(
            [jnp.concatenate([d_ref[0, pl.ds(n2 + k, kp, stride=pitch), :],
                              d_ref[1, pl.ds(n2 + k, kp, stride=pitch), :]], axis=0) for k in range(FFT_GROUP)],
            axis=1)
        y = _dot3(gh, gl, dcat)
        for k in range(FFT_GROUP):
            o_ref[pl.ds(n2 + k, r // 2, stride=pitch), :] = y[:, k * LANES:(k + 1) * LANES]
        return carry

    lax.fori_loop(0, r // FFT_GROUP, body, 0)


def _fft_stage_f(dmat, gfin, r, pitch):
    b, _, rows, c = dmat.shape
    kp = rows // pitch
    out_rows = r // 2 * pitch
    return pl.pallas_call(
        functools.partial(_fft_f_kernel, r=r, kp=kp, pitch=pitch),
        out_shape=_sds((b, out_rows, c), F32),
        grid=(b, c // LANES),
        in_specs=[pl.BlockSpec((None, 2, rows, LANES), lambda bi, ci: (bi, 0, 0, ci)),
                  pl.BlockSpec((r // 2, 2 * kp), lambda bi, ci: (0, 0)),
                  pl.BlockSpec((r // 2, 2 * kp), lambda bi, ci: (0, 0))],
        out_specs=pl.BlockSpec((None, out_rows, LANES), lambda bi, ci: (bi, 0, ci)),
        compiler_params=_params(("arbitrary", "arbitrary")),
        name="fft_stage_f",
    )(dmat, *gfin)


def _fft_radix(length):
    r = int(round(math.sqrt(2 * length)))
    assert r * r == 2 * length, "sequence length must make 2L a perfect square"
    return r, r + 8


def _long_conv(zin, hfb, r, pitch):
    fa_half, fbig, fconj, gfin, tw = _dft_tables(r)
    ks = _fft_filter_spectrum(_fft_stage_a(hfb, fa_half, r, pitch), tw, hfb[1, 0:1, :], fbig, r, pitch)
    a = _fft_stage_a(zin, fa_half, r, pitch)
    dmat = _fft_stage_m(a, ks, tw, fbig, fconj, r, pitch)
    return _fft_stage_f(dmat, gfin, r, pitch)


def _alibi_slopes(n_heads):
    def pow2_slopes(m):
        start = 2.0 ** (-8.0 / m)
        return [start ** (i + 1) for i in range(m)]
    base = 2 ** int(math.floor(math.log2(n_heads)))
    slopes = pow2_slopes(base)
    if base < n_heads:
        slopes = slopes + pow2_slopes(2 * base)[0::2][: n_heads - base]
    return np.array(sorted(slopes, reverse=True), dtype=np.float32)


def _attn_kernel(q_ref, k_ref, kp_ref, kn_ref, v_ref, vp_ref, vn_ref, o_ref, l_ref, *, tq, n, dil, slopes):
    i = pl.program_id(1)
    side = ATTN_SIDE
    nk = tq + 2 * side
    row = lax.broadcasted_iota(jnp.int32, (tq, nk), 0)
    col = lax.broadcasted_iota(jnp.int32, (tq, nk), 1)
    rel = jnp.abs(col - side - row)
    kglob = i * tq + col - side
    valid = (rel <= side) & (kglob >= 0) & (kglob < n)
    dist = (rel * dil).astype(F32)
    scale = HEAD_DIM ** -0.5
    for h in range(HEADS_PER_GROUP):
        hs = slice(h * HEAD_DIM, (h + 1) * HEAD_DIM)
        q = q_ref[:, hs]
        kc = jnp.concatenate([kp_ref[:, hs], k_ref[:, hs], kn_ref[:, hs]], axis=0)
        vc = jnp.concatenate([vp_ref[:, hs], v_ref[:, hs], vn_ref[:, hs]], axis=0)
        s = lax.dot_general(q, kc, (((1,), (1,)), ((), ())), preferred_element_type=F32) * scale
        s = jnp.where(valid, s - float(slopes[h]) * dist, NEG_INF)
        m = jnp.max(s, axis=-1, keepdims=True)
        p = jnp.exp(s - m)
        den = jnp.sum(p, axis=-1, keepdims=True)
        o = jnp.dot(p.astype(BF16), vc, preferred_element_type=F32) / den
        o_ref[:, hs] = o.astype(o_ref.dtype)
        l_ref[:, hs] = jnp.broadcast_to(m + jnp.log(den), (tq, HEAD_DIM))


def _dilated_attention(qkv, n, dil, slopes, col0=0):
    streams = qkv.shape[0]
    tq = min(128, n)
    side = ATTN_SIDE
    nh = n // side
    gw = GROUP_WIDTH
    main = lambda cb: pl.BlockSpec((None, tq, gw), lambda s, i: (s, i, col0 + cb))
    prev = lambda cb: pl.BlockSpec((None, side, gw),
                                   lambda s, i: (s, jnp.maximum(i * (tq // side) - 1, 0), col0 + cb))
    nxt = lambda cb: pl.BlockSpec((None, side, gw),
                                  lambda s, i: (s, jnp.minimum((i + 1) * (tq // side), nh - 1), col0 + cb))
    return pl.pallas_call(
        functools.partial(_attn_kernel, tq=tq, n=n, dil=dil, slopes=tuple(float(v) for v in slopes)),
        out_shape=(_sds((streams, n, gw), BF16), _sds((streams, n, gw), F32)),
        grid=(streams, n // tq),
        in_specs=[main(0), main(1), prev(1), nxt(1), main(2), prev(2), nxt(2)],
        out_specs=(pl.BlockSpec((None, tq, gw), lambda s, i: (s, i, 0)),
                   pl.BlockSpec((None, tq, gw), lambda s, i: (s, i, 0))),
        compiler_params=_params(("arbitrary", "arbitrary")),
        name=f"dilated_attention_d{dil}",
    )(qkv, qkv, qkv, qkv, qkv, qkv, qkv)


def _merge_kernel(y_ref, z_ref, x1_ref, skip_ref, o0_ref, l0_ref, o1_ref, l1_ref, o2_ref, l2_ref,
                  ghy_ref, gat_ref, wh_ref, wa_ref, out_ref, hy_s, at_s, so1, sl1, so2, sl2, *, tm, tn, r, pitch):
    nct = GROUP_WIDTH // LANES
    for g in range(tm // r):
        rows = slice(g * pitch, g * pitch + r)
        hy_s[g * r:(g + 1) * r, :] = ((y_ref[rows, :] + z_ref[rows, :] * skip_ref[...])
                                      * x1_ref[g * r:(g + 1) * r, :]).astype(BF16)
    for dil, oref, lref, so, sl in ((4, o1_ref, l1_ref, so1, sl1), (16, o2_ref, l2_ref, so2, sl2)):
        for res in range(dil):
            for c in range(nct):
                cs = slice(c * LANES, (c + 1) * LANES)
                so[c, pl.ds(res, tm // dil, stride=dil), :] = oref[res, :, cs].astype(F32)
                sl[c, pl.ds(res, tm // dil, stride=dil), :] = lref[res, :, cs]
    for c in range(nct):
        cs = slice(c * LANES, (c + 1) * LANES)
        a0, a1, a2 = l0_ref[:, cs], sl1[c], sl2[c]
        m = jnp.maximum(jnp.maximum(a0, a1), a2)
        e0, e1, e2 = jnp.exp(a0 - m), jnp.exp(a1 - m), jnp.exp(a2 - m)
        at = (e0 * o0_ref[:, cs].astype(F32) + e1 * so1[c] + e2 * so2[c]) / (e0 + e1 + e2)
        at_s[:, cs] = at.astype(BF16)

    hy, at = hy_s[...], at_s[...]
    for j in range(out_ref.shape[1] // tn):
        cols = slice(j * tn, (j + 1) * tn)
        acc_h = jnp.dot(hy, wh_ref[:, cols], preferred_element_type=F32)
        acc_a = jnp.dot(at, wa_ref[:, cols], preferred_element_type=F32)
        out = (jax.nn.sigmoid(ghy_ref[:, cols].astype(F32)) * acc_h
               + jax.nn.sigmoid(gat_ref[:, cols].astype(F32)) * acc_a)
        out_ref[:, cols] = out.astype(out_ref.dtype)


def _merge(yconv, zin, x1c, skip, o0, l0, o1, l1, o2, l2, gates, wh, wa, seq, r, pitch):
    n_tok, hw = x1c.shape
    d_model = wh.shape[1]
    gw = GROUP_WIDTH
    tm = 256
    spb = seq // tm
    row = lambda width: pl.BlockSpec((tm, width), lambda i: (i, 0))
    prow = pl.BlockSpec((tm // r * pitch, hw), lambda i: (i, 0))
    res = lambda dil: pl.BlockSpec((None, dil, tm // dil, gw), lambda i: (i // spb, 0, i % spb, 0))
    resident = lambda rows: pl.BlockSpec((rows, d_model), lambda i: (0, 0), pipeline_mode=pl.Buffered(1))
    return pl.pallas_call(
        functools.partial(_merge_kernel, tm=tm, tn=COL_BLOCK, r=r, pitch=pitch),
        out_shape=_sds((n_tok, d_model), BF16),
        grid=(n_tok // tm,),
        in_specs=[prow, prow, row(hw), pl.BlockSpec((1, hw), lambda i: (0, 0)),
                  row(gw), row(gw), res(4), res(4), res(16), res(16),
                  pl.BlockSpec((tm, d_model), lambda i: (i, 0)),
                  pl.BlockSpec((tm, d_model), lambda i: (i, 1)),
                  resident(hw), resident(gw)],
        out_specs=pl.BlockSpec((tm, d_model), lambda i: (i, 0)),
        scratch_shapes=[pltpu.VMEM((tm, hw), BF16), pltpu.VMEM((tm, gw), BF16)]
                       + [pltpu.VMEM((gw // LANES, tm, LANES), F32)] * 4,
        compiler_params=_params(("arbitrary",)),
        name="gated_merge",
    )(yconv, zin, x1c, skip, o0, l0, o1, l1, o2, l2, gates, gates, wh, wa)


def _router_kernel(mo_ref, x_ref, gm_ref, gpost_ref, gpre_ref, sc_ref, sh_ref, wr_ref, br_ref,
                   x1_ref, h2_ref, idx_ref, tw_ref, *, ts, d_model, n_experts):
    mo = mo_ref[...]
    y = mo * lax.rsqrt(jnp.mean(mo * mo, axis=-1, keepdims=True) + RMS_EPS) * gpost_ref[...]
    x1 = x_ref[...] + gm_ref[...] * y
    x1_ref[...] = x1
    h2 = (x1 * lax.rsqrt(jnp.mean(x1 * x1, axis=-1, keepdims=True) + RMS_EPS) * gpre_ref[...]
          * (1.0 + sc_ref[...]) + sh_ref[...])
    nct = d_model // 2 // LANES
    sp = nct + SLAB_PAD
    words = _pack_bf16_pairs(h2)
    for c in range(nct):
        h2_ref[pl.ds(c, ts, stride=sp), :] = words[:, c * LANES:(c + 1) * LANES]
    for c in range(nct, sp):
        h2_ref[pl.ds(c, ts, stride=sp), :] = jnp.zeros((ts, LANES), jnp.uint32)
    h2_hi, h2_lo = _split_bf16(h2)
    dot = lambda a, b: jnp.dot(a, b, preferred_element_type=F32)
    logits = dot(h2_hi, wr_ref[0]) + (dot(h2_lo, wr_ref[0]) + dot(h2_hi, wr_ref[1])) + br_ref[...]
    lane = lax.broadcasted_iota(jnp.int32, logits.shape, 1)
    lane_f = lane.astype(F32)
    logits = jnp.where(lane < n_experts, logits, -jnp.inf)
    idx_out = jnp.zeros(logits.shape, jnp.int32)
    val_out = jnp.zeros(logits.shape, F32)
    top0 = None
    den = None
    for k in range(TOP_K):
        m = jnp.max(logits, axis=-1, keepdims=True)
        idx = jnp.min(jnp.where(logits == m, lane_f, float(LANES)), axis=-1, keepdims=True).astype(jnp.int32)
        if k == 0:
            top0 = m
        e = jnp.exp(m - top0)
        den = e if den is None else den + e
        idx_out = jnp.where(lane == k, idx, idx_out)
        val_out = jnp.where(lane == k, e, val_out)
        logits = jnp.where(lane == idx, -jnp.inf, logits)
    idx_ref[...] = idx_out
    tw_ref[...] = val_out / den


def _post_mix_and_route(mo, x, gate_m, g_post, g_pre, scale_f, shift_f, w_router, b_router):
    b, s, d = x.shape
    e = w_router.shape[1]
    ts = 256
    wr = jnp.stack(_split_bf16(jnp.zeros((d, LANES), F32).at[:, :e].set(w_router)))
    br = jnp.zeros((1, LANES), F32).at[0, :e].set(b_router)
    sp = d // 2 // LANES + SLAB_PAD
    spb = s // ts
    rowblk = lambda width: pl.BlockSpec((ts, width), lambda i: (i, 0))
    per_batch = pl.BlockSpec((None, 1, d), lambda i: (i // spb, 0, 0))
    vec = pl.BlockSpec((1, d), lambda i: (0, 0))
    n_tok = b * s
    return pl.pallas_call(
        functools.partial(_router_kernel, ts=ts, d_model=d, n_experts=e),
        out_shape=(_sds((n_tok, d), F32), _sds((n_tok * sp, LANES), jnp.uint32),
                   _sds((n_tok, LANES), jnp.int32), _sds((n_tok, LANES), F32)),
        grid=(n_tok // ts,),
        in_specs=[rowblk(d), rowblk(d), per_batch, vec, vec, per_batch, per_batch,
                  pl.BlockSpec((2, d, LANES), lambda i: (0, 0, 0)), pl.BlockSpec((1, LANES), lambda i: (0, 0))],
        out_specs=(rowblk(d), pl.BlockSpec((ts * sp, LANES), lambda i: (i, 0)), rowblk(LANES), rowblk(LANES)),
        compiler_params=_params(("arbitrary",)),
        name="post_mix_route",
    )(mo, x.reshape(n_tok, d), gate_m, g_post, g_pre, scale_f, shift_f, wr, br)


W_CHUNKS = 8


def _weight_stream(w_hbms, stage, wbuf, wsem, rc):
    n = len(w_hbms)

    def copies(e, c, st):
        rows = pl.ds(pl.multiple_of(c * rc, rc), rc)
        return [pltpu.make_async_copy(w.at[e, rows, :], stage.at[st, m], wsem.at[st]) for m, w in enumerate(w_hbms)]

    def start(e, c, st):
        for cp in copies(e, c, st):
            cp.start()

    def finish(e, lo, hi, half):
        def body(c, carry):
            st = c % 2
            for cp in copies(e, c, st):
                cp.wait()
            rows = pl.ds(pl.multiple_of(c * rc, rc), rc)
            for m in range(n):
                wbuf[n * half + m, rows, :] = stage[st, m].astype(BF16)

            @pl.when(c + 1 < W_CHUNKS)
            def _():
                start(e, c + 1, 1 - st)
            return carry
        lax.fori_loop(lo, hi, body, 0)

    return start, finish


def _stream_weights_step(i, n_used, be_ref, sch_ref, start, finish):
    half = sch_ref[5 * i]

    @pl.when((i == 0) & (n_used > 0))
    def _():
        start(be_ref[0], 0, 0)
        finish(be_ref[0], 0, W_CHUNKS, half)

    @pl.when(i < n_used)
    def _():
        nxt = sch_ref[5 * i + 1]

        @pl.when(sch_ref[5 * i + 2] == 1)
        def _():
            start(nxt, 0, 0)

        finish(nxt, sch_ref[5 * i + 3], sch_ref[5 * i + 4], 1 - half)

    return half


def _expert_up_kernel(be_ref, nu_ref, sch_ref, tok0_ref, tok1_ref, h2_hbm, wg_hbm, bg_ref, wu_hbm, bu_ref, act_ref,
                      xbuf, xb, wbuf, stage, sem, wsem, *, tb, nct, rc):
    i = pl.program_id(0)
    n_used = nu_ref[0]
    sp = nct + SLAB_PAD

    def row_copy(t, j, slot):
        return pltpu.make_async_copy(h2_hbm.at[pl.ds(pl.multiple_of(t * sp, 8), nct), :],
                                     xbuf.at[slot, pl.ds(pl.multiple_of(j * sp, 8), nct), :], sem.at[slot])

    def gather(tok_ref, slot):
        def body(j, carry):
            row_copy(tok_ref[0, j], j, slot).start()
            return carry
        lax.fori_loop(0, tb, body, 0, unroll=8)

    def wait(slot):
        pltpu.make_async_copy(h2_hbm.at[pl.ds(0, tb * nct), :], xbuf.at[slot, pl.ds(0, tb * nct), :],
                              sem.at[slot]).wait()

    @pl.when((i == 0) & (n_used > 0))
    def _():
        gather(tok0_ref, 0)

    start, finish = _weight_stream((wg_hbm, wu_hbm), stage, wbuf, wsem, rc)
    half = _stream_weights_step(i, n_used, be_ref, sch_ref, start, finish)

    @pl.when(i < n_used)
    def _():
        slot = i % 2
        wait(slot)
        for j in range(tb):
            row_copy(tok1_ref[0, j], j, 1 - slot).start()
        for c in range(nct):
            hi, lo = _unpack_bf16_pairs(xbuf[slot, pl.ds(c, tb, stride=sp), :])
            xb[:, 2 * c * LANES:(2 * c + 1) * LANES] = hi.astype(BF16)
            xb[:, (2 * c + 1) * LANES:(2 * c + 2) * LANES] = lo.astype(BF16)
        x = xb[...]
        g = jnp.dot(x, wbuf[2 * half], preferred_element_type=F32) + bg_ref[...]
        u = jnp.dot(x, wbuf[2 * half + 1], preferred_element_type=F32) + bu_ref[...]
        g = jnp.minimum(g, SWIGLU_LIMIT)
        u = jnp.clip(u, -SWIGLU_LIMIT, SWIGLU_LIMIT)
        act_ref[...] = (g * jax.nn.sigmoid(SWIGLU_ALPHA * g) * (u + 1.0)).astype(act_ref.dtype)

    @pl.when((i == n_used) & (n_used > 0))
    def _():
        wait(i % 2)

    @pl.when(i >= n_used)
    def _():
        act_ref[...] = jnp.zeros(act_ref.shape, act_ref.dtype)


def _expert_up(block_e, n_used, sched, row_tok3, h2s, wg, bg, wu, bu, nct):
    n_blocks, _, tb = row_tok3.shape
    _, d, f = wg.shape
    rc = d // W_CHUNKS
    grid_spec = pltpu.PrefetchScalarGridSpec(
        num_scalar_prefetch=3,
        grid=(n_blocks,),
        in_specs=[pl.BlockSpec((None, 1, tb), lambda i, be, nu, sc: (i, 0, 0), memory_space=pltpu.SMEM),
                  pl.BlockSpec((None, 1, tb), lambda i, be, nu, sc: (jnp.minimum(i + 1, n_blocks - 1), 0, 0),
                               memory_space=pltpu.SMEM),
                  pl.BlockSpec(memory_space=pl.ANY),
                  pl.BlockSpec(memory_space=pl.ANY),
                  pl.BlockSpec((None, 1, f), lambda i, be, nu, sc: (be[i], 0, 0)),
                  pl.BlockSpec(memory_space=pl.ANY),
                  pl.BlockSpec((None, 1, f), lambda i, be, nu, sc: (be[i], 0, 0))],
        out_specs=pl.BlockSpec((tb, f), lambda i, be, nu, sc: (i, 0)),
        scratch_shapes=[pltpu.VMEM((2, tb * (nct + SLAB_PAD), LANES), jnp.uint32), pltpu.VMEM((tb, d), BF16),
                        pltpu.VMEM((4, d, f), BF16), pltpu.VMEM((2, 2, rc, f), F32),
                        pltpu.SemaphoreType.DMA((2,)), pltpu.SemaphoreType.DMA((2,))],
    )
    return pl.pallas_call(
        functools.partial(_expert_up_kernel, tb=tb, nct=nct, rc=rc),
        out_shape=_sds((n_blocks * tb, f), BF16),
        grid_spec=grid_spec,
        compiler_params=_params(("arbitrary",)),
        name="expert_up",
    )(block_e, n_used, sched, row_tok3, row_tok3, h2s, wg, bg, wu, bu)


def _expert_down_kernel(be_ref, nu_ref, sch_ref, act_ref, wd_hbm, bd_ref, ys_ref, wbuf, stage, wsem, *, tb, nct, rc):
    i = pl.program_id(0)
    start, finish = _weight_stream((wd_hbm,), stage, wbuf, wsem, rc)
    half = _stream_weights_step(i, nu_ref[0], be_ref, sch_ref, start, finish)

    @pl.when(i < nu_ref[0])
    def _():
        act = act_ref[...]
        sp = nct + SLAB_PAD
        for c in range(nct):
            cols = slice(2 * c * LANES, (2 * c + 2) * LANES)
            y = jnp.dot(act, wbuf[half, :, cols], preferred_element_type=F32) + bd_ref[:, cols]
            ys_ref[pl.ds(c, tb, stride=sp), :] = _pack_bf16_pairs(y)
        for c in range(nct, sp):
            ys_ref[pl.ds(c, tb, stride=sp), :] = jnp.zeros((tb, LANES), jnp.uint32)

    @pl.when(i >= nu_ref[0])
    def _():
        ys_ref[...] = jnp.zeros(ys_ref.shape, ys_ref.dtype)


def _expert_down(block_e, n_used, sched, act, wd, bd, tb):
    _, f, d = wd.shape
    n_blocks = act.shape[0] // tb
    nct = d // 2 // LANES
    rc = f // W_CHUNKS
    grid_spec = pltpu.PrefetchScalarGridSpec(
        num_scalar_prefetch=3,
        grid=(n_blocks,),
        in_specs=[pl.BlockSpec((tb, f), lambda i, be, nu, sc: (i, 0)),
                  pl.BlockSpec(memory_space=pl.ANY),
                  pl.BlockSpec((None, 1, d), lambda i, be, nu, sc: (be[i], 0, 0))],
        out_specs=pl.BlockSpec((tb * (nct + SLAB_PAD), LANES), lambda i, be, nu, sc: (i, 0)),
        scratch_shapes=[pltpu.VMEM((2, f, d), BF16), pltpu.VMEM((2, 1, rc, d), F32),
                        pltpu.SemaphoreType.DMA((2,))],
    )
    return pl.pallas_call(
        functools.partial(_expert_down_kernel, tb=tb, nct=nct, rc=rc),
        out_shape=_sds((n_blocks * tb * (nct + SLAB_PAD), LANES), jnp.uint32),
        grid_spec=grid_spec,
        compiler_params=_params(("arbitrary",)),
        name="expert_down",
    )(block_e, n_used, sched, act, wd, bd)


def _combine_kernel(d0_ref, d1_ref, ys_hbm, tw_ref, x1_ref, gf_ref, gpost_ref, o_ref, buf, ff, sem, *, tc, nct):
    i = pl.program_id(0)
    last = pl.num_programs(0) - 1
    n_rows = TOP_K * tc
    sp = nct + SLAB_PAD

    def row_copy(r, j, slot):
        return pltpu.make_async_copy(ys_hbm.at[pl.ds(pl.multiple_of(r * sp, 8), nct), :],
                                     buf.at[slot, pl.ds(pl.multiple_of(j * sp, 8), nct), :], sem.at[slot])

    def gather(dref, slot):
        def body(j, carry):
            row_copy(dref[0, j], j, slot).start()
            return carry
        lax.fori_loop(0, n_rows, body, 0, unroll=8)

    def wait(slot):
        pltpu.make_async_copy(ys_hbm.at[pl.ds(0, n_rows * nct), :], buf.at[slot, pl.ds(0, n_rows * nct), :],
                              sem.at[slot]).wait()

    @pl.when(i == 0)
    def _():
        gather(d0_ref, 0)

    slot = i % 2
    wait(slot)
    for j in range(n_rows):
        row_copy(d1_ref[0, j], j, 1 - slot).start()
    wk =[tw_ref[:, k:k + 1] for k in range(TOP_K)]
    for c in range(nct):
        acc_hi = acc_lo = None
        for k in range(TOP_K):
            hi, lo = _unpack_bf16_pairs(buf[slot, pl.ds(k * tc * sp + c, tc, stride=sp), :])
            acc_hi = wk[k] * hi if acc_hi is None else acc_hi + wk[k] * hi
            acc_lo = wk[k] * lo if acc_lo is None else acc_lo + wk[k] * lo
        ff[:, 2 * c * LANES:(2 * c + 1) * LANES] = acc_hi
        ff[:, (2 * c + 1) * LANES:(2 * c + 2) * LANES] = acc_lo
    f = ff[...]
    y = f * lax.rsqrt(jnp.mean(f * f, axis=-1, keepdims=True) + RMS_EPS) * gpost_ref[...]
    o_ref[...] = x1_ref[...] + gf_ref[...] * y

    @pl.when(i == last)
    def _():
        wait(1 - slot)


def _combine(dest3, ys, top_w, x1, gate_f, g_post, seq):
    n_tok, d = x1.shape
    n_steps, _, n_rows = dest3.shape
    tc = n_rows // TOP_K
    nct = d // 2 // LANES
    spb = seq // tc
    return pl.pallas_call(
        functools.partial(_combine_kernel, tc=tc, nct=nct),
        out_shape=_sds((n_tok, d), F32),
        grid=(n_steps,),
        in_specs=[pl.BlockSpec((None, 1, n_rows), lambda i: (i, 0, 0), memory_space=pltpu.SMEM),
                  pl.BlockSpec((None, 1, n_rows), lambda i: (jnp.minimum(i + 1, n_steps - 1), 0, 0),
                               memory_space=pltpu.SMEM),
                  pl.BlockSpec(memory_space=pl.ANY),
                  pl.BlockSpec((tc, LANES), lambda i: (i, 0)),
                  pl.BlockSpec((tc, d), lambda i: (i, 0)),
                  pl.BlockSpec((None, 1, d), lambda i: (i // spb, 0, 0)),
                  pl.BlockSpec((1, d), lambda i: (0, 0))],
        out_specs=pl.BlockSpec((tc, d), lambda i: (i, 0)),
        scratch_shapes=[pltpu.VMEM((2, n_rows * (nct + SLAB_PAD), LANES), jnp.uint32), pltpu.VMEM((tc, d), F32),
                        pltpu.SemaphoreType.DMA((2,))],
        compiler_params=_params(("arbitrary",)),
        name="expert_combine",
    )(dest3, dest3, ys, top_w, x1, gate_f, g_post)


def _rank_kernel(idx_ref, tri_ref, rank_ref, cnt_ref, carry, *, tr):
    @pl.when(pl.program_id(0) == 0)
    def _():
        carry[...] = jnp.zeros(carry.shape, F32)

    idx = idx_ref[...]
    lane = lax.broadcasted_iota(jnp.int32, idx.shape, 1)
    base = carry[0:1, :]
    out = jnp.zeros(idx.shape, jnp.int32)
    for k in range(TOP_K):
        onehot = jnp.where(lane == idx[:, k:k + 1], 1.0, 0.0)
        csum = jnp.dot(tri_ref[...], onehot.astype(BF16), preferred_element_type=F32)
        rank = jnp.sum(onehot * (csum + base), axis=-1, keepdims=True) - 1.0
        out = jnp.where(lane == k, rank.astype(jnp.int32), out)
        base = base + csum[tr - 1:tr, :]
    rank_ref[...] = out
    carry[...] = jnp.broadcast_to(base, carry.shape)
    cnt_ref[...] = jnp.broadcast_to(base, cnt_ref.shape).astype(jnp.int32)


def _expert_ranks(top_idx_padded):
    n_tok = top_idx_padded.shape[0]
    tr = 512
    tri = jnp.asarray(np.tril(np.ones((tr, tr), np.float32)), BF16)
    return pl.pallas_call(
        functools.partial(_rank_kernel, tr=tr),
        out_shape=(_sds((n_tok, LANES), jnp.int32), _sds((8, LANES), jnp.int32)),
        grid=(n_tok // tr,),
        in_specs=[pl.BlockSpec((tr, LANES), lambda i: (i, 0)), pl.BlockSpec((tr, tr), lambda i: (0, 0))],
        out_specs=(pl.BlockSpec((tr, LANES), lambda i: (i, 0)), pl.BlockSpec((8, LANES), lambda i: (0, 0))),
        scratch_shapes=[pltpu.VMEM((8, LANES), F32)],
        compiler_params=_params(("arbitrary",)),
        name="expert_ranks",
    )(top_idx_padded, tri)


def _routing_tables(top_idx_padded, n_experts, tb):
    n_tok = top_idx_padded.shape[0]
    n_assign = n_tok * TOP_K
    ranks, counts = _expert_ranks(top_idx_padded)
    sizes = counts[0, :n_experts]
    padded = (sizes + tb - 1) // tb * tb
    pad_end = jnp.cumsum(padded)
    pad_start = pad_end - padded
    top_idx = top_idx_padded[:, :TOP_K]
    experts = jnp.arange(n_experts, dtype=jnp.int32)
    start_of = jnp.sum(jnp.where(top_idx[:, :, None] == experts, pad_start, 0), axis=-1)
    dest = (start_of + ranks[:, :TOP_K]).astype(jnp.int32).reshape(-1)
    n_rows = -(-n_assign // tb) * tb + n_experts * tb
    n_blocks = n_rows // tb
    tok = (jnp.arange(n_assign, dtype=jnp.int32) // TOP_K)
    row_tok = jnp.zeros((n_rows,), jnp.int32).at[dest].set(tok, unique_indices=True, mode="promise_in_bounds")
    block_start = jnp.arange(n_blocks, dtype=jnp.int32) * tb
    block_e = jnp.minimum(jnp.sum((pad_end[None, :] <= block_start[:, None]).astype(jnp.int32), axis=1),
                          n_experts - 1).astype(jnp.int32)
    n_used = (pad_end[-1] // tb).astype(jnp.int32).reshape(1)
    nb = padded // tb
    occupied = nb > 0
    ordinal = jnp.cumsum(occupied.astype(jnp.int32)) - 1
    later = lax.cummin(jnp.where(occupied, experts, n_experts), reverse=True)
    next_e = jnp.concatenate([later[1:], jnp.full((1,), n_experts, jnp.int32)])
    blk = jnp.arange(n_blocks, dtype=jnp.int32)
    k = blk - (pad_start // tb)[block_e]
    nbe = nb[block_e]
    active = (blk < n_used[0]) & (next_e[block_e] < n_experts)
    zero = jnp.zeros_like(blk)
    spread = jnp.maximum(nbe - 1, 1)
    lo = jnp.where(nbe > 1, jnp.maximum(k - 1, 0) * W_CHUNKS // spread, 0)
    hi = jnp.where(nbe > 1, k * W_CHUNKS // spread, W_CHUNKS)
    sched = jnp.stack([ordinal[block_e] % 2,
                       jnp.where(active, next_e[block_e], zero),
                       jnp.where(active & (k == 0), 1, zero),
                       jnp.where(active, lo, zero),
                       jnp.where(active, hi, zero)], axis=1).reshape(-1).astype(jnp.int32)
    return dest, row_tok, block_e, n_used, sched, n_rows


def _moe(h2s, top_idx, top_w, wg, bg, wu, bu, wd, bd, x1, gate_f, g_post, seq):
    n_tok, d = x1.shape
    n_experts = wg.shape[0]
    tb = MOE_ROWS
    nct = d // 2 // LANES
    dest, row_tok, block_e, n_used, sched, n_rows = _routing_tables(top_idx, n_experts, tb)
    n_blocks = n_rows // tb
    act = _expert_up(block_e, n_used, sched, row_tok.reshape(n_blocks, 1, tb), h2s, wg, bg[:, None, :], wu,
                     bu[:, None, :], nct)
    ys = _expert_down(block_e, n_used, sched, act, wd, bd[:, None, :], tb)
    tc = 128
    dest3 = dest.reshape(n_tok // tc, tc, TOP_K).transpose(0, 2, 1).reshape(n_tok // tc, 1, TOP_K * tc)
    return _combine(dest3, ys, top_w, x1, gate_f, g_post, seq)


def _layer(x, c8, p):
    b, s, d = x.shape
    n_tok = b * s
    hw = p["hy_skip"].shape[0]
    gw = GROUP_WIDTH
    assert hw == COL_BLOCK and gw == COL_BLOCK and d % COL_BLOCK == 0

    mod = _adaln(c8, p["w_ada"], p["b_ada"][None])[:b]
    shift_m, scale_m, gate_m, shift_f, scale_f, gate_f = [m[:, None, :] for m in jnp.split(mod, 6, axis=-1)]

    h, h4, h16 = _prenorm_mix(x, p["g_pre_mix"][None], scale_m, shift_m)

    w_in = p["w_in"].astype(BF16)
    nd = d // COL_BLOCK
    pn = _matmul(h.reshape(n_tok, d), w_in, [0, 1, 2, 3, 6, 9], BF16, "in_proj_natural")
    gates = _matmul(h.reshape(n_tok, d), w_in, list(range(12, 12 + 2 * nd)), BF16, "in_proj_gates")
    qkv1 = _matmul(h4.reshape(n_tok, d), w_in, [4, 7, 10], BF16, "in_proj_dil4")
    qkv2 = _matmul(h16.reshape(n_tok, d), w_in, [5, 8, 11], BF16, "in_proj_dil16")

    r, pitch = _fft_radix(s)
    hfb = _hyena_filters(s, r, pitch, p["hy_f_w1"], p["hy_f_b1"], p["hy_f_w2"], p["hy_f_b2"], p["hy_f_w3"],
                         p["hy_f_b3"], p["hy_f_freq"], p["hy_f_wout"])
    zin, x1c = _shortconv(pn.reshape(b, s, -1), p["hy_conv_w"], p["hy_conv_b"], hw, r, pitch)
    yconv = _long_conv(zin, hfb, r, pitch)

    slopes = _alibi_slopes(N_GROUPS * HEADS_PER_GROUP).reshape(N_GROUPS, HEADS_PER_GROUP)
    o0, l0 = _dilated_attention(pn.reshape(b, s, -1), s, 1, slopes[0], col0=3)
    o1, l1 = _dilated_attention(qkv1.reshape(b * 4, s // 4, 3 * gw), s // 4, 4, slopes[1])
    o2, l2 = _dilated_attention(qkv2.reshape(b * 16, s // 16, 3 * gw), s // 16, 16, slopes[2])

    merged = _merge(yconv.reshape(-1, hw), zin.reshape(-1, hw), x1c.reshape(n_tok, hw), p["hy_skip"][None],
                    o0.reshape(n_tok, gw), l0.reshape(n_tok, gw),
                    o1.reshape(b, 4, s // 4, gw), l1.reshape(b, 4, s // 4, gw),
                    o2.reshape(b, 16, s // 16, gw), l2.reshape(b, 16, s // 16, gw),
                    gates, p["w_proj_hyena"].astype(BF16), p["w_proj_attn"].astype(BF16), s, r, pitch)
    mo = _matmul(merged, p["w_out"].astype(BF16), list(range(nd)), F32, "out_proj")

    x1, h2s, top_idx, top_w = _post_mix_and_route(mo, x, gate_m, p["g_post_mix"][None], p["g_pre_ffn"][None],
                                                  scale_f, shift_f, p["w_router"], p["b_router"])
    out = _moe(h2s, top_idx, top_w, p["w_gate"], p["b_gate"], p["w_up"], p["b_up"], p["w_down"], p["b_down"], x1, gate_f,
               p["g_post_ffn"][None], s)
    return out.reshape(b, s, d)


def kernel(x, c, w_ada, b_ada, g_pre_mix, g_post_mix, g_pre_ffn, g_post_ffn, w_in, hy_conv_w, hy_conv_b, hy_skip, hy_f_w1, hy_f_b1, hy_f_w2, hy_f_b2, hy_f_w3, hy_f_b3, hy_f_freq, hy_f_wout, w_proj_hyena, w_proj_attn, w_out, w_router, b_router, w_gate, b_gate, w_up, b_up, w_down, b_down):
    names = ("w_ada", "b_ada", "g_pre_mix", "g_post_mix", "g_pre_ffn", "g_post_ffn", "w_in", "hy_conv_w",
             "hy_conv_b", "hy_skip", "hy_f_w1", "hy_f_b1", "hy_f_w2", "hy_f_b2", "hy_f_w3", "hy_f_b3",
             "hy_f_freq", "hy_f_wout", "w_proj_hyena", "w_proj_attn", "w_out", "w_router", "b_router",
             "w_gate", "b_gate", "w_up", "b_up", "w_down", "b_down")
    stacked = (w_ada, b_ada, g_pre_mix, g_post_mix, g_pre_ffn, g_post_ffn, w_in, hy_conv_w, hy_conv_b, hy_skip,
               hy_f_w1, hy_f_b1, hy_f_w2, hy_f_b2, hy_f_w3, hy_f_b3, hy_f_freq, hy_f_wout, w_proj_hyena,
               w_proj_attn, w_out, w_router, b_router, w_gate, b_gate, w_up, b_up, w_down, b_down)
    depth = w_ada.shape[0]
    b = x.shape[0]
    c8 = jnp.zeros((8, c.shape[1]), F32).at[:b].set(c)
    for l in range(depth):
        x = _layer(x, c8, {k: v[l] for k, v in zip(names, stacked)})
    return x
```

```python
import functools
import math

import jax
import jax.numpy as jnp
import numpy as np
from jax import lax
from jax.experimental import pallas as pl
from jax.experimental.pallas import tpu as pltpu

F32 = jnp.float32
BF16 = jnp.bfloat16
HIGHEST = lax.Precision.HIGHEST

LANES = 128
HEAD_DIM = 128
HEADS_PER_GROUP = 8
DILATED_GROUPS = ((128, 1), (512, 4), (2048, 16))
N_GROUPS = len(DILATED_GROUPS)
GROUP_WIDTH = HEADS_PER_GROUP * HEAD_DIM
ATTN_SIDE = 64
TOP_K = 4
SWIGLU_LIMIT = 7.0
SWIGLU_ALPHA = 1.702
RMS_EPS = 1e-6
NEG_INF = -1e30
HYENA_N_BANDS = 16
HYENA_DECAY_TARGET = 1e-2
HYENA_FAST_DECAY_PCT = 0.3
HYENA_SLOW_DECAY_PCT = 1.5
COL_BLOCK = 1024
MOE_ROWS = 256
SLAB_PAD = 8
VMEM_LIMIT = 56 * 1024 * 1024


def _pack_bf16_pairs(x):
    bits = lax.bitcast_convert_type(x.astype(BF16).astype(F32), jnp.uint32)
    groups = [bits[:, j:j + LANES] | (bits[:, j + LANES:j + 2 * LANES] >> 16)
              for j in range(0, x.shape[1], 2 * LANES)]
    return groups[0] if len(groups) == 1 else jnp.concatenate(groups, axis=1)


def _unpack_bf16_pairs(w):
    hi = lax.bitcast_convert_type(w & jnp.uint32(0xFFFF0000), F32)
    lo = lax.bitcast_convert_type(w << 16, F32)
    return hi, lo


def _params(sem, vmem=VMEM_LIMIT):
    return pltpu.CompilerParams(dimension_semantics=sem, vmem_limit_bytes=vmem)


def _sds(shape, dtype):
    return jax.ShapeDtypeStruct(shape, dtype)


def _ada_kernel(c_ref, w_ref, b_ref, o_ref):
    c = c_ref[...]
    sc = (c * jax.nn.sigmoid(c)).astype(BF16)
    o_ref[...] = jnp.dot(sc, w_ref[...].astype(BF16), preferred_element_type=F32) + b_ref[...]


def _adaln(c8, w_ada, b_ada):
    d, cols = w_ada.shape
    tn = 512
    return pl.pallas_call(
        _ada_kernel,
        out_shape=_sds((8, cols), F32),
        grid=(cols // tn,),
        in_specs=[pl.BlockSpec((8, d), lambda j: (0, 0)),
                  pl.BlockSpec((d, tn), lambda j: (0, j)),
                  pl.BlockSpec((1, tn), lambda j: (0, j))],
        out_specs=pl.BlockSpec((8, tn), lambda j: (0, j)),
        compiler_params=_params(("arbitrary",)),
        name="adaln",
    )(c8, w_ada, b_ada)


def _prenorm_kernel(x_ref, g_ref, sc_ref, sh_ref, o_ref, o4_ref, o16_ref, scr_ref, *, ts, d_model):
    x = x_ref[...]
    ms = jnp.mean(x * x, axis=-1, keepdims=True)
    h = x * lax.rsqrt(ms + RMS_EPS) * g_ref[...] * (1.0 + sc_ref[...]) + sh_ref[...]
    o_ref[...] = h.astype(BF16)
    nct = d_model // LANES
    for c in range(nct):
        scr_ref[c] = h[:, c * LANES:(c + 1) * LANES]
    for dil, oref in ((4, o4_ref), (16, o16_ref)):
        for r in range(dil):
            for c in range(nct):
                oref[r, :, c * LANES:(c + 1) * LANES] = scr_ref[c, pl.ds(r, ts // dil, stride=dil), :].astype(BF16)


def _prenorm_mix(x, g, scale, shift):
    b, s, d = x.shape
    ts = 256
    kern = functools.partial(_prenorm_kernel, ts=ts, d_model=d)
    return pl.pallas_call(
        kern,
        out_shape=(_sds((b, s, d), BF16), _sds((b, 4, s // 4, d), BF16), _sds((b, 16, s // 16, d), BF16)),
        grid=(b, s // ts),
        in_specs=[pl.BlockSpec((None, ts, d), lambda bi, i: (bi, i, 0)),
                  pl.BlockSpec((1, d), lambda bi, i: (0, 0)),
                  pl.BlockSpec((None, 1, d), lambda bi, i: (bi, 0, 0)),
                  pl.BlockSpec((None, 1, d), lambda bi, i: (bi, 0, 0))],
        out_specs=(pl.BlockSpec((None, ts, d), lambda bi, i: (bi, i, 0)),
                   pl.BlockSpec((None, 4, ts // 4, d), lambda bi, i: (bi, 0, i, 0)),
                   pl.BlockSpec((None, 16, ts // 16, d), lambda bi, i: (bi, 0, i, 0))),
        scratch_shapes=[pltpu.VMEM((d // LANES, ts, LANES), F32)],
        compiler_params=_params(("arbitrary", "arbitrary")),
        name="prenorm_mix",
    )(x, g, scale, shift)


def _mm_kernel(tbl_ref, a_ref, w_ref, o_ref):
    del tbl_ref
    o_ref[...] = jnp.dot(a_ref[...], w_ref[...], preferred_element_type=F32).astype(o_ref.dtype)


def _matmul(a, w, col_blocks, out_dtype, name, tm=1024, tn=COL_BLOCK):
    m, k = a.shape
    tm = min(tm, m)
    nb = len(col_blocks)
    tbl = jnp.asarray(col_blocks, jnp.int32)
    grid_spec = pltpu.PrefetchScalarGridSpec(
        num_scalar_prefetch=1,
        grid=(m // tm, nb),
        in_specs=[pl.BlockSpec((tm, k), lambda i, j, t: (i, 0)),
                  pl.BlockSpec((k, tn), lambda i, j, t: (0, t[j]))],
        out_specs=pl.BlockSpec((tm, tn), lambda i, j, t: (i, j)),
    )
    return pl.pallas_call(
        _mm_kernel,
        out_shape=_sds((m, nb * tn), out_dtype),
        grid_spec=grid_spec,
        compiler_params=_params(("arbitrary", "arbitrary")),
        name=name,
    )(tbl, a, w)


def _filter_kernel(z_ref, w1_ref, b1_ref, w2_ref, b2_ref, w3_ref, b3_ref, fr_ref, wo_ref, dl_ref,
                   h_ref, *, hw, r, pitch):
    def dot(a, b):
        return jnp.dot(a, b, precision=HIGHEST, preferred_element_type=F32)

    z = z_ref[...]
    fr = fr_ref[...]
    h = jnp.sin(fr * (dot(z, w1_ref[...]) + b1_ref[...]))
    h = jnp.sin(fr * (dot(h, w2_ref[...]) + b2_ref[...]))
    h = jnp.sin(fr * (dot(h, w3_ref[...]) + b3_ref[...]))
    filt = dot(h, wo_ref[...])
    decay = jnp.exp(-z[:, 0:1] * dl_ref[...])
    hf = filt[:, :hw] * decay
    hb = filt[:, hw:] * decay
    pad = jnp.zeros((pitch - r, hw), F32)
    for g in range(z.shape[0] // r):
        h_ref[0, g * pitch:g * pitch + r, :] = hf[g * r:(g + 1) * r]
        h_ref[1, g * pitch:g * pitch + r, :] = hb[g * r:(g + 1) * r]
        h_ref[0, g * pitch + r:(g + 1) * pitch, :] = pad
        h_ref[1, g * pitch + r:(g + 1) * pitch, :] = pad


def _hyena_filters(length, r, pitch, w1, b1, w2, b2, w3, b3, freq, wout):
    emb, fw = w1.shape
    hw = wout.shape[1] // 2
    t = np.linspace(0.0, 1.0, length)[:, None]
    bands = np.linspace(1e-4, HYENA_N_BANDS - 1, HYENA_N_BANDS)[None, :]
    ang = (2.0 * math.pi / length) * np.arange(length)[:, None] * bands
    z = np.concatenate([t, np.cos(ang), -np.sin(ang)], axis=-1)
    zpad = np.zeros((length, LANES), np.float32)
    zpad[:, :emb] = z
    w1p = jnp.zeros((LANES, fw), F32).at[:emb].set(w1)
    min_decay = math.log(HYENA_DECAY_TARGET) / HYENA_FAST_DECAY_PCT
    max_decay = math.log(HYENA_DECAY_TARGET) / HYENA_SLOW_DECAY_PCT
    deltas = np.abs(np.linspace(min_decay, max_decay, hw))[None, :].astype(np.float32)
    tl = min(1024, length)
    full = lambda shape: pl.BlockSpec(shape, lambda i: (0,) * len(shape))
    return pl.pallas_call(
        functools.partial(_filter_kernel, hw=hw, r=r, pitch=pitch),
        out_shape=_sds((2, length // r * pitch, hw), F32),
        grid=(length // tl,),
        in_specs=[pl.BlockSpec((tl, LANES), lambda i: (i, 0)),
                  full((LANES, fw)), full((1, fw)), full((fw, fw)), full((1, fw)),
                  full((fw, fw)), full((1, fw)), full((1, fw)), full((fw, 2 * hw)), full((1, hw))],
        out_specs=pl.BlockSpec((2, tl // r * pitch, hw), lambda i: (0, i, 0)),
        compiler_params=_params(("arbitrary",)),
        name="hyena_filters",
    )(jnp.asarray(zpad), w1p, b1[None], w2, b2[None], w3, b3[None], freq[None], wout, jnp.asarray(deltas))


def _shortconv_kernel(u_ref, up_ref, un_ref, w_ref, b_ref, z_ref, x1_ref, *, ts, hw, r, pitch):
    i = pl.program_id(1)
    last = pl.num_programs(1) - 1
    u = u_ref[...].astype(F32)
    prev_blk = up_ref[...].astype(F32)
    next_blk = un_ref[...].astype(F32)
    prev_row = jnp.where(i > 0, prev_blk[15:16, :], 0.0)
    next_row = jnp.where(i < last, next_blk[0:1, :], 0.0)
    row = lax.broadcasted_iota(jnp.int32, u.shape, 0)
    um = jnp.where(row == 0, prev_row, pltpu.roll(u, 1, 0))
    up = jnp.where(row == ts - 1, next_row, pltpu.roll(u, ts - 1, 0))
    w = w_ref[...]
    uc = w[0:1] * um + w[1:2] * u + w[2:3] * up + b_ref[...]
    x1_ref[...] = uc[:, :hw]
    z = uc[:, 2 * hw:] * uc[:, hw:2 * hw]
    pad = jnp.zeros((pitch - r, hw), F32)
    for g in range(ts // r):
        z_ref[g * pitch:g * pitch + r, :] = z[g * r:(g + 1) * r]
        z_ref[g * pitch + r:(g + 1) * pitch, :] = pad


def _shortconv(pn3, conv_w, conv_b, hw, r, pitch):
    b, s, _ = pn3.shape
    ts = 512
    w3 = 3 * hw
    nh = s // 16
    return pl.pallas_call(
        functools.partial(_shortconv_kernel, ts=ts, hw=hw, r=r, pitch=pitch),
        out_shape=(_sds((b, s // r * pitch, hw), F32), _sds((b, s, hw), F32)),
        grid=(b, s // ts),
        in_specs=[pl.BlockSpec((None, ts, w3), lambda bi, i: (bi, i, 0)),
                  pl.BlockSpec((None, 16, w3), lambda bi, i: (bi, jnp.maximum(i * (ts // 16) - 1, 0), 0)),
                  pl.BlockSpec((None, 16, w3), lambda bi, i: (bi, jnp.minimum((i + 1) * (ts // 16), nh - 1), 0)),
                  pl.BlockSpec((3, w3), lambda bi, i: (0, 0)),
                  pl.BlockSpec((1, w3), lambda bi, i: (0, 0))],
        out_specs=(pl.BlockSpec((None, ts // r * pitch, hw), lambda bi, i: (bi, i, 0)),
                   pl.BlockSpec((None, ts, hw), lambda bi, i: (bi, i, 0))),
        compiler_params=_params(("arbitrary", "arbitrary")),
        name="hyena_shortconv",
    )(pn3, pn3, pn3, conv_w, conv_b[None])


FFT_GROUP = 4


def _split_bf16(a):
    hi = a.astype(BF16)
    return hi, (a - hi.astype(F32)).astype(BF16)


def _dot3(fh, fl, x):
    xh, xl = _split_bf16(x)
    d = lambda a, b: jnp.dot(a, b, preferred_element_type=F32)
    return d(fh, xh) + (d(fh, xl) + d(fl, xh))


def _fft_kept(r):
    return (r // 2 + 1 + 7) // 8 * 8


def _dft_tables(r):
    idx = np.arange(r)
    ang = 2.0 * np.pi * np.outer(idx, idx) / r
    cos, sin = np.cos(ang), np.sin(ang)
    kp = _fft_kept(r)
    fa_half = np.concatenate([cos[:kp, :r // 2], -sin[:kp, :r // 2]], axis=0)
    fbig = np.block([[cos, sin], [-sin, cos]])
    fconj = np.block([[cos, -sin], [sin, cos]])
    wgt = np.zeros(kp)
    wgt[0] = wgt[r // 2] = 1.0
    wgt[1:r // 2] = 2.0
    gfin = np.concatenate([cos[:r // 2, :kp] * wgt, -sin[:r // 2, :kp] * wgt], axis=1)
    tang = 2.0 * np.pi * np.outer(idx, idx) / (r * r)
    tw = np.stack([np.cos(tang), -np.sin(tang)], axis=-1)
    split = lambda a: _split_bf16(jnp.asarray(a.astype(np.float32)))
    return split(fa_half), split(fbig), split(fconj), split(gfin), jnp.asarray(tw.astype(np.float32))


def _fft_a_kernel(x_ref, fh_ref, fl_ref, o_ref, *, r, k1, kp, pitch):
    fh, fl = fh_ref[...], fl_ref[...]
    pad = jnp.zeros((pitch - r, LANES), F32)
    for g in range(kp):
        o_ref[0, g * pitch + r:(g + 1) * pitch, :] = pad
        o_ref[1, g * pitch + r:(g + 1) * pitch, :] = pad

    def body(g, carry):
        n2 = g * FFT_GROUP
        xs = jnp.concatenate([x_ref[pl.ds(n2 + k, k1, stride=pitch), :] for k in range(FFT_GROUP)], axis=1)
        a = _dot3(fh, fl, xs)
        for k in range(FFT_GROUP):
            o_ref[0, pl.ds(n2 + k, kp, stride=pitch), :] = a[:kp, k * LANES:(k + 1) * LANES]
            o_ref[1, pl.ds(n2 + k, kp, stride=pitch), :] = a[kp:, k * LANES:(k + 1) * LANES]
        return carry

    lax.fori_loop(0, r // FFT_GROUP, body, 0, unroll=4)


def _fft_stage_a(x, fmat, r, pitch):
    bx, rows, c = x.shape
    k1 = rows // pitch
    kp = _fft_kept(r)
    return pl.pallas_call(
        functools.partial(_fft_a_kernel, r=r, k1=k1, kp=kp, pitch=pitch),
        out_shape=_sds((bx, 2, kp * pitch, c), F32),
        grid=(bx, c // LANES),
        in_specs=[pl.BlockSpec((None, rows, LANES), lambda b, ci: (b, 0, ci)),
                  pl.BlockSpec((2 * kp, k1), lambda b, ci: (0, 0)),
                  pl.BlockSpec((2 * kp, k1), lambda b, ci: (0, 0))],
        out_specs=pl.BlockSpec((None, 2, kp * pitch, LANES), lambda b, ci: (b, 0, 0, ci)),
        compiler_params=_params(("arbitrary", "arbitrary")),
        name="fft_stage_a",
    )(x, *fmat)


def _twiddled(a_ref, b, tr, ti, r):
    are, aim = a_ref[b, 0, :r, :], a_ref[b, 1, :r, :]
    return jnp.concatenate([are * tr - aim * ti, are * ti + aim * tr], axis=0)


def _fft_mk_kernel(a_ref, tw_ref, h0_ref, fbh_ref, fbl_ref, o_ref, *, r, scale):
    tr, ti = tw_ref[:, 0:1], tw_ref[:, 1:2]
    fbh, fbl = fbh_ref[...], fbl_ref[...]
    xf = _dot3(fbh, fbl, _twiddled(a_ref, 0, tr, ti, r))
    xb = _dot3(fbh, fbl, _twiddled(a_ref, 1, tr, ti, r))
    o_ref[0] = (xf[:r] + xb[:r] - h0_ref[...]) * scale
    o_ref[1] = (xf[r:] - xb[r:]) * scale


def _fft_filter_spectrum(a, tw, h0, fbig, r, pitch):
    c = a.shape[-1]
    kp = _fft_kept(r)
    return pl.pallas_call(
        functools.partial(_fft_mk_kernel, r=r, scale=1.0 / (r * r)),
        out_shape=_sds((2, kp * r, c), F32),
        grid=(kp,),
        in_specs=[pl.BlockSpec((2, 2, pitch, c), lambda k: (0, 0, k, 0)),
                  pl.BlockSpec((None, r, 2), lambda k: (k, 0, 0)),
                  pl.BlockSpec((1, c), lambda k: (0, 0)),
                  pl.BlockSpec((2 * r, 2 * r), lambda k: (0, 0)),
                  pl.BlockSpec((2 * r, 2 * r), lambda k: (0, 0))],
        out_specs=pl.BlockSpec((2, r, c), lambda k: (0, k, 0)),
        compiler_params=_params(("arbitrary",)),
        name="fft_filter_spectrum",
    )(a, tw, h0, *fbig)


def _fft_m_kernel(a_ref, ks_ref, tw_ref, fbh_ref, fbl_ref, fch_ref, fcl_ref, o_ref, *, r):
    tr, ti = tw_ref[:, 0:1], tw_ref[:, 1:2]
    x = _dot3(fbh_ref[...], fbl_ref[...], _twiddled(a_ref, 0, tr, ti, r))
    xre, xim = x[:r], x[r:]
    kre, kim = ks_ref[0], ks_ref[1]
    c = _dot3(fch_ref[...], fcl_ref[...],
              jnp.concatenate([xre * kre - xim * kim, xre * kim + xim * kre], axis=0))
    cre, cim = c[:r], c[r:]
    o_ref[0, :r, :] = cre * tr + cim * ti
    o_ref[1, :r, :] = cim * tr - cre * ti
    o_ref[:, r:, :] = jnp.zeros((2,) + (o_ref.shape[1] - r, o_ref.shape[2]), F32)


def _fft_stage_m(a, ks, tw, fbig, fconj, r, pitch):
    b, _, rows, c = a.shape
    a5 = a.reshape(b, 1, 2, rows, c)
    mat = pl.BlockSpec((2 * r, 2 * r), lambda k, bi: (0, 0))
    return pl.pallas_call(
        functools.partial(_fft_m_kernel, r=r),
        out_shape=_sds(a.shape, F32),
        grid=(rows // pitch, b),
        in_specs=[pl.BlockSpec((None, 1, 2, pitch, c), lambda k, bi: (bi, 0, 0, k, 0)),
                  pl.BlockSpec((2, r, c), lambda k, bi: (0, k, 0)),
                  pl.BlockSpec((None, r, 2), lambda k, bi: (k, 0, 0)),
                  mat, mat, mat, mat],
        out_specs=pl.BlockSpec((None, 2, pitch, c), lambda k, bi: (bi, 0, k, 0)),
        compiler_params=_params(("arbitrary", "arbitrary")),
        name="fft_stage_m",
    )(a5, ks, tw, *fbig, *fconj)


def _fft_f_kernel(d_ref, gh_ref, gl_ref, o_ref, *, r, kp, pitch):
    gh, gl = gh_ref[...], gl_ref[...]
    pad = jnp.zeros((pitch - r, LANES), F32)
    for g in range(r // 2):
        o_ref[g * pitch + r:(g + 1) * pitch, :] = pad

    def body(g, carry):
        n2 = g * FFT_GROUP
        dcat = jnp.concatenate(
            [jnp.concatenate([d_ref[0, pl.ds(n2 + k, kp, stride=pitch), :],
                              d_ref[1, pl.ds(n2 + k, kp, stride=pitch), :]], axis=0) for k in range(FFT_GROUP)],
            axis=1)
        y = _dot3(gh, gl, dcat)
        for k in range(FFT_GROUP):
            o_ref[pl.ds(n2 + k, r // 2, stride=pitch), :] = y[:, k * LANES:(k + 1) * LANES]
        return carry

    lax.fori_loop(0, r // FFT_GROUP, body, 0, unroll=4)


def _fft_stage_f(dmat, gfin, r, pitch):
    b, _, rows, c = dmat.shape
    kp = rows // pitch
    out_rows = r // 2 * pitch
    return pl.pallas_call(
        functools.partial(_fft_f_kernel, r=r, kp=kp, pitch=pitch),
        out_shape=_sds((b, out_rows, c), F32),
        grid=(b, c // LANES),
        in_specs=[pl.BlockSpec((None, 2, rows, LANES), lambda bi, ci: (bi, 0, 0, ci)),
                  pl.BlockSpec((r // 2, 2 * kp), lambda bi, ci: (0, 0)),
                  pl.BlockSpec((r // 2, 2 * kp), lambda bi, ci: (0, 0))],
        out_specs=pl.BlockSpec((None, out_rows, LANES), lambda bi, ci: (bi, 0, ci)),
        compiler_params=_params(("arbitrary", "arbitrary")),
        name="fft_stage_f",
    )(dmat, *gfin)


def _fft_radix(length):
    r = int(round(math.sqrt(2 * length)))
    assert r * r == 2 * length, "sequence length must make 2L a perfect square"
    return r, r + 8


def _long_conv(zin, hfb, r, pitch):
    fa_half, fbig, fconj, gfin, tw = _dft_tables(r)
    ks = _fft_filter_spectrum(_fft_stage_a(hfb, fa_half, r, pitch), tw, hfb[1, 0:1, :], fbig, r, pitch)
    a = _fft_stage_a(zin, fa_half, r, pitch)
    dmat = _fft_stage_m(a, ks, tw, fbig, fconj, r, pitch)
    return _fft_stage_f(dmat, gfin, r, pitch)


def _alibi_slopes(n_heads):
    def pow2_slopes(m):
        start = 2.0 ** (-8.0 / m)
        return [start ** (i + 1) for i in range(m)]
    base = 2 ** int(math.floor(math.log2(n_heads)))
    slopes = pow2_slopes(base)
    if base < n_heads:
        slopes = slopes + pow2_slopes(2 * base)[0::2][: n_heads - base]
    return np.array(sorted(slopes, reverse=True), dtype=np.float32)


def _attn_kernel(q_ref, k_ref, kp_ref, kn_ref, v_ref, vp_ref, vn_ref, o_ref, l_ref, *, tq, n, dil, slopes):
    i = pl.program_id(1)
    side = ATTN_SIDE
    nk = tq + 2 * side
    row = lax.broadcasted_iota(jnp.int32, (tq, nk), 0)
    col = lax.broadcasted_iota(jnp.int32, (tq, nk), 1)
    rel = jnp.abs(col - side - row)
    kglob = i * tq + col - side
    valid = (rel <= side) & (kglob >= 0) & (kglob < n)
    dist = (rel * dil).astype(F32)
    scale = HEAD_DIM ** -0.5
    for h in range(HEADS_PER_GROUP):
        hs = slice(h * HEAD_DIM, (h + 1) * HEAD_DIM)
        q = q_ref[:, hs]
        kc = jnp.concatenate([kp_ref[:, hs], k_ref[:, hs], kn_ref[:, hs]], axis=0)
        vc = jnp.concatenate([vp_ref[:, hs], v_ref[:, hs], vn_ref[:, hs]], axis=0)
        s = lax.dot_general(q, kc, (((1,), (1,)), ((), ())), preferred_element_type=F32) * scale
        s = jnp.where(valid, s - float(slopes[h]) * dist, NEG_INF)
        m = jnp.max(s, axis=-1, keepdims=True)
        p = jnp.exp(s - m)
        den = jnp.sum(p, axis=-1, keepdims=True)
        o = jnp.dot(p.astype(BF16), vc, preferred_element_type=F32) / den
        o_ref[:, hs] = o.astype(o_ref.dtype)
        l_ref[:, hs] = jnp.broadcast_to(m + jnp.log(den), (tq, HEAD_DIM))


def _dilated_attention(qkv, n, dil, slopes, col0=0):
    streams = qkv.shape[0]
    tq = min(128, n)
    side = ATTN_SIDE
    nh = n // side
    gw = GROUP_WIDTH
    main = lambda cb: pl.BlockSpec((None, tq, gw), lambda s, i: (s, i, col0 + cb))
    prev = lambda cb: pl.BlockSpec((None, side, gw),
                                   lambda s, i: (s, jnp.maximum(i * (tq // side) - 1, 0), col0 + cb))
    nxt = lambda cb: pl.BlockSpec((None, side, gw),
                                  lambda s, i: (s, jnp.minimum((i + 1) * (tq // side), nh - 1), col0 + cb))
    return pl.pallas_call(
        functools.partial(_attn_kernel, tq=tq, n=n, dil=dil, slopes=tuple(float(v) for v in slopes)),
        out_shape=(_sds((streams, n, gw), BF16), _sds((streams, n, gw), F32)),
        grid=(streams, n // tq),
        in_specs=[main(0), main(1), prev(1), nxt(1), main(2), prev(2), nxt(2)],
        out_specs=(pl.BlockSpec((None, tq, gw), lambda s, i: (s, i, 0)),
                   pl.BlockSpec((None, tq, gw), lambda s, i: (s, i, 0))),
        compiler_params=_params(("arbitrary", "arbitrary")),
        name=f"dilated_attention_d{dil}",
    )(qkv, qkv, qkv, qkv, qkv, qkv, qkv)


def _merge_kernel(y_ref, z_ref, x1_ref, skip_ref, o0_ref, l0_ref, o1_ref, l1_ref, o2_ref, l2_ref,
                  ghy_ref, gat_ref, wh_ref, wa_ref, out_ref, hy_s, at_s, so1, sl1, so2, sl2, *, tm, tn, r, pitch):
    nct = GROUP_WIDTH // LANES
    for g in range(tm // r):
        rows = slice(g * pitch, g * pitch + r)
        hy_s[g * r:(g + 1) * r, :] = ((y_ref[rows, :] + z_ref[rows, :] * skip_ref[...])
                                      * x1_ref[g * r:(g + 1) * r, :]).astype(BF16)
    for dil, oref, lref, so, sl in ((4, o1_ref, l1_ref, so1, sl1), (16, o2_ref, l2_ref, so2, sl2)):
        for res in range(dil):
            for c in range(nct):
                cs = slice(c * LANES, (c + 1) * LANES)
                so[c, pl.ds(res, tm // dil, stride=dil), :] = oref[res, :, cs].astype(F32)
                sl[c, pl.ds(res, tm // dil, stride=dil), :] = lref[res, :, cs]
    for c in range(nct):
        cs = slice(c * LANES, (c + 1) * LANES)
        a0, a1, a2 = l0_ref[:, cs], sl1[c], sl2[c]
        m = jnp.maximum(jnp.maximum(a0, a1), a2)
        e0, e1, e2 = jnp.exp(a0 - m), jnp.exp(a1 - m), jnp.exp(a2 - m)
        at = (e0 * o0_ref[:, cs].astype(F32) + e1 * so1[c] + e2 * so2[c]) / (e0 + e1 + e2)
        at_s[:, cs] = at.astype(BF16)

    hy, at = hy_s[...], at_s[...]
    for j in range(out_ref.shape[1] // tn):
        cols = slice(j * tn, (j + 1) * tn)
        acc_h = jnp.dot(hy, wh_ref[:, cols], preferred_element_type=F32)
        acc_a = jnp.dot(at, wa_ref[:, cols], preferred_element_type=F32)
        out = (jax.nn.sigmoid(ghy_ref[:, cols].astype(F32)) * acc_h
               + jax.nn.sigmoid(gat_ref[:, cols].astype(F32)) * acc_a)
        out_ref[:, cols] = out.astype(out_ref.dtype)


def _merge(yconv, zin, x1c, skip, o0, l0, o1, l1, o2, l2, gates, wh, wa, seq, r, pitch):
    n_tok, hw = x1c.shape
    d_model = wh.shape[1]
    gw = GROUP_WIDTH
    tm = 256
    spb = seq // tm
    row = lambda width: pl.BlockSpec((tm, width), lambda i: (i, 0))
    prow = pl.BlockSpec((tm // r * pitch, hw), lambda i: (i, 0))
    res = lambda dil: pl.BlockSpec((None, dil, tm // dil, gw), lambda i: (i // spb, 0, i % spb, 0))
    resident = lambda rows: pl.BlockSpec((rows, d_model), lambda i: (0, 0), pipeline_mode=pl.Buffered(1))
    return pl.pallas_call(
        functools.partial(_merge_kernel, tm=tm, tn=COL_BLOCK, r=r, pitch=pitch),
        out_shape=_sds((n_tok, d_model), BF16),
        grid=(n_tok // tm,),
        in_specs=[prow, prow, row(hw), pl.BlockSpec((1, hw), lambda i: (0, 0)),
                  row(gw), row(gw), res(4), res(4), res(16), res(16),
                  pl.BlockSpec((tm, d_model), lambda i: (i, 0)),
                  pl.BlockSpec((tm, d_model), lambda i: (i, 1)),
                  resident(hw), resident(gw)],
        out_specs=pl.BlockSpec((tm, d_model), lambda i: (i, 0)),
        scratch_shapes=[pltpu.VMEM((tm, hw), BF16), pltpu.VMEM((tm, gw), BF16)]
                       + [pltpu.VMEM((gw // LANES, tm, LANES), F32)] * 4,
        compiler_params=_params(("arbitrary",)),
        name="gated_merge",
    )(yconv, zin, x1c, skip, o0, l0, o1, l1, o2, l2, gates, gates, wh, wa)


def _router_kernel(mo_ref, x_ref, gm_ref, gpost_ref, gpre_ref, sc_ref, sh_ref, wr_ref, br_ref,
                   x1_ref, h2_ref, idx_ref, tw_ref, *, ts, d_model, n_experts):
    mo = mo_ref[...]
    y = mo * lax.rsqrt(jnp.mean(mo * mo, axis=-1, keepdims=True) + RMS_EPS) * gpost_ref[...]
    x1 = x_ref[...] + gm_ref[...] * y
    x1_ref[...] = x1
    h2 = (x1 * lax.rsqrt(jnp.mean(x1 * x1, axis=-1, keepdims=True) + RMS_EPS) * gpre_ref[...]
          * (1.0 + sc_ref[...]) + sh_ref[...])
    nct = d_model // 2 // LANES
    sp = nct + SLAB_PAD
    words = _pack_bf16_pairs(h2)
    for c in range(nct):
        h2_ref[pl.ds(c, ts, stride=sp), :] = words[:, c * LANES:(c + 1) * LANES]
    for c in range(nct, sp):
        h2_ref[pl.ds(c, ts, stride=sp), :] = jnp.zeros((ts, LANES), jnp.uint32)
    h2_hi, h2_lo = _split_bf16(h2)
    dot = lambda a, b: jnp.dot(a, b, preferred_element_type=F32)
    logits = dot(h2_hi, wr_ref[0]) + (dot(h2_lo, wr_ref[0]) + dot(h2_hi, wr_ref[1])) + br_ref[...]
    lane = lax.broadcasted_iota(jnp.int32, logits.shape, 1)
    lane_f = lane.astype(F32)
    logits = jnp.where(lane < n_experts, logits, -jnp.inf)
    idx_out = jnp.zeros(logits.shape, jnp.int32)
    val_out = jnp.zeros(logits.shape, F32)
    top0 = None
    den = None
    for k in range(TOP_K):
        m = jnp.max(logits, axis=-1, keepdims=True)
        idx = jnp.min(jnp.where(logits == m, lane_f, float(LANES)), axis=-1, keepdims=True).astype(jnp.int32)
        if k == 0:
            top0 = m
        e = jnp.exp(m - top0)
        den = e if den is None else den + e
        idx_out = jnp.where(lane == k, idx, idx_out)
        val_out = jnp.where(lane == k, e, val_out)
        logits = jnp.where(lane == idx, -jnp.inf, logits)
    idx_ref[...] = idx_out
    tw_ref[...] = val_out / den


def _post_mix_and_route(mo, x, gate_m, g_post, g_pre, scale_f, shift_f, w_router, b_router):
    b, s, d = x.shape
    e = w_router.shape[1]
    ts = 256
    wr = jnp.stack(_split_bf16(jnp.zeros((d, LANES), F32).at[:, :e].set(w_router)))
    br = jnp.zeros((1, LANES), F32).at[0, :e].set(b_router)
    sp = d // 2 // LANES + SLAB_PAD
    spb = s // ts
    rowblk = lambda width: pl.BlockSpec((ts, width), lambda i: (i, 0))
    per_batch = pl.BlockSpec((None, 1, d), lambda i: (i // spb, 0, 0))
    vec = pl.BlockSpec((1, d), lambda i: (0, 0))
    n_tok = b * s
    return pl.pallas_call(
        functools.partial(_router_kernel, ts=ts, d_model=d, n_experts=e),
        out_shape=(_sds((n_tok, d), F32), _sds((n_tok * sp, LANES), jnp.uint32),
                   _sds((n_tok, LANES), jnp.int32), _sds((n_tok, LANES), F32)),
        grid=(n_tok // ts,),
        in_specs=[rowblk(d), rowblk(d), per_batch, vec, vec, per_batch, per_batch,
                  pl.BlockSpec((2, d, LANES), lambda i: (0, 0, 0)), pl.BlockSpec((1, LANES), lambda i: (0, 0))],
        out_specs=(rowblk(d), pl.BlockSpec((ts * sp, LANES), lambda i: (i, 0)), rowblk(LANES), rowblk(LANES)),
        compiler_params=_params(("arbitrary",)),
        name="post_mix_route",
    )(mo, x.reshape(n_tok, d), gate_m, g_post, g_pre, scale_f, shift_f, wr, br)


W_CHUNKS = 8


def _weight_stream(w_hbms, stage, wbuf, wsem, rc):
    n = len(w_hbms)

    def copies(e, c, st):
        rows = pl.ds(pl.multiple_of(c * rc, rc), rc)
        return [pltpu.make_async_copy(w.at[e, rows, :], stage.at[st, m], wsem.at[st]) for m, w in enumerate(w_hbms)]

    def start(e, c, st):
        for cp in copies(e, c, st):
            cp.start()

    def finish(e, lo, hi, half):
        def body(c, carry):
            st = c % 2
            for cp in copies(e, c, st):
                cp.wait()
            rows = pl.ds(pl.multiple_of(c * rc, rc), rc)
            for m in range(n):
                wbuf[n * half + m, rows, :] = stage[st, m].astype(BF16)

            @pl.when(c + 1 < W_CHUNKS)
            def _():
                start(e, c + 1, 1 - st)
            return carry
        lax.fori_loop(lo, hi, body, 0)

    return start, finish


def _stream_weights_step(i, n_used, be_ref, sch_ref, start, finish):
    half = sch_ref[5 * i]

    @pl.when((i == 0) & (n_used > 0))
    def _():
        start(be_ref[0], 0, 0)
        finish(be_ref[0], 0, W_CHUNKS, half)

    @pl.when(i < n_used)
    def _():
        nxt = sch_ref[5 * i + 1]

        @pl.when(sch_ref[5 * i + 2] == 1)
        def _():
            start(nxt, 0, 0)

        finish(nxt, sch_ref[5 * i + 3], sch_ref[5 * i + 4], 1 - half)

    return half


def _expert_up_kernel(be_ref, nu_ref, sch_ref, tok0_ref, tok1_ref, h2_hbm, wg_hbm, bg_ref, wu_hbm, bu_ref, act_ref,
                      xbuf, xb, wbuf, stage, sem, wsem, *, tb, nct, rc):
    i = pl.program_id(0)
    n_used = nu_ref[0]
    sp = nct + SLAB_PAD

    def row_copy(t, j, slot):
        return pltpu.make_async_copy(h2_hbm.at[pl.ds(pl.multiple_of(t * sp, 8), nct), :],
                                     xbuf.at[slot, pl.ds(pl.multiple_of(j * sp, 8), nct), :], sem.at[slot])

    def gather(tok_ref, slot):
        def body(j, carry):
            row_copy(tok_ref[0, j], j, slot).start()
            return carry
        lax.fori_loop(0, tb, body, 0, unroll=8)

    def wait(slot):
        pltpu.make_async_copy(h2_hbm.at[pl.ds(0, tb * nct), :], xbuf.at[slot, pl.ds(0, tb * nct), :],
                              sem.at[slot]).wait()

    @pl.when((i == 0) & (n_used > 0))
    def _():
        gather(tok0_ref, 0)

    start, finish = _weight_stream((wg_hbm, wu_hbm), stage, wbuf, wsem, rc)
    half = _stream_weights_step(i, n_used, be_ref, sch_ref, start, finish)

    @pl.when(i < n_used)
    def _():
        slot = i % 2
        wait(slot)
        for j in range(tb):
            row_copy(tok1_ref[0, j], j, 1 - slot).start()
        for c in range(nct):
            hi, lo = _unpack_bf16_pairs(xbuf[slot, pl.ds(c, tb, stride=sp), :])
            xb[:, 2 * c * LANES:(2 * c + 1) * LANES] = hi.astype(BF16)
            xb[:, (2 * c + 1) * LANES:(2 * c + 2) * LANES] = lo.astype(BF16)
        x = xb[...]
        g = jnp.dot(x, wbuf[2 * half], preferred_element_type=F32) + bg_ref[...]
        u = jnp.dot(x, wbuf[2 * half + 1], preferred_element_type=F32) + bu_ref[...]
        g = jnp.minimum(g, SWIGLU_LIMIT)
        u = jnp.clip(u, -SWIGLU_LIMIT, SWIGLU_LIMIT)
        act_ref[...] = (g * jax.nn.sigmoid(SWIGLU_ALPHA * g) * (u + 1.0)).astype(act_ref.dtype)

    @pl.when((i == n_used) & (n_used > 0))
    def _():
        wait(i % 2)

    @pl.when(i >= n_used)
    def _():
        act_ref[...] = jnp.zeros(act_ref.shape, act_ref.dtype)


def _expert_up(block_e, n_used, sched, row_tok3, h2s, wg, bg, wu, bu, nct):
    n_blocks, _, tb = row_tok3.shape
    _, d, f = wg.shape
    rc = d // W_CHUNKS
    grid_spec = pltpu.PrefetchScalarGridSpec(
        num_scalar_prefetch=3,
        grid=(n_blocks,),
        in_specs=[pl.BlockSpec((None, 1, tb), lambda i, be, nu, sc: (i, 0, 0), memory_space=pltpu.SMEM),
                  pl.BlockSpec((None, 1, tb), lambda i, be, nu, sc: (jnp.minimum(i + 1, n_blocks - 1), 0, 0),
                               memory_space=pltpu.SMEM),
                  pl.BlockSpec(memory_space=pl.ANY),
                  pl.BlockSpec(memory_space=pl.ANY),
                  pl.BlockSpec((None, 1, f), lambda i, be, nu, sc: (be[i], 0, 0)),
                  pl.BlockSpec(memory_space=pl.ANY),
                  pl.BlockSpec((None, 1, f), lambda i, be, nu, sc: (be[i], 0, 0))],
        out_specs=pl.BlockSpec((tb, f), lambda i, be, nu, sc: (i, 0)),
        scratch_shapes=[pltpu.VMEM((2, tb * (nct + SLAB_PAD), LANES), jnp.uint32), pltpu.VMEM((tb, d), BF16),
                        pltpu.VMEM((4, d, f), BF16), pltpu.VMEM((2, 2, rc, f), F32),
                        pltpu.SemaphoreType.DMA((2,)), pltpu.SemaphoreType.DMA((2,))],
    )
    return pl.pallas_call(
        functools.partial(_expert_up_kernel, tb=tb, nct=nct, rc=rc),
        out_shape=_sds((n_blocks * tb, f), BF16),
        grid_spec=grid_spec,
        compiler_params=_params(("arbitrary",)),
        name="expert_up",
    )(block_e, n_used, sched, row_tok3, row_tok3, h2s, wg, bg, wu, bu)


def _expert_down_kernel(be_ref, nu_ref, sch_ref, act_ref, wd_hbm, bd_ref, ys_ref, wbuf, stage, wsem, *, tb, nct, rc):
    i = pl.program_id(0)
    start, finish = _weight_stream((wd_hbm,), stage, wbuf, wsem, rc)
    half = _stream_weights_step(i, nu_ref[0], be_ref, sch_ref, start, finish)

    @pl.when(i < nu_ref[0])
    def _():
        act = act_ref[...]
        sp = nct + SLAB_PAD
        for c in range(nct):
            cols = slice(2 * c * LANES, (2 * c + 2) * LANES)
            y = jnp.dot(act, wbuf[half, :, cols], preferred_element_type=F32) + bd_ref[:, cols]
            ys_ref[pl.ds(c, tb, stride=sp), :] = _pack_bf16_pairs(y)
        for c in range(nct, sp):
            ys_ref[pl.ds(c, tb, stride=sp), :] = jnp.zeros((tb, LANES), jnp.uint32)

    @pl.when(i >= nu_ref[0])
    def _():
        ys_ref[...] = jnp.zeros(ys_ref.shape, ys_ref.dtype)


def _expert_down(block_e, n_used, sched, act, wd, bd, tb):
    _, f, d = wd.shape
    n_blocks = act.shape[0] // tb
    nct = d // 2 // LANES
    rc = f // W_CHUNKS
    grid_spec = pltpu.PrefetchScalarGridSpec(
        num_scalar_prefetch=3,
        grid=(n_blocks,),
        in_specs=[pl.BlockSpec((tb, f), lambda i, be, nu, sc: (i, 0)),
                  pl.BlockSpec(memory_space=pl.ANY),
                  pl.BlockSpec((None, 1, d), lambda i, be, nu, sc: (be[i], 0, 0))],
        out_specs=pl.BlockSpec((tb * (nct + SLAB_PAD), LANES), lambda i, be, nu, sc: (i, 0)),
        scratch_shapes=[pltpu.VMEM((2, f, d), BF16), pltpu.VMEM((2, 1, rc, d), F32),
                        pltpu.SemaphoreType.DMA((2,))],
    )
    return pl.pallas_call(
        functools.partial(_expert_down_kernel, tb=tb, nct=nct, rc=rc),
        out_shape=_sds((n_blocks * tb * (nct + SLAB_PAD), LANES), jnp.uint32),
        grid_spec=grid_spec,
        compiler_params=_params(("arbitrary",)),
        name="expert_down",
    )(block_e, n_used, sched, act, wd, bd)


def _combine_kernel(d0_ref, d1_ref, ys_hbm, tw_ref, x1_ref, gf_ref, gpost_ref, o_ref, buf, ff, sem, *, tc, nct):
    i = pl.program_id(0)
    last = pl.num_programs(0) - 1
    n_rows = TOP_K * tc
    sp = nct + SLAB_PAD

    def row_copy(r, j, slot):
        return pltpu.make_async_copy(ys_hbm.at[pl.ds(pl.multiple_of(r * sp, 8), nct), :],
                                     buf.at[slot, pl.ds(pl.multiple_of(j * sp, 8), nct), :], sem.at[slot])

    def gather(dref, slot):
        def body(j, carry):
            row_copy(dref[0, j], j, slot).start()
            return carry
        lax.fori_loop(0, n_rows, body, 0, unroll=8)

    def wait(slot):
        pltpu.make_async_copy(ys_hbm.at[pl.ds(0, n_rows * nct), :], buf.at[slot, pl.ds(0, n_rows * nct), :],
                              sem.at[slot]).wait()

    @pl.when(i == 0)
    def _():
        gather(d0_ref, 0)

    slot = i % 2
    wait(slot)
    for j in range(n_rows):
        row_copy(d1_ref[0, j], j, 1 - slot).start()
    wk =[tw_ref[:, k:k + 1] for k in range(TOP_K)]
    for c in range(nct):
        acc_hi = acc_lo = None
        for k in range(TOP_K):
            hi, lo = _unpack_bf16_pairs(buf[slot, pl.ds(k * tc * sp + c, tc, stride=sp), :])
            acc_hi = wk[k] * hi if acc_hi is None else acc_hi + wk[k] * hi
            acc_lo = wk[k] * lo if acc_lo is None else acc_lo + wk[k] * lo
        ff[:, 2 * c * LANES:(2 * c + 1) * LANES] = acc_hi
        ff[:, (2 * c + 1) * LANES:(2 * c + 2) * LANES] = acc_lo
    f = ff[...]
    y = f * lax.rsqrt(jnp.mean(f * f, axis=-1, keepdims=True) + RMS_EPS) * gpost_ref[...]
    o_ref[...] = x1_ref[...] + gf_ref[...] * y

    @pl.when(i == last)
    def _():
        wait(1 - slot)


def _combine(dest3, ys, top_w, x1, gate_f, g_post, seq):
    n_tok, d = x1.shape
    n_steps, _, n_rows = dest3.shape
    tc = n_rows // TOP_K
    nct = d // 2 // LANES
    spb = seq // tc
    return pl.pallas_call(
        functools.partial(_combine_kernel, tc=tc, nct=nct),
        out_shape=_sds((n_tok, d), F32),
        grid=(n_steps,),
        in_specs=[pl.BlockSpec((None, 1, n_rows), lambda i: (i, 0, 0), memory_space=pltpu.SMEM),
                  pl.BlockSpec((None, 1, n_rows), lambda i: (jnp.minimum(i + 1, n_steps - 1), 0, 0),
                               memory_space=pltpu.SMEM),
                  pl.BlockSpec(memory_space=pl.ANY),
                  pl.BlockSpec((tc, LANES), lambda i: (i, 0)),
                  pl.BlockSpec((tc, d), lambda i: (i, 0)),
                  pl.BlockSpec((None, 1, d), lambda i: (i // spb, 0, 0)),
                  pl.BlockSpec((1, d), lambda i: (0, 0))],
        out_specs=pl.BlockSpec((tc, d), lambda i: (i, 0)),
        scratch_shapes=[pltpu.VMEM((2, n_rows * (nct + SLAB_PAD), LANES), jnp.uint32), pltpu.VMEM((tc, d), F32),
                        pltpu.SemaphoreType.DMA((2,))],
        compiler_params=_params(("arbitrary",)),
        name="expert_combine",
    )(dest3, dest3, ys, top_w, x1, gate_f, g_post)


def _rank_kernel(idx_ref, tri_ref, rank_ref, cnt_ref, carry, *, tr):
    @pl.when(pl.program_id(0) == 0)
    def _():
        carry[...] = jnp.zeros(carry.shape, F32)

    idx = idx_ref[...]
    lane = lax.broadcasted_iota(jnp.int32, idx.shape, 1)
    base = carry[0:1, :]
    out = jnp.zeros(idx.shape, jnp.int32)
    for k in range(TOP_K):
        onehot = jnp.where(lane == idx[:, k:k + 1], 1.0, 0.0)
        csum = jnp.dot(tri_ref[...], onehot.astype(BF16), preferred_element_type=F32)
        rank = jnp.sum(onehot * (csum + base), axis=-1, keepdims=True) - 1.0
        out = jnp.where(lane == k, rank.astype(jnp.int32), out)
        base = base + csum[tr - 1:tr, :]
    rank_ref[...] = out
    carry[...] = jnp.broadcast_to(base, carry.shape)
    cnt_ref[...] = jnp.broadcast_to(base, cnt_ref.shape).astype(jnp.int32)


def _expert_ranks(top_idx_padded):
    n_tok = top_idx_padded.shape[0]
    tr = 512
    tri = jnp.asarray(np.tril(np.ones((tr, tr), np.float32)), BF16)
    return pl.pallas_call(
        functools.partial(_rank_kernel, tr=tr),
        out_shape=(_sds((n_tok, LANES), jnp.int32), _sds((8, LANES), jnp.int32)),
        grid=(n_tok // tr,),
        in_specs=[pl.BlockSpec((tr, LANES), lambda i: (i, 0)), pl.BlockSpec((tr, tr), lambda i: (0, 0))],
        out_specs=(pl.BlockSpec((tr, LANES), lambda i: (i, 0)), pl.BlockSpec((8, LANES), lambda i: (0, 0))),
        scratch_shapes=[pltpu.VMEM((8, LANES), F32)],
        compiler_params=_params(("arbitrary",)),
        name="expert_ranks",
    )(top_idx_padded, tri)


def _routing_tables(top_idx_padded, n_experts, tb):
    n_tok = top_idx_padded.shape[0]
    n_assign = n_tok * TOP_K
    ranks, counts = _expert_ranks(top_idx_padded)
    sizes = counts[0, :n_experts]
    padded = (sizes + tb - 1) // tb * tb
    pad_end = jnp.cumsum(padded)
    pad_start = pad_end - padded
    top_idx = top_idx_padded[:, :TOP_K]
    experts = jnp.arange(n_experts, dtype=jnp.int32)
    start_of = jnp.sum(jnp.where(top_idx[:, :, None] == experts, pad_start, 0), axis=-1)
    dest = (start_of + ranks[:, :TOP_K]).astype(jnp.int32).reshape(-1)
    n_rows = -(-n_assign // tb) * tb + n_experts * tb
    n_blocks = n_rows // tb
    tok = (jnp.arange(n_assign, dtype=jnp.int32) // TOP_K)
    row_tok = jnp.zeros((n_rows,), jnp.int32).at[dest].set(tok, unique_indices=True, mode="promise_in_bounds")
    block_start = jnp.arange(n_blocks, dtype=jnp.int32) * tb
    block_e = jnp.minimum(jnp.sum((pad_end[None, :] <= block_start[:, None]).astype(jnp.int32), axis=1),
                          n_experts - 1).astype(jnp.int32)
    n_used = (pad_end[-1] // tb).astype(jnp.int32).reshape(1)
    nb = padded // tb
    occupied = nb > 0
    ordinal = jnp.cumsum(occupied.astype(jnp.int32)) - 1
    later = lax.cummin(jnp.where(occupied, experts, n_experts), reverse=True)
    next_e = jnp.concatenate([later[1:], jnp.full((1,), n_experts, jnp.int32)])
    blk = jnp.arange(n_blocks, dtype=jnp.int32)
    k = blk - (pad_start // tb)[block_e]
    nbe = nb[block_e]
    active = (blk < n_used[0]) & (next_e[block_e] < n_experts)
    zero = jnp.zeros_like(blk)
    spread = jnp.maximum(nbe - 1, 1)
    lo = jnp.where(nbe > 1, jnp.maximum(k - 1, 0) * W_CHUNKS // spread, 0)
    hi = jnp.where(nbe > 1, k * W_CHUNKS // spread, W_CHUNKS)
    sched = jnp.stack([ordinal[block_e] % 2,
                       jnp.where(active, next_e[block_e], zero),
                       jnp.where(active & (k == 0), 1, zero),
                       jnp.where(active, lo, zero),
                       jnp.where(active, hi, zero)], axis=1).reshape(-1).astype(jnp.int32)
    return dest, row_tok, block_e, n_used, sched, n_rows


def _moe(h2s, top_idx, top_w, wg, bg, wu, bu, wd, bd, x1, gate_f, g_post, seq):
    n_tok, d = x1.shape
    n_experts = wg.shape[0]
    tb = MOE_ROWS
    nct = d // 2 // LANES
    dest, row_tok, block_e, n_used, sched, n_rows = _routing_tables(top_idx, n_experts, tb)
    n_blocks = n_rows // tb
    act = _expert_up(block_e, n_used, sched, row_tok.reshape(n_blocks, 1, tb), h2s, wg, bg[:, None, :], wu,
                     bu[:, None, :], nct)
    ys = _expert_down(block_e, n_used, sched, act, wd, bd[:, None, :], tb)
    tc = 128
    dest3 = dest.reshape(n_tok // tc, tc, TOP_K).transpose(0, 2, 1).reshape(n_tok // tc, 1, TOP_K * tc)
    return _combine(dest3, ys, top_w, x1, gate_f, g_post, seq)


def _layer(x, c8, p):
    b, s, d = x.shape
    n_tok = b * s
    hw = p["hy_skip"].shape[0]
    gw = GROUP_WIDTH
    assert hw == COL_BLOCK and gw == COL_BLOCK and d % COL_BLOCK == 0

    mod = _adaln(c8, p["w_ada"], p["b_ada"][None])[:b]
    shift_m, scale_m, gate_m, shift_f, scale_f, gate_f = [m[:, None, :] for m in jnp.split(mod, 6, axis=-1)]

    h, h4, h16 = _prenorm_mix(x, p["g_pre_mix"][None], scale_m, shift_m)

    w_in = p["w_in"].astype(BF16)
    nd = d // COL_BLOCK
    pn = _matmul(h.reshape(n_tok, d), w_in, [0, 1, 2, 3, 6, 9], BF16, "in_proj_natural")
    gates = _matmul(h.reshape(n_tok, d), w_in, list(range(12, 12 + 2 * nd)), BF16, "in_proj_gates")
    qkv1 = _matmul(h4.reshape(n_tok, d), w_in, [4, 7, 10], BF16, "in_proj_dil4")
    qkv2 = _matmul(h16.reshape(n_tok, d), w_in, [5, 8, 11], BF16, "in_proj_dil16")

    r, pitch = _fft_radix(s)
    hfb = _hyena_filters(s, r, pitch, p["hy_f_w1"], p["hy_f_b1"], p["hy_f_w2"], p["hy_f_b2"], p["hy_f_w3"],
                         p["hy_f_b3"], p["hy_f_freq"], p["hy_f_wout"])
    zin, x1c = _shortconv(pn.reshape(b, s, -1), p["hy_conv_w"], p["hy_conv_b"], hw, r, pitch)
    yconv = _long_conv(zin, hfb, r, pitch)

    slopes = _alibi_slopes(N_GROUPS * HEADS_PER_GROUP).reshape(N_GROUPS, HEADS_PER_GROUP)
    o0, l0 = _dilated_attention(pn.reshape(b, s, -1), s, 1, slopes[0], col0=3)
    o1, l1 = _dilated_attention(qkv1.reshape(b * 4, s // 4, 3 * gw), s // 4, 4, slopes[1])
    o2, l2 = _dilated_attention(qkv2.reshape(b * 16, s // 16, 3 * gw), s // 16, 16, slopes[2])

    merged = _merge(yconv.reshape(-1, hw), zin.reshape(-1, hw), x1c.reshape(n_tok, hw), p["hy_skip"][None],
                    o0.reshape(n_tok, gw), l0.reshape(n_tok, gw),
                    o1.reshape(b, 4, s // 4, gw), l1.reshape(b, 4, s // 4, gw),
                    o2.reshape(b, 16, s // 16, gw), l2.reshape(b, 16, s // 16, gw),
                    gates, p["w_proj_hyena"].astype(BF16), p["w_proj_attn"].astype(BF16), s, r, pitch)
    mo = _matmul(merged, p["w_out"].astype(BF16), list(range(nd)), F32, "out_proj")

    x1, h2s, top_idx, top_w = _post_mix_and_route(mo, x, gate_m, p["g_post_mix"][None], p["g_pre_ffn"][None],
                                                  scale_f, shift_f, p["w_router"], p["b_router"])
    out = _moe(h2s, top_idx, top_w, p["w_gate"], p["b_gate"], p["w_up"], p["b_up"], p["w_down"], p["b_down"], x1, gate_f,
               p["g_post_ffn"][None], s)
    return out.reshape(b, s, d)


def kernel(x, c, w_ada, b_ada, g_pre_mix, g_post_mix, g_pre_ffn, g_post_ffn, w_in, hy_conv_w, hy_conv_b, hy_skip, hy_f_w1, hy_f_b1, hy_f_w2, hy_f_b2, hy_f_w3, hy_f_b3, hy_f_freq, hy_f_wout, w_proj_hyena, w_proj_attn, w_out, w_router, b_router, w_gate, b_gate, w_up, b_up, w_down, b_down):
    names = ("w_ada", "b_ada", "g_pre_mix", "g_post_mix", "g_pre_ffn", "g_post_ffn", "w_in", "hy_conv_w",
             "hy_conv_b", "hy_skip", "hy_f_w1", "hy_f_b1", "hy_f_w2", "hy_f_b2", "hy_f_w3", "hy_f_b3",
             "hy_f_freq", "hy_f_wout", "w_proj_hyena", "w_proj_attn", "w_out", "w_router", "b_router",
             "w_gate", "b_gate", "w_up", "b_up", "w_down", "b_down")
    stacked = (w_ada, b_ada, g_pre_mix, g_post_mix, g_pre_ffn, g_post_ffn, w_in, hy_conv_w, hy_conv_b, hy_skip,
               hy_f_w1, hy_f_b1, hy_f_w2, hy_f_b2, hy_f_w3, hy_f_b3, hy_f_freq, hy_f_wout, w_proj_hyena,
               w_proj_attn, w_out, w_router, b_router, w_gate, b_gate, w_up, b_up, w_down, b_down)
    depth = w_ada.shape[0]
    b = x.shape[0]
    c8 = jnp.zeros((8, c.shape[1]), F32).at[:b].set(c)
    for l in range(depth):
        x = _layer(x, c8, {k: v[l] for k, v in zip(names, stacked)})
    return x
```

```python
import functools
import math

import jax
import jax.numpy as jnp
import numpy as np
from jax import lax
from jax.experimental import pallas as pl
from jax.experimental.pallas import tpu as pltpu

F32 = jnp.float32
BF16 = jnp.bfloat16
HIGHEST = lax.Precision.HIGHEST

LANES = 128
HEAD_DIM = 128
HEADS_PER_GROUP = 8
DILATED_GROUPS = ((128, 1), (512, 4), (2048, 16))
N_GROUPS = len(DILATED_GROUPS)
GROUP_WIDTH = HEADS_PER_GROUP * HEAD_DIM
ATTN_SIDE = 64
TOP_K = 4
SWIGLU_LIMIT = 7.0
SWIGLU_ALPHA = 1.702
RMS_EPS = 1e-6
NEG_INF = -1e30
HYENA_N_BANDS = 16
HYENA_DECAY_TARGET = 1e-2
HYENA_FAST_DECAY_PCT = 0.3
HYENA_SLOW_DECAY_PCT = 1.5
COL_BLOCK = 1024
MOE_ROWS = 256
SLAB_PAD = 8
VMEM_LIMIT = 56 * 1024 * 1024


def _pack_bf16_pairs(x):
    bits = lax.bitcast_convert_type(x.astype(BF16).astype(F32), jnp.uint32)
    groups = [bits[:, j:j + LANES] | (bits[:, j + LANES:j + 2 * LANES] >> 16)
              for j in range(0, x.shape[1], 2 * LANES)]
    return groups[0] if len(groups) == 1 else jnp.concatenate(groups, axis=1)


def _unpack_bf16_pairs(w):
    hi = lax.bitcast_convert_type(w & jnp.uint32(0xFFFF0000), F32)
    lo = lax.bitcast_convert_type(w << 16, F32)
    return hi, lo


def _params(sem, vmem=VMEM_LIMIT):
    return pltpu.CompilerParams(dimension_semantics=sem, vmem_limit_bytes=vmem)


def _sds(shape, dtype):
    return jax.ShapeDtypeStruct(shape, dtype)


def _ada_kernel(c_ref, w_ref, b_ref, o_ref):
    c = c_ref[...]
    sc = (c * jax.nn.sigmoid(c)).astype(BF16)
    o_ref[...] = jnp.dot(sc, w_ref[...].astype(BF16), preferred_element_type=F32) + b_ref[...]


def _adaln(c8, w_ada, b_ada):
    d, cols = w_ada.shape
    tn = 512
    return pl.pallas_call(
        _ada_kernel,
        out_shape=_sds((8, cols), F32),
        grid=(cols // tn,),
        in_specs=[pl.BlockSpec((8, d), lambda j: (0, 0)),
                  pl.BlockSpec((d, tn), lambda j: (0, j)),
                  pl.BlockSpec((1, tn), lambda j: (0, j))],
        out_specs=pl.BlockSpec((8, tn), lambda j: (0, j)),
        compiler_params=_params(("arbitrary",)),
        name="adaln",
    )(c8, w_ada, b_ada)


def _prenorm_kernel(x_ref, g_ref, sc_ref, sh_ref, o_ref, o4_ref, o16_ref, scr_ref, *, ts, d_model):
    x = x_ref[...]
    ms = jnp.mean(x * x, axis=-1, keepdims=True)
    h = x * lax.rsqrt(ms + RMS_EPS) * g_ref[...] * (1.0 + sc_ref[...]) + sh_ref[...]
    o_ref[...] = h.astype(BF16)
    nct = d_model // LANES
    for c in range(nct):
        scr_ref[c] = h[:, c * LANES:(c + 1) * LANES]
    for dil, oref in ((4, o4_ref), (16, o16_ref)):
        for r in range(dil):
            for c in range(nct):
                oref[r, :, c * LANES:(c + 1) * LANES] = scr_ref[c, pl.ds(r, ts // dil, stride=dil), :].astype(BF16)


def _prenorm_mix(x, g, scale, shift):
    b, s, d = x.shape
    ts = 256
    kern = functools.partial(_prenorm_kernel, ts=ts, d_model=d)
    return pl.pallas_call(
        kern,
        out_shape=(_sds((b, s, d), BF16), _sds((b, 4, s // 4, d), BF16), _sds((b, 16, s // 16, d), BF16)),
        grid=(b, s // ts),
        in_specs=[pl.BlockSpec((None, ts, d), lambda bi, i: (bi, i, 0)),
                  pl.BlockSpec((1, d), lambda bi, i: (0, 0)),
                  pl.BlockSpec((None, 1, d), lambda bi, i: (bi, 0, 0)),
                  pl.BlockSpec((None, 1, d), lambda bi, i: (bi, 0, 0))],
        out_specs=(pl.BlockSpec((None, ts, d), lambda bi, i: (bi, i, 0)),
                   pl.BlockSpec((None, 4, ts // 4, d), lambda bi, i: (bi, 0, i, 0)),
                   pl.BlockSpec((None, 16, ts // 16, d), lambda bi, i: (bi, 0, i, 0))),
        scratch_shapes=[pltpu.VMEM((d // LANES, ts, LANES), F32)],
        compiler_params=_params(("arbitrary", "arbitrary")),
        name="prenorm_mix",
    )(x, g, scale, shift)


def _mm_kernel(tbl_ref, a_ref, w_ref, o_ref):
    del tbl_ref
    o_ref[...] = jnp.dot(a_ref[...], w_ref[...], preferred_element_type=F32).astype(o_ref.dtype)


def _matmul(a, w, col_blocks, out_dtype, name, tm=1024, tn=COL_BLOCK):
    m, k = a.shape
    tm = min(tm, m)
    nb = len(col_blocks)
    tbl = jnp.asarray(col_blocks, jnp.int32)
    grid_spec = pltpu.PrefetchScalarGridSpec(
        num_scalar_prefetch=1,
        grid=(m // tm, nb),
        in_specs=[pl.BlockSpec((tm, k), lambda i, j, t: (i, 0)),
                  pl.BlockSpec((k, tn), lambda i, j, t: (0, t[j]))],
        out_specs=pl.BlockSpec((tm, tn), lambda i, j, t: (i, j)),
    )
    return pl.pallas_call(
        _mm_kernel,
        out_shape=_sds((m, nb * tn), out_dtype),
        grid_spec=grid_spec,
        compiler_params=_params(("arbitrary", "arbitrary")),
        name=name,
    )(tbl, a, w)


def _filter_kernel(z_ref, w1_ref, b1_ref, w2_ref, b2_ref, w3_ref, b3_ref, fr_ref, wo_ref, dl_ref,
                   h_ref, *, hw, r, pitch):
    def dot(a, b):
        return jnp.dot(a, b, precision=HIGHEST, preferred_element_type=F32)

    z = z_ref[...]
    fr = fr_ref[...]
    h = jnp.sin(fr * (dot(z, w1_ref[...]) + b1_ref[...]))
    h = jnp.sin(fr * (dot(h, w2_ref[...]) + b2_ref[...]))
    h = jnp.sin(fr * (dot(h, w3_ref[...]) + b3_ref[...]))
    filt = dot(h, wo_ref[...])
    decay = jnp.exp(-z[:, 0:1] * dl_ref[...])
    hf = filt[:, :hw] * decay
    hb = filt[:, hw:] * decay
    pad = jnp.zeros((pitch - r, hw), F32)
    for g in range(z.shape[0] // r):
        h_ref[0, g * pitch:g * pitch + r, :] = hf[g * r:(g + 1) * r]
        h_ref[1, g * pitch:g * pitch + r, :] = hb[g * r:(g + 1) * r]
        h_ref[0, g * pitch + r:(g + 1) * pitch, :] = pad
        h_ref[1, g * pitch + r:(g + 1) * pitch, :] = pad


def _hyena_filters(length, r, pitch, w1, b1, w2, b2, w3, b3, freq, wout):
    emb, fw = w1.shape
    hw = wout.shape[1] // 2
    t = np.linspace(0.0, 1.0, length)[:, None]
    bands = np.linspace(1e-4, HYENA_N_BANDS - 1, HYENA_N_BANDS)[None, :]
    ang = (2.0 * math.pi / length) * np.arange(length)[:, None] * bands
    z = np.concatenate([t, np.cos(ang), -np.sin(ang)], axis=-1)
    zpad = np.zeros((length, LANES), np.float32)
    zpad[:, :emb] = z
    w1p = jnp.zeros((LANES, fw), F32).at[:emb].set(w1)
    min_decay = math.log(HYENA_DECAY_TARGET) / HYENA_FAST_DECAY_PCT
    max_decay = math.log(HYENA_DECAY_TARGET) / HYENA_SLOW_DECAY_PCT
    deltas = np.abs(np.linspace(min_decay, max_decay, hw))[None, :].astype(np.float32)
    tl = min(1024, length)
    full = lambda shape: pl.BlockSpec(shape, lambda i: (0,) * len(shape))
    return pl.pallas_call(
        functools.partial(_filter_kernel, hw=hw, r=r, pitch=pitch),
        out_shape=_sds((2, length // r * pitch, hw), F32),
        grid=(length // tl,),
        in_specs=[pl.BlockSpec((tl, LANES), lambda i: (i, 0)),
                  full((LANES, fw)), full((1, fw)), full((fw, fw)), full((1, fw)),
                  full((fw, fw)), full((1, fw)), full((1, fw)), full((fw, 2 * hw)), full((1, hw))],
        out_specs=pl.BlockSpec((2, tl // r * pitch, hw), lambda i: (0, i, 0)),
        compiler_params=_params(("arbitrary",)),
        name="hyena_filters",
    )(jnp.asarray(zpad), w1p, b1[None], w2, b2[None], w3, b3[None], freq[None], wout, jnp.asarray(deltas))


def _shortconv_kernel(u_ref, up_ref, un_ref, w_ref, b_ref, z_ref, x1_ref, *, ts, hw, r, pitch):
    i = pl.program_id(1)
    last = pl.num_programs(1) - 1
    u = u_ref[...].astype(F32)
    prev_blk = up_ref[...].astype(F32)
    next_blk = un_ref[...].astype(F32)
    prev_row = jnp.where(i > 0, prev_blk[15:16, :], 0.0)
    next_row = jnp.where(i < last, next_blk[0:1, :], 0.0)
    row = lax.broadcasted_iota(jnp.int32, u.shape, 0)
    um = jnp.where(row == 0, prev_row, pltpu.roll(u, 1, 0))
    up = jnp.where(row == ts - 1, next_row, pltpu.roll(u, ts - 1, 0))
    w = w_ref[...]
    uc = w[0:1] * um + w[1:2] * u + w[2:3] * up + b_ref[...]
    x1_ref[...] = uc[:, :hw]
    z = uc[:, 2 * hw:] * uc[:, hw:2 * hw]
    pad = jnp.zeros((pitch - r, hw), F32)
    for g in range(ts // r):
        z_ref[g * pitch:g * pitch + r, :] = z[g * r:(g + 1) * r]
        z_ref[g * pitch + r:(g + 1) * pitch, :] = pad


def _shortconv(pn3, conv_w, conv_b, hw, r, pitch):
    b, s, _ = pn3.shape
    ts = 512
    w3 = 3 * hw
    nh = s // 16
    return pl.pallas_call(
        functools.partial(_shortconv_kernel, ts=ts, hw=hw, r=r, pitch=pitch),
        out_shape=(_sds((b, s // r * pitch, hw), F32), _sds((b, s, hw), F32)),
        grid=(b, s // ts),
        in_specs=[pl.BlockSpec((None, ts, w3), lambda bi, i: (bi, i, 0)),
                  pl.BlockSpec((None, 16, w3), lambda bi, i: (bi, jnp.maximum(i * (ts // 16) - 1, 0), 0)),
                  pl.BlockSpec((None, 16, w3), lambda bi, i: (bi, jnp.minimum((i + 1) * (ts // 16), nh - 1), 0)),
                  pl.BlockSpec((3, w3), lambda bi, i: (0, 0)),
                  pl.BlockSpec((1, w3), lambda bi, i: (0, 0))],
        out_specs=(pl.BlockSpec((None, ts // r * pitch, hw), lambda bi, i: (bi, i, 0)),
                   pl.BlockSpec((None, ts, hw), lambda bi, i: (bi, i, 0))),
        compiler_params=_params(("arbitrary", "arbitrary")),
        name="hyena_shortconv",
    )(pn3, pn3, pn3, conv_w, conv_b[None])


FFT_GROUP = 4


def _split_bf16(a):
    hi = a.astype(BF16)
    return hi, (a - hi.astype(F32)).astype(BF16)


def _dot3(fh, fl, x):
    xh, xl = _split_bf16(x)
    d = lambda a, b: jnp.dot(a, b, preferred_element_type=F32)
    return d(fh, xh) + (d(fh, xl) + d(fl, xh))


def _fft_kept(r):
    return (r // 2 + 1 + 7) // 8 * 8


def _dft_tables(r):
    idx = np.arange(r)
    ang = 2.0 * np.pi * np.outer(idx, idx) / r
    cos, sin = np.cos(ang), np.sin(ang)
    kp = _fft_kept(r)
    fa_half = np.concatenate([cos[:kp, :r // 2], -sin[:kp, :r // 2]], axis=0)
    fbig = np.block([[cos, sin], [-sin, cos]])
    fconj = np.block([[cos, -sin], [sin, cos]])
    wgt = np.zeros(kp)
    wgt[0] = wgt[r // 2] = 1.0
    wgt[1:r // 2] = 2.0
    gfin = np.concatenate([cos[:r // 2, :kp] * wgt, -sin[:r // 2, :kp] * wgt], axis=1)
    tang = 2.0 * np.pi * np.outer(idx, idx) / (r * r)
    tw = np.stack([np.cos(tang), -np.sin(tang)], axis=-1)
    split = lambda a: _split_bf16(jnp.asarray(a.astype(np.float32)))
    return split(fa_half), split(fbig), split(fconj), split(gfin), jnp.asarray(tw.astype(np.float32))


def _fft_a_kernel(x_ref, fh_ref, fl_ref, o_ref, *, r, k1, kp, pitch):
    fh, fl = fh_ref[...], fl_ref[...]
    pad = jnp.zeros((pitch - r, LANES), F32)
    for g in range(kp):
        o_ref[0, g * pitch + r:(g + 1) * pitch, :] = pad
        o_ref[1, g * pitch + r:(g + 1) * pitch, :] = pad

    def body(g, carry):
        n2 = g * FFT_GROUP
        xs = jnp.concatenate([x_ref[pl.ds(n2 + k, k1, stride=pitch), :] for k in range(FFT_GROUP)], axis=1)
        a = _dot3(fh, fl, xs)
        for k in range(FFT_GROUP):
            o_ref[0, pl.ds(n2 + k, kp, stride=pitch), :] = a[:kp, k * LANES:(k + 1) * LANES]
            o_ref[1, pl.ds(n2 + k, kp, stride=pitch), :] = a[kp:, k * LANES:(k + 1) * LANES]
        return carry

    lax.fori_loop(0, r // FFT_GROUP, body, 0, unroll=4)


def _fft_stage_a(x, fmat, r, pitch):
    bx, rows, c = x.shape
    k1 = rows // pitch
    kp = _fft_kept(r)
    return pl.pallas_call(
        functools.partial(_fft_a_kernel, r=r, k1=k1, kp=kp, pitch=pitch),
        out_shape=_sds((bx, 2, kp * pitch, c), F32),
        grid=(bx, c // LANES),
        in_specs=[pl.BlockSpec((None, rows, LANES), lambda b, ci: (b, 0, ci)),
                  pl.BlockSpec((2 * kp, k1), lambda b, ci: (0, 0)),
                  pl.BlockSpec((2 * kp, k1), lambda b, ci: (0, 0))],
        out_specs=pl.BlockSpec((None, 2, kp * pitch, LANES), lambda b, ci: (b, 0, 0, ci)),
        compiler_params=_params(("arbitrary", "arbitrary")),
        name="fft_stage_a",
    )(x, *fmat)


def _twiddled(a_ref, b, tr, ti, r):
    are, aim = a_ref[b, 0, :r, :], a_ref[b, 1, :r, :]
    return jnp.concatenate([are * tr - aim * ti, are * ti + aim * tr], axis=0)


def _fft_mk_kernel(a_ref, tw_ref, h0_ref, fbh_ref, fbl_ref, o_ref, *, r, scale):
    tr, ti = tw_ref[:, 0:1], tw_ref[:, 1:2]
    fbh, fbl = fbh_ref[...], fbl_ref[...]
    xf = _dot3(fbh, fbl, _twiddled(a_ref, 0, tr, ti, r))
    xb = _dot3(fbh, fbl, _twiddled(a_ref, 1, tr, ti, r))
    o_ref[0] = (xf[:r] + xb[:r] - h0_ref[...]) * scale
    o_ref[1] = (xf[r:] - xb[r:]) * scale


def _fft_filter_spectrum(a, tw, h0, fbig, r, pitch):
    c = a.shape[-1]
    kp = _fft_kept(r)
    return pl.pallas_call(
        functools.partial(_fft_mk_kernel, r=r, scale=1.0 / (r * r)),
        out_shape=_sds((2, kp * r, c), F32),
        grid=(kp,),
        in_specs=[pl.BlockSpec((2, 2, pitch, c), lambda k: (0, 0, k, 0)),
                  pl.BlockSpec((None, r, 2), lambda k: (k, 0, 0)),
                  pl.BlockSpec((1, c), lambda k: (0, 0)),
                  pl.BlockSpec((2 * r, 2 * r), lambda k: (0, 0)),
                  pl.BlockSpec((2 * r, 2 * r), lambda k: (0, 0))],
        out_specs=pl.BlockSpec((2, r, c), lambda k: (0, k, 0)),
        compiler_params=_params(("arbitrary",)),
        name="fft_filter_spectrum",
    )(a, tw, h0, *fbig)


def _fft_m_kernel(a_ref, ks_ref, tw_ref, fbh_ref, fbl_ref, fch_ref, fcl_ref, o_ref, *, r):
    tr, ti = tw_ref[:, 0:1], tw_ref[:, 1:2]
    x = _dot3(fbh_ref[...], fbl_ref[...], _twiddled(a_ref, 0, tr, ti, r))
    xre, xim = x[:r], x[r:]
    kre, kim = ks_ref[0], ks_ref[1]
    c = _dot3(fch_ref[...], fcl_ref[...],
              jnp.concatenate([xre * kre - xim * kim, xre * kim + xim * kre], axis=0))
    cre, cim = c[:r], c[r:]
    o_ref[0, :r, :] = cre * tr + cim * ti
    o_ref[1, :r, :] = cim * tr - cre * ti
    o_ref[:, r:, :] = jnp.zeros((2,) + (o_ref.shape[1] - r, o_ref.shape[2]), F32)


def _fft_stage_m(a, ks, tw, fbig, fconj, r, pitch):
    b, _, rows, c = a.shape
    a5 = a.reshape(b, 1, 2, rows, c)
    mat = pl.BlockSpec((2 * r, 2 * r), lambda k, bi: (0, 0))
    return pl.pallas_call(
        functools.partial(_fft_m_kernel, r=r),
        out_shape=_sds(a.shape, F32),
        grid=(rows // pitch, b),
        in_specs=[pl.BlockSpec((None, 1, 2, pitch, c), lambda k, bi: (bi, 0, 0, k, 0)),
                  pl.BlockSpec((2, r, c), lambda k, bi: (0, k, 0)),
                  pl.BlockSpec((None, r, 2), lambda k, bi: (k, 0, 0)),
                  mat, mat, mat, mat],
        out_specs=pl.BlockSpec((None, 2, pitch, c), lambda k, bi: (bi, 0, k, 0)),
        compiler_params=_params(("arbitrary", "arbitrary")),
        name="fft_stage_m",
    )(a5, ks, tw, *fbig, *fconj)


def _fft_f_kernel(d_ref, gh_ref, gl_ref, o_ref, *, r, kp, pitch):
    gh, gl = gh_ref[...], gl_ref[...]
    pad = jnp.zeros((pitch - r, LANES), F32)
    for g in range(r // 2):
        o_ref[g * pitch + r:(g + 1) * pitch, :] = pad

    def body(g, carry):
        n2 = g * FFT_GROUP
        dcat = jnp.concatenate(
            [jnp.concatenate([d_ref[0, pl.ds(n2 + k, kp, stride=pitch), :],
                              d_ref[1, pl.ds(n2 + k, kp, stride=pitch), :]], axis=0) for k in range(FFT_GROUP)],
            axis=1)
        y = _dot3(gh, gl, dcat)
        for k in range(FFT_GROUP):
            o_ref[pl.ds(n2 + k, r // 2, stride=pitch), :] = y[:, k * LANES:(k + 1) * LANES]
        return carry

    lax.fori_loop(0, r // FFT_GROUP, body, 0, unroll=4)


def _fft_stage_f(dmat, gfin, r, pitch):
    b, _, rows, c = dmat.shape
    kp = rows // pitch
    out_rows = r // 2 * pitch
    return pl.pallas_call(
        functools.partial(_fft_f_kernel, r=r, kp=kp, pitch=pitch),
        out_shape=_sds((b, out_rows, c), F32),
        grid=(b, c // LANES),
        in_specs=[pl.BlockSpec((None, 2, rows, LANES), lambda bi, ci: (bi, 0, 0, ci)),
                  pl.BlockSpec((r // 2, 2 * kp), lambda bi, ci: (0, 0)),
                  pl.BlockSpec((r // 2, 2 * kp), lambda bi, ci: (0, 0))],
        out_specs=pl.BlockSpec((None, out_rows, LANES), lambda bi, ci: (bi, 0, ci)),
        compiler_params=_params(("arbitrary", "arbitrary")),
        name="fft_stage_f",
    )(dmat, *gfin)


def _fft_radix(length):
    r = int(round(math.sqrt(2 * length)))
    assert r * r == 2 * length, "sequence length must make 2L a perfect square"
    return r, r + 8


def _long_conv(zin, hfb, r, pitch):
    fa_half, fbig, fconj, gfin, tw = _dft_tables(r)
    ks = _fft_filter_spectrum(_fft_stage_a(hfb, fa_half, r, pitch), tw, hfb[1, 0:1, :], fbig, r, pitch)
    a = _fft_stage_a(zin, fa_half, r, pitch)
    dmat = _fft_stage_m(a, ks, tw, fbig, fconj, r, pitch)
    return _fft_stage_f(dmat, gfin, r, pitch)


def _alibi_slopes(n_heads):
    def pow2_slopes(m):
        start = 2.0 ** (-8.0 / m)
        return [start ** (i + 1) for i in range(m)]
    base = 2 ** int(math.floor(math.log2(n_heads)))
    slopes = pow2_slopes(base)
    if base < n_heads:
        slopes = slopes + pow2_slopes(2 * base)[0::2][: n_heads - base]
    return np.array(sorted(slopes, reverse=True), dtype=np.float32)


def _attn_kernel(q_ref, k_ref, kp_ref, kn_ref, v_ref, vp_ref, vn_ref, o_ref, l_ref, *, tq, n, dil, slopes):
    i = pl.program_id(1)
    side = ATTN_SIDE
    nk = tq + 2 * side
    row = lax.broadcasted_iota(jnp.int32, (tq, nk), 0)
    col = lax.broadcasted_iota(jnp.int32, (tq, nk), 1)
    rel = jnp.abs(col - side - row)
    kglob = i * tq + col - side
    valid = (rel <= side) & (kglob >= 0) & (kglob < n)
    dist = (rel * dil).astype(F32)
    scale = HEAD_DIM ** -0.5
    for h in range(HEADS_PER_GROUP):
        hs = slice(h * HEAD_DIM, (h + 1) * HEAD_DIM)
        q = q_ref[:, hs]
        kc = jnp.concatenate([kp_ref[:, hs], k_ref[:, hs], kn_ref[:, hs]], axis=0)
        vc = jnp.concatenate([vp_ref[:, hs], v_ref[:, hs], vn_ref[:, hs]], axis=0)
        s = lax.dot_general(q, kc, (((1,), (1,)), ((), ())), preferred_element_type=F32) * scale
        s = jnp.where(valid, s - float(slopes[h]) * dist, NEG_INF)
        m = jnp.max(s, axis=-1, keepdims=True)
        p = jnp.exp(s - m)
        den = jnp.sum(p, axis=-1, keepdims=True)
        o = jnp.dot(p.astype(BF16), vc, preferred_element_type=F32) / den
        o_ref[:, hs] = o.astype(o_ref.dtype)
        l_ref[:, hs] = jnp.broadcast_to(m + jnp.log(den), (tq, HEAD_DIM))


def _dilated_attention(qkv, n, dil, slopes, col0=0):
    streams = qkv.shape[0]
    tq = min(128, n)
    side = ATTN_SIDE
    nh = n // side
    gw = GROUP_WIDTH
    main = lambda cb: pl.BlockSpec((None, tq, gw), lambda s, i: (s, i, col0 + cb))
    prev = lambda cb: pl.BlockSpec((None, side, gw),
                                   lambda s, i: (s, jnp.maximum(i * (tq // side) - 1, 0), col0 + cb))
    nxt = lambda cb: pl.BlockSpec((None, side, gw),
                                  lambda s, i: (s, jnp.minimum((i + 1) * (tq // side), nh - 1), col0 + cb))
    return pl.pallas_call(
        functools.partial(_attn_kernel, tq=tq, n=n, dil=dil, slopes=tuple(float(v) for v in slopes)),
        out_shape=(_sds((streams, n, gw), BF16), _sds((streams, n, gw), F32)),
        grid=(streams, n // tq),
        in_specs=[main(0), main(1), prev(1), nxt(1), main(2), prev(2), nxt(2)],
        out_specs=(pl.BlockSpec((None, tq, gw), lambda s, i: (s, i, 0)),
                   pl.BlockSpec((None, tq, gw), lambda s, i: (s, i, 0))),
        compiler_params=_params(("arbitrary", "arbitrary")),
        name=f"dilated_attention_d{dil}",
    )(qkv, qkv, qkv, qkv, qkv, qkv, qkv)


def _merge_kernel(y_ref, z_ref, x1_ref, skip_ref, o0_ref, l0_ref, o1_ref, l1_ref, o2_ref, l2_ref,
                  ghy_ref, gat_ref, wh_ref, wa_ref, out_ref, hy_s, at_s, so1, sl1, so2, sl2, *, tm, tn, r, pitch):
    nct = GROUP_WIDTH // LANES
    for g in range(tm // r):
        rows = slice(g * pitch, g * pitch + r)
        hy_s[g * r:(g + 1) * r, :] = ((y_ref[rows, :] + z_ref[rows, :] * skip_ref[...])
                                      * x1_ref[g * r:(g + 1) * r, :]).astype(BF16)
    for dil, oref, lref, so, sl in ((4, o1_ref, l1_ref, so1, sl1), (16, o2_ref, l2_ref, so2, sl2)):
        for res in range(dil):
            for c in range(nct):
                cs = slice(c * LANES, (c + 1) * LANES)
                so[c, pl.ds(res, tm // dil, stride=dil), :] = oref[res, :, cs].astype(F32)
                sl[c, pl.ds(res, tm // dil, stride=dil), :] = lref[res, :, cs]
    for c in range(nct):
        cs = slice(c * LANES, (c + 1) * LANES)
        a0, a1, a2 = l0_ref[:, cs], sl1[c], sl2[c]
        m = jnp.maximum(jnp.maximum(a0, a1), a2)
        e0, e1, e2 = jnp.exp(a0 - m), jnp.exp(a1 - m), jnp.exp(a2 - m)
        at = (e0 * o0_ref[:, cs].astype(F32) + e1 * so1[c] + e2 * so2[c]) / (e0 + e1 + e2)
        at_s[:, cs] = at.astype(BF16)

    hy, at = hy_s[...], at_s[...]
    for j in range(out_ref.shape[1] // tn):
        cols = slice(j * tn, (j + 1) * tn)
        acc_h = jnp.dot(hy, wh_ref[:, cols], preferred_element_type=F32)
        acc_a = jnp.dot(at, wa_ref[:, cols], preferred_element_type=F32)
        out = (jax.nn.sigmoid(ghy_ref[:, cols].astype(F32)) * acc_h
               + jax.nn.sigmoid(gat_ref[:, cols].astype(F32)) * acc_a)
        out_ref[:, cols] = out.astype(out_ref.dtype)


def _merge(yconv, zin, x1c, skip, o0, l0, o1, l1, o2, l2, gates, wh, wa, seq, r, pitch):
    n_tok, hw = x1c.shape
    d_model = wh.shape[1]
    gw = GROUP_WIDTH
    tm = 256
    spb = seq // tm
    row = lambda width: pl.BlockSpec((tm, width), lambda i: (i, 0))
    prow = pl.BlockSpec((tm // r * pitch, hw), lambda i: (i, 0))
    res = lambda dil: pl.BlockSpec((None, dil, tm // dil, gw), lambda i: (i // spb, 0, i % spb, 0))
    resident = lambda rows: pl.BlockSpec((rows, d_model), lambda i: (0, 0), pipeline_mode=pl.Buffered(1))
    return pl.pallas_call(
        functools.partial(_merge_kernel, tm=tm, tn=COL_BLOCK, r=r, pitch=pitch),
        out_shape=_sds((n_tok, d_model), BF16),
        grid=(n_tok // tm,),
        in_specs=[prow, prow, row(hw), pl.BlockSpec((1, hw), lambda i: (0, 0)),
                  row(gw), row(gw), res(4), res(4), res(16), res(16),
                  pl.BlockSpec((tm, d_model), lambda i: (i, 0)),
                  pl.BlockSpec((tm, d_model), lambda i: (i, 1)),
                  resident(hw), resident(gw)],
        out_specs=pl.BlockSpec((tm, d_model), lambda i: (i, 0)),
        scratch_shapes=[pltpu.VMEM((tm, hw), BF16), pltpu.VMEM((tm, gw), BF16)]
                       + [pltpu.VMEM((gw // LANES, tm, LANES), F32)] * 4,
        compiler_params=_params(("arbitrary",)),
        name="gated_merge",
    )(yconv, zin, x1c, skip, o0, l0, o1, l1, o2, l2, gates, gates, wh, wa)


def _router_kernel(mo_ref, x_ref, gm_ref, gpost_ref, gpre_ref, sc_ref, sh_ref, wr_ref, br_ref,
                   x1_ref, h2_ref, idx_ref, tw_ref, *, ts, d_model, n_experts):
    mo = mo_ref[...]
    y = mo * lax.rsqrt(jnp.mean(mo * mo, axis=-1, keepdims=True) + RMS_EPS) * gpost_ref[...]
    x1 = x_ref[...] + gm_ref[...] * y
    x1_ref[...] = x1
    h2 = (x1 * lax.rsqrt(jnp.mean(x1 * x1, axis=-1, keepdims=True) + RMS_EPS) * gpre_ref[...]
          * (1.0 + sc_ref[...]) + sh_ref[...])
    nct = d_model // 2 // LANES
    sp = nct + SLAB_PAD
    words = _pack_bf16_pairs(h2)
    for c in range(nct):
        h2_ref[pl.ds(c, ts, stride=sp), :] = words[:, c * LANES:(c + 1) * LANES]
    for c in range(nct, sp):
        h2_ref[pl.ds(c, ts, stride=sp), :] = jnp.zeros((ts, LANES), jnp.uint32)
    h2_hi, h2_lo = _split_bf16(h2)
    dot = lambda a, b: jnp.dot(a, b, preferred_element_type=F32)
    logits = dot(h2_hi, wr_ref[0]) + (dot(h2_lo, wr_ref[0]) + dot(h2_hi, wr_ref[1])) + br_ref[...]
    lane = lax.broadcasted_iota(jnp.int32, logits.shape, 1)
    lane_f = lane.astype(F32)
    logits = jnp.where(lane < n_experts, logits, -jnp.inf)
    idx_out = jnp.zeros(logits.shape, jnp.int32)
    val_out = jnp.zeros(logits.shape, F32)
    top0 = None
    den = None
    for k in range(TOP_K):
        m = jnp.max(logits, axis=-1, keepdims=True)
        idx = jnp.min(jnp.where(logits == m, lane_f, float(LANES)), axis=-1, keepdims=True).astype(jnp.int32)
        if k == 0:
            top0 = m
        e = jnp.exp(m - top0)
        den = e if den is None else den + e
        idx_out = jnp.where(lane == k, idx, idx_out)
        val_out = jnp.where(lane == k, e, val_out)
        logits = jnp.where(lane == idx, -jnp.inf, logits)
    idx_ref[...] = idx_out
    tw_ref[...] = val_out / den


def _post_mix_and_route(mo, x, gate_m, g_post, g_pre, scale_f, shift_f, w_router, b_router):
    b, s, d = x.shape
    e = w_router.shape[1]
    ts = 256
    wr = jnp.stack(_split_bf16(jnp.zeros((d, LANES), F32).at[:, :e].set(w_router)))
    br = jnp.zeros((1, LANES), F32).at[0, :e].set(b_router)
    sp = d // 2 // LANES + SLAB_PAD
    spb = s // ts
    rowblk = lambda width: pl.BlockSpec((ts, width), lambda i: (i, 0))
    per_batch = pl.BlockSpec((None, 1, d), lambda i: (i // spb, 0, 0))
    vec = pl.BlockSpec((1, d), lambda i: (0, 0))
    n_tok = b * s
    return pl.pallas_call(
        functools.partial(_router_kernel, ts=ts, d_model=d, n_experts=e),
        out_shape=(_sds((n_tok, d), F32), _sds((n_tok * sp, LANES), jnp.uint32),
                   _sds((n_tok, LANES), jnp.int32), _sds((n_tok, LANES), F32)),
        grid=(n_tok // ts,),
        in_specs=[rowblk(d), rowblk(d), per_batch, vec, vec, per_batch, per_batch,
                  pl.BlockSpec((2, d, LANES), lambda i: (0, 0, 0)), pl.BlockSpec((1, LANES), lambda i: (0, 0))],
        out_specs=(rowblk(d), pl.BlockSpec((ts * sp, LANES), lambda i: (i, 0)), rowblk(LANES), rowblk(LANES)),
        compiler_params=_params(("arbitrary",)),
        name="post_mix_route",
    )(mo, x.reshape(n_tok, d), gate_m, g_post, g_pre, scale_f, shift_f, wr, br)


W_CHUNKS = 8


def _weight_stream(w_hbms, stage, wbuf, wsem, rc):
    n = len(w_hbms)

    def copies(e, c, st):
        rows = pl.ds(pl.multiple_of(c * rc, rc), rc)
        return [pltpu.make_async_copy(w.at[e, rows, :], stage.at[st, m], wsem.at[st]) for m, w in enumerate(w_hbms)]

    def start(e, c, st):
        for cp in copies(e, c, st):
            cp.start(priority=1)

    def finish(e, lo, hi, half):
        def body(c, carry):
            st = c % 2
            for cp in copies(e, c, st):
                cp.wait()
            rows = pl.ds(pl.multiple_of(c * rc, rc), rc)
            for m in range(n):
                wbuf[n * half + m, rows, :] = stage[st, m].astype(BF16)

            @pl.when(c + 1 < W_CHUNKS)
            def _():
                start(e, c + 1, 1 - st)
            return carry
        lax.fori_loop(lo, hi, body, 0)

    return start, finish


def _stream_weights_step(i, n_used, be_ref, sch_ref, start, finish):
    half = sch_ref[5 * i]

    @pl.when((i == 0) & (n_used > 0))
    def _():
        start(be_ref[0], 0, 0)
        finish(be_ref[0], 0, W_CHUNKS, half)

    @pl.when(i < n_used)
    def _():
        nxt = sch_ref[5 * i + 1]

        @pl.when(sch_ref[5 * i + 2] == 1)
        def _():
            start(nxt, 0, 0)

        finish(nxt, sch_ref[5 * i + 3], sch_ref[5 * i + 4], 1 - half)

    return half


def _expert_up_kernel(be_ref, nu_ref, sch_ref, tok0_ref, tok1_ref, h2_hbm, wg_hbm, bg_ref, wu_hbm, bu_ref, act_ref,
                      xbuf, xb, wbuf, stage, sem, wsem, *, tb, nct, rc):
    i = pl.program_id(0)
    n_used = nu_ref[0]
    sp = nct + SLAB_PAD

    def row_copy(t, j, slot):
        return pltpu.make_async_copy(h2_hbm.at[pl.ds(pl.multiple_of(t * sp, 8), nct), :],
                                     xbuf.at[slot, pl.ds(pl.multiple_of(j * sp, 8), nct), :], sem.at[slot])

    def gather(tok_ref, slot):
        def body(j, carry):
            row_copy(tok_ref[0, j], j, slot).start()
            return carry
        lax.fori_loop(0, tb, body, 0, unroll=8)

    def wait(slot):
        pltpu.make_async_copy(h2_hbm.at[pl.ds(0, tb * nct), :], xbuf.at[slot, pl.ds(0, tb * nct), :],
                              sem.at[slot]).wait()

    @pl.when((i == 0) & (n_used > 0))
    def _():
        gather(tok0_ref, 0)

    start, finish = _weight_stream((wg_hbm, wu_hbm), stage, wbuf, wsem, rc)
    half = _stream_weights_step(i, n_used, be_ref, sch_ref, start, finish)

    @pl.when(i < n_used)
    def _():
        slot = i % 2
        wait(slot)
        for j in range(tb):
            row_copy(tok1_ref[0, j], j, 1 - slot).start()
        for c in range(nct):
            hi, lo = _unpack_bf16_pairs(xbuf[slot, pl.ds(c, tb, stride=sp), :])
            xb[:, 2 * c * LANES:(2 * c + 1) * LANES] = hi.astype(BF16)
            xb[:, (2 * c + 1) * LANES:(2 * c + 2) * LANES] = lo.astype(BF16)
        x = xb[...]
        g = jnp.dot(x, wbuf[2 * half], preferred_element_type=F32) + bg_ref[...]
        u = jnp.dot(x, wbuf[2 * half + 1], preferred_element_type=F32) + bu_ref[...]
        g = jnp.minimum(g, SWIGLU_LIMIT)
        u = jnp.clip(u, -SWIGLU_LIMIT, SWIGLU_LIMIT)
        act_ref[...] = (g * jax.nn.sigmoid(SWIGLU_ALPHA * g) * (u + 1.0)).astype(act_ref.dtype)

    @pl.when((i == n_used) & (n_used > 0))
    def _():
        wait(i % 2)

    @pl.when(i >= n_used)
    def _():
        act_ref[...] = jnp.zeros(act_ref.shape, act_ref.dtype)


def _expert_up(block_e, n_used, sched, row_tok3, h2s, wg, bg, wu, bu, nct):
    n_blocks, _, tb = row_tok3.shape
    _, d, f = wg.shape
    rc = d // W_CHUNKS
    grid_spec = pltpu.PrefetchScalarGridSpec(
        num_scalar_prefetch=3,
        grid=(n_blocks,),
        in_specs=[pl.BlockSpec((None, 1, tb), lambda i, be, nu, sc: (i, 0, 0), memory_space=pltpu.SMEM),
                  pl.BlockSpec((None, 1, tb), lambda i, be, nu, sc: (jnp.minimum(i + 1, n_blocks - 1), 0, 0),
                               memory_space=pltpu.SMEM),
                  pl.BlockSpec(memory_space=pl.ANY),
                  pl.BlockSpec(memory_space=pl.ANY),
                  pl.BlockSpec((None, 1, f), lambda i, be, nu, sc: (be[i], 0, 0)),
                  pl.BlockSpec(memory_space=pl.ANY),
                  pl.BlockSpec((None, 1, f), lambda i, be, nu, sc: (be[i], 0, 0))],
        out_specs=pl.BlockSpec((tb, f), lambda i, be, nu, sc: (i, 0)),
        scratch_shapes=[pltpu.VMEM((2, tb * (nct + SLAB_PAD), LANES), jnp.uint32), pltpu.VMEM((tb, d), BF16),
                        pltpu.VMEM((4, d, f), BF16), pltpu.VMEM((2, 2, rc, f), F32),
                        pltpu.SemaphoreType.DMA((2,)), pltpu.SemaphoreType.DMA((2,))],
    )
    return pl.pallas_call(
        functools.partial(_expert_up_kernel, tb=tb, nct=nct, rc=rc),
        out_shape=_sds((n_blocks * tb, f), BF16),
        grid_spec=grid_spec,
        compiler_params=_params(("arbitrary",)),
        name="expert_up",
    )(block_e, n_used, sched, row_tok3, row_tok3, h2s, wg, bg, wu, bu)


def _expert_down_kernel(be_ref, nu_ref, sch_ref, act_ref, wd_hbm, bd_ref, ys_ref, wbuf, stage, wsem, *, tb, nct, rc):
    i = pl.program_id(0)
    start, finish = _weight_stream((wd_hbm,), stage, wbuf, wsem, rc)
    half = _stream_weights_step(i, nu_ref[0], be_ref, sch_ref, start, finish)

    @pl.when(i < nu_ref[0])
    def _():
        act = act_ref[...]
        sp = nct + SLAB_PAD
        for c in range(nct):
            cols = slice(2 * c * LANES, (2 * c + 2) * LANES)
            y = jnp.dot(act, wbuf[half, :, cols], preferred_element_type=F32) + bd_ref[:, cols]
            ys_ref[pl.ds(c, tb, stride=sp), :] = _pack_bf16_pairs(y)
        for c in range(nct, sp):
            ys_ref[pl.ds(c, tb, stride=sp), :] = jnp.zeros((tb, LANES), jnp.uint32)

    @pl.when(i >= nu_ref[0])
    def _():
        ys_ref[...] = jnp.zeros(ys_ref.shape, ys_ref.dtype)


def _expert_down(block_e, n_used, sched, act, wd, bd, tb):
    _, f, d = wd.shape
    n_blocks = act.shape[0] // tb
    nct = d // 2 // LANES
    rc = f // W_CHUNKS
    grid_spec = pltpu.PrefetchScalarGridSpec(
        num_scalar_prefetch=3,
        grid=(n_blocks,),
        in_specs=[pl.BlockSpec((tb, f), lambda i, be, nu, sc: (i, 0)),
                  pl.BlockSpec(memory_space=pl.ANY),
                  pl.BlockSpec((None, 1, d), lambda i, be, nu, sc: (be[i], 0, 0))],
        out_specs=pl.BlockSpec((tb * (nct + SLAB_PAD), LANES), lambda i, be, nu, sc: (i, 0)),
        scratch_shapes=[pltpu.VMEM((2, f, d), BF16), pltpu.VMEM((2, 1, rc, d), F32),
                        pltpu.SemaphoreType.DMA((2,))],
    )
    return pl.pallas_call(
        functools.partial(_expert_down_kernel, tb=tb, nct=nct, rc=rc),
        out_shape=_sds((n_blocks * tb * (nct + SLAB_PAD), LANES), jnp.uint32),
        grid_spec=grid_spec,
        compiler_params=_params(("arbitrary",)),
        name="expert_down",
    )(block_e, n_used, sched, act, wd, bd)


def _combine_kernel(d0_ref, d1_ref, ys_hbm, tw_ref, x1_ref, gf_ref, gpost_ref, o_ref, buf, ff, sem, *, tc, nct):
    i = pl.program_id(0)
    last = pl.num_programs(0) - 1
    n_rows = TOP_K * tc
    sp = nct + SLAB_PAD

    def row_copy(r, j, slot):
        return pltpu.make_async_copy(ys_hbm.at[pl.ds(pl.multiple_of(r * sp, 8), nct), :],
                                     buf.at[slot, pl.ds(pl.multiple_of(j * sp, 8), nct), :], sem.at[slot])

    def gather(dref, slot):
        def body(j, carry):
            row_copy(dref[0, j], j, slot).start()
            return carry
        lax.fori_loop(0, n_rows, body, 0, unroll=8)

    def wait(slot):
        pltpu.make_async_copy(ys_hbm.at[pl.ds(0, n_rows * nct), :], buf.at[slot, pl.ds(0, n_rows * nct), :],
                              sem.at[slot]).wait()

    @pl.when(i == 0)
    def _():
        gather(d0_ref, 0)

    slot = i % 2
    wait(slot)
    for j in range(n_rows):
        row_copy(d1_ref[0, j], j, 1 - slot).start()
    wk =[tw_ref[:, k:k + 1] for k in range(TOP_K)]
    for c in range(nct):
        acc_hi = acc_lo = None
        for k in range(TOP_K):
            hi, lo = _unpack_bf16_pairs(buf[slot, pl.ds(k * tc * sp + c, tc, stride=sp), :])
            acc_hi = wk[k] * hi if acc_hi is None else acc_hi + wk[k] * hi
            acc_lo = wk[k] * lo if acc_lo is None else acc_lo + wk[k] * lo
        ff[:, 2 * c * LANES:(2 * c + 1) * LANES] = acc_hi
        ff[:, (2 * c + 1) * LANES:(2 * c + 2) * LANES] = acc_lo
    f = ff[...]
    y = f * lax.rsqrt(jnp.mean(f * f, axis=-1, keepdims=True) + RMS_EPS) * gpost_ref[...]
    o_ref[...] = x1_ref[...] + gf_ref[...] * y

    @pl.when(i == last)
    def _():
        wait(1 - slot)


def _combine(dest3, ys, top_w, x1, gate_f, g_post, seq):
    n_tok, d = x1.shape
    n_steps, _, n_rows = dest3.shape
    tc = n_rows // TOP_K
    nct = d // 2 // LANES
    spb = seq // tc
    return pl.pallas_call(
        functools.partial(_combine_kernel, tc=tc, nct=nct),
        out_shape=_sds((n_tok, d), F32),
        grid=(n_steps,),
        in_specs=[pl.BlockSpec((None, 1, n_rows), lambda i: (i, 0, 0), memory_space=pltpu.SMEM),
                  pl.BlockSpec((None, 1, n_rows), lambda i: (jnp.minimum(i + 1, n_steps - 1), 0, 0),
                               memory_space=pltpu.SMEM),
                  pl.BlockSpec(memory_space=pl.ANY),
                  pl.BlockSpec((tc, LANES), lambda i: (i, 0)),
                  pl.BlockSpec((tc, d), lambda i: (i, 0)),
                  pl.BlockSpec((None, 1, d), lambda i: (i // spb, 0, 0)),
                  pl.BlockSpec((1, d), lambda i: (0, 0))],
        out_specs=pl.BlockSpec((tc, d), lambda i: (i, 0)),
        scratch_shapes=[pltpu.VMEM((2, n_rows * (nct + SLAB_PAD), LANES), jnp.uint32), pltpu.VMEM((tc, d), F32),
                        pltpu.SemaphoreType.DMA((2,))],
        compiler_params=_params(("arbitrary",)),
        name="expert_combine",
    )(dest3, dest3, ys, top_w, x1, gate_f, g_post)


def _rank_kernel(idx_ref, tri_ref, rank_ref, cnt_ref, carry, *, tr):
    @pl.when(pl.program_id(0) == 0)
    def _():
        carry[...] = jnp.zeros(carry.shape, F32)

    idx = idx_ref[...]
    lane = lax.broadcasted_iota(jnp.int32, idx.shape, 1)
    base = carry[0:1, :]
    out = jnp.zeros(idx.shape, jnp.int32)
    for k in range(TOP_K):
        onehot = jnp.where(lane == idx[:, k:k + 1], 1.0, 0.0)
        csum = jnp.dot(tri_ref[...], onehot.astype(BF16), preferred_element_type=F32)
        rank = jnp.sum(onehot * (csum + base), axis=-1, keepdims=True) - 1.0
        out = jnp.where(lane == k, rank.astype(jnp.int32), out)
        base = base + csum[tr - 1:tr, :]
    rank_ref[...] = out
    carry[...] = jnp.broadcast_to(base, carry.shape)
    cnt_ref[...] = jnp.broadcast_to(base, cnt_ref.shape).astype(jnp.int32)


def _expert_ranks(top_idx_padded):
    n_tok = top_idx_padded.shape[0]
    tr = 512
    tri = jnp.asarray(np.tril(np.ones((tr, tr), np.float32)), BF16)
    return pl.pallas_call(
        functools.partial(_rank_kernel, tr=tr),
        out_shape=(_sds((n_tok, LANES), jnp.int32), _sds((8, LANES), jnp.int32)),
        grid=(n_tok // tr,),
        in_specs=[pl.BlockSpec((tr, LANES), lambda i: (i, 0)), pl.BlockSpec((tr, tr), lambda i: (0, 0))],
        out_specs=(pl.BlockSpec((tr, LANES), lambda i: (i, 0)), pl.BlockSpec((8, LANES), lambda i: (0, 0))),
        scratch_shapes=[pltpu.VMEM((8, LANES), F32)],
        compiler_params=_params(("arbitrary",)),
        name="expert_ranks",
    )(top_idx_padded, tri)


def _routing_tables(top_idx_padded, n_experts, tb):
    n_tok = top_idx_padded.shape[0]
    n_assign = n_tok * TOP_K
    ranks, counts = _expert_ranks(top_idx_padded)
    sizes = counts[0, :n_experts]
    padded = (sizes + tb - 1) // tb * tb
    pad_end = jnp.cumsum(padded)
    pad_start = pad_end - padded
    top_idx = top_idx_padded[:, :TOP_K]
    experts = jnp.arange(n_experts, dtype=jnp.int32)
    start_of = jnp.sum(jnp.where(top_idx[:, :, None] == experts, pad_start, 0), axis=-1)
    dest = (start_of + ranks[:, :TOP_K]).astype(jnp.int32).reshape(-1)
    n_rows = -(-n_assign // tb) * tb + n_experts * tb
    n_blocks = n_rows // tb
    tok = (jnp.arange(n_assign, dtype=jnp.int32) // TOP_K)
    row_tok = jnp.zeros((n_rows,), jnp.int32).at[dest].set(tok, unique_indices=True, mode="promise_in_bounds")
    block_start = jnp.arange(n_blocks, dtype=jnp.int32) * tb
    block_e = jnp.minimum(jnp.sum((pad_end[None, :] <= block_start[:, None]).astype(jnp.int32), axis=1),
                          n_experts - 1).astype(jnp.int32)
    n_used = (pad_end[-1] // tb).astype(jnp.int32).reshape(1)
    nb = padded // tb
    occupied = nb > 0
    ordinal = jnp.cumsum(occupied.astype(jnp.int32)) - 1
    later = lax.cummin(jnp.where(occupied, experts, n_experts), reverse=True)
    next_e = jnp.concatenate([later[1:], jnp.full((1,), n_experts, jnp.int32)])
    blk = jnp.arange(n_blocks, dtype=jnp.int32)
    k = blk - (pad_start // tb)[block_e]
    nbe = nb[block_e]
    active = (blk < n_used[0]) & (next_e[block_e] < n_experts)
    zero = jnp.zeros_like(blk)
    spread = jnp.maximum(nbe - 1, 1)
    lo = jnp.where(nbe > 1, jnp.maximum(k - 1, 0) * W_CHUNKS // spread, 0)
    hi = jnp.where(nbe > 1, k * W_CHUNKS // spread, W_CHUNKS)
    sched = jnp.stack([ordinal[block_e] % 2,
                       jnp.where(active, next_e[block_e], zero),
                       jnp.where(active & (k == 0), 1, zero),
                       jnp.where(active, lo, zero),
                       jnp.where(active, hi, zero)], axis=1).reshape(-1).astype(jnp.int32)
    return dest, row_tok, block_e, n_used, sched, n_rows


def _moe(h2s, top_idx, top_w, wg, bg, wu, bu, wd, bd, x1, gate_f, g_post, seq):
    n_tok, d = x1.shape
    n_experts = wg.shape[0]
    tb = MOE_ROWS
    nct = d // 2 // LANES
    dest, row_tok, block_e, n_used, sched, n_rows = _routing_tables(top_idx, n_experts, tb)
    n_blocks = n_rows // tb
    act = _expert_up(block_e, n_used, sched, row_tok.reshape(n_blocks, 1, tb), h2s, wg, bg[:, None, :], wu,
                     bu[:, None, :], nct)
    ys = _expert_down(block_e, n_used, sched, act, wd, bd[:, None, :], tb)
    tc = 128
    dest3 = dest.reshape(n_tok // tc, tc, TOP_K).transpose(0, 2, 1).reshape(n_tok // tc, 1, TOP_K * tc)
    return _combine(dest3, ys, top_w, x1, gate_f, g_post, seq)


def _layer(x, c8, p):
    b, s, d = x.shape
    n_tok = b * s
    hw = p["hy_skip"].shape[0]
    gw = GROUP_WIDTH
    assert hw == COL_BLOCK and gw == COL_BLOCK and d % COL_BLOCK == 0

    mod = _adaln(c8, p["w_ada"], p["b_ada"][None])[:b]
    shift_m, scale_m, gate_m, shift_f, scale_f, gate_f = [m[:, None, :] for m in jnp.split(mod, 6, axis=-1)]

    h, h4, h16 = _prenorm_mix(x, p["g_pre_mix"][None], scale_m, shift_m)

    w_in = p["w_in"].astype(BF16)
    nd = d // COL_BLOCK
    pn = _matmul(h.reshape(n_tok, d), w_in, [0, 1, 2, 3, 6, 9], BF16, "in_proj_natural")
    gates = _matmul(h.reshape(n_tok, d), w_in, list(range(12, 12 + 2 * nd)), BF16, "in_proj_gates")
    qkv1 = _matmul(h4.reshape(n_tok, d), w_in, [4, 7, 10], BF16, "in_proj_dil4")
    qkv2 = _matmul(h16.reshape(n_tok, d), w_in, [5, 8, 11], BF16, "in_proj_dil16")

    r, pitch = _fft_radix(s)
    hfb = _hyena_filters(s, r, pitch, p["hy_f_w1"], p["hy_f_b1"], p["hy_f_w2"], p["hy_f_b2"], p["hy_f_w3"],
                         p["hy_f_b3"], p["hy_f_freq"], p["hy_f_wout"])
    zin, x1c = _shortconv(pn.reshape(b, s, -1), p["hy_conv_w"], p["hy_conv_b"], hw, r, pitch)
    yconv = _long_conv(zin, hfb, r, pitch)

    slopes = _alibi_slopes(N_GROUPS * HEADS_PER_GROUP).reshape(N_GROUPS, HEADS_PER_GROUP)
    o0, l0 = _dilated_attention(pn.reshape(b, s, -1), s, 1, slopes[0], col0=3)
    o1, l1 = _dilated_attention(qkv1.reshape(b * 4, s // 4, 3 * gw), s // 4, 4, slopes[1])
    o2, l2 = _dilated_attention(qkv2.reshape(b * 16, s // 16, 3 * gw), s // 16, 16, slopes[2])

    merged = _merge(yconv.reshape(-1, hw), zin.reshape(-1, hw), x1c.reshape(n_tok, hw), p["hy_skip"][None],
                    o0.reshape(n_tok, gw), l0.reshape(n_tok, gw),
                    o1.reshape(b, 4, s // 4, gw), l1.reshape(b, 4, s // 4, gw),
                    o2.reshape(b, 16, s // 16, gw), l2.reshape(b, 16, s // 16, gw),
                    gates, p["w_proj_hyena"].astype(BF16), p["w_proj_attn"].astype(BF16), s, r, pitch)
    mo = _matmul(merged, p["w_out"].astype(BF16), list(range(nd)), F32, "out_proj")

    x1, h2s, top_idx, top_w = _post_mix_and_route(mo, x, gate_m, p["g_post_mix"][None], p["g_pre_ffn"][None],
                                                  scale_f, shift_f, p["w_router"], p["b_router"])
    out = _moe(h2s, top_idx, top_w, p["w_gate"], p["b_gate"], p["w_up"], p["b_up"], p["w_down"], p["b_down"], x1, gate_f,
               p["g_post_ffn"][None], s)
    return out.reshape(b, s, d)


def kernel(x, c, w_ada, b_ada, g_pre_mix, g_post_mix, g_pre_ffn, g_post_ffn, w_in, hy_conv_w, hy_conv_b, hy_skip, hy_f_w1, hy_f_b1, hy_f_w2, hy_f_b2, hy_f_w3, hy_f_b3, hy_f_freq, hy_f_wout, w_proj_hyena, w_proj_attn, w_out, w_router, b_router, w_gate, b_gate, w_up, b_up, w_down, b_down):
    names = ("w_ada", "b_ada", "g_pre_mix", "g_post_mix", "g_pre_ffn", "g_post_ffn", "w_in", "hy_conv_w",
             "hy_conv_b", "hy_skip", "hy_f_w1", "hy_f_b1", "hy_f_w2", "hy_f_b2", "hy_f_w3", "hy_f_b3",
             "hy_f_freq", "hy_f_wout", "w_proj_hyena", "w_proj_attn", "w_out", "w_router", "b_router",
             "w_gate", "b_gate", "w_up", "b_up", "w_down", "b_down")
    stacked = (w_ada, b_ada, g_pre_mix, g_post_mix, g_pre_ffn, g_post_ffn, w_in, hy_conv_w, hy_conv_b, hy_skip,
               hy_f_w1, hy_f_b1, hy_f_w2, hy_f_b2, hy_f_w3, hy_f_b3, hy_f_freq, hy_f_wout, w_proj_hyena,
               w_proj_attn, w_out, w_router, b_router, w_gate, b_gate, w_up, b_up, w_down, b_down)
    depth = w_ada.shape[0]
    b = x.shape[0]
    c8 = jnp.zeros((8, c.shape[1]), F32).at[:b].set(c)
    for l in range(depth):
        x = _layer(x, c8, {k: v[l] for k, v in zip(names, stacked)})
    return x
```
